```python
import math
import jax, jax.numpy as jnp
from jax import lax
import numpy as np

D_MODEL = 2048
BATCH = 1
SEQ = 16384
DEPTH = 1

CHUNK = 64
SSM_EXPAND = 2
SSM_D_INNER = SSM_EXPAND * D_MODEL
SSM_HEAD_DIM = 64
SSM_N_HEADS = SSM_D_INNER // SSM_HEAD_DIM
SSM_N_GROUPS = 8
SSM_HEADS_PER_GROUP = SSM_N_HEADS // SSM_N_GROUPS
SSM_D_STATE = 128
SSM_CONV = 4
SSM_GN = SSM_N_GROUPS * SSM_D_STATE
SSM_CONV_DIM = SSM_D_INNER + 2 * SSM_GN
SC_DIM = D_MODEL
SC_WIDTH = 3
N_EXPERTS = 32
TOP_K = 4
D_FF = D_MODEL
SWIGLU_LIMIT = 7.0
SWIGLU_ALPHA = 1.702
MOE_BLOCK = 256
N_BRANCHES = 2
EPS = 1e-5

OFF_Z = 0
OFF_XBC = OFF_Z + SSM_D_INNER
OFF_DT = OFF_XBC + SSM_CONV_DIM
OFF_SC = OFF_DT + SSM_N_HEADS
OFF_GATE = OFF_SC + 3 * SC_DIM
D_IN_PROJ = OFF_GATE + N_BRANCHES * D_MODEL

kernel_name = "hybrid_ssd_shortconv_moe_block"


def rms_norm(x, g):
    x32 = x.astype(jnp.float32)
    y = x32 * lax.rsqrt(jnp.mean(x32 * x32, axis=-1, keepdims=True) + EPS)
    return (y * g.astype(jnp.float32)).astype(x.dtype)


def gated_group_rms_norm(y, z, g):
    b, s, d = y.shape
    v = (y.astype(jnp.float32) * jax.nn.silu(z.astype(jnp.float32)))
    v = v.reshape(b, s, SSM_N_GROUPS, d // SSM_N_GROUPS)
    v = v * lax.rsqrt(jnp.mean(v * v, axis=-1, keepdims=True) + EPS)
    return (v.reshape(b, s, d) * g.astype(jnp.float32)).astype(y.dtype)


def causal_depthwise_conv(x, w, bias=None):
    width, c = w.shape
    y = lax.conv_general_dilated(
        x, w[:, None, :].astype(x.dtype), window_strides=(1,),
        padding=[(width - 1, 0)], dimension_numbers=("NWC", "WIO", "NWC"),
        feature_group_count=c)
    if bias is not None:
        y = y + bias.astype(x.dtype)
    return y


def ssd_chunked(x, dt, a, b_mat, c_mat):
    bsz, s = x.shape[0], x.shape[1]
    nc = s // CHUNK
    g, hg, p, n = SSM_N_GROUPS, SSM_HEADS_PER_GROUP, SSM_HEAD_DIM, SSM_D_STATE
    xc = x.reshape(bsz, nc, CHUNK, g, hg, p)
    dtc = dt.reshape(bsz, nc, CHUNK, g, hg)
    bc = b_mat.reshape(bsz, nc, CHUNK, g, n)
    cc = c_mat.reshape(bsz, nc, CHUNK, g, n)
    da_cs = jnp.cumsum(dtc * a.reshape(g, hg), axis=2)
    xdt = xc * dtc[..., None]
    seg = da_cs[:, :, :, None] - da_cs[:, :, None, :]
    causal = jnp.tril(jnp.ones((CHUNK, CHUNK), dtype=bool))[:, :, None, None]
    decay = jnp.exp(jnp.where(causal, seg, -jnp.inf))
    cb = jnp.einsum("bclgn,bcsgn->bclsg", cc, bc)
    y_diag = jnp.einsum("bclsgh,bcsghp->bclghp", cb[..., None] * decay, xdt)
    decay_to_end = jnp.exp(da_cs[:, :, -1:] - da_cs)
    states = jnp.einsum("bclgn,bclgh,bclghp->bcghpn", bc, decay_to_end, xdt)
    chunk_decay = jnp.exp(da_cs[:, :, -1])

    def step(h, inp):
        st, dec = inp
        return h * dec[..., None, None] + st, h

    h0 = jnp.zeros((bsz, g, hg, p, n), dtype=states.dtype)
    _, prev = lax.scan(step, h0, (jnp.swapaxes(states, 0, 1), jnp.swapaxes(chunk_decay, 0, 1)))
    prev = jnp.swapaxes(prev, 0, 1)
    y_off = jnp.einsum("bclgn,bcghpn,bclgh->bclghp", cc, prev, jnp.exp(da_cs))
    return (y_diag + y_off).reshape(bsz, s, SSM_N_HEADS, p)


def mamba2_branch(z, xbc, dt_raw, conv_w, conv_b, dt_bias, a_log, d_skip, norm_g, w_out):
    bsz, s, _ = z.shape
    xbc = jax.nn.silu(causal_depthwise_conv(xbc, conv_w, conv_b))
    xs = xbc[..., :SSM_D_INNER]
    bm = xbc[..., SSM_D_INNER:SSM_D_INNER + SSM_GN]
    cm = xbc[..., SSM_D_INNER + SSM_GN:]
    dt = jax.nn.softplus(dt_raw.astype(jnp.float32) + dt_bias.astype(jnp.float32))
    a = -jnp.exp(a_log.astype(jnp.float32))
    xh = xs.reshape(bsz, s, SSM_N_HEADS, SSM_HEAD_DIM).astype(jnp.float32)
    y = ssd_chunked(xh, dt, a,
                    bm.reshape(bsz, s, SSM_N_GROUPS, SSM_D_STATE).astype(jnp.float32),
                    cm.reshape(bsz, s, SSM_N_GROUPS, SSM_D_STATE).astype(jnp.float32))
    y = y + d_skip.astype(jnp.float32)[:, None] * xh
    y = y.reshape(bsz, s, SSM_D_INNER).astype(z.dtype)
    y = gated_group_rms_norm(y, z, norm_g)
    return y @ w_out


def short_conv_branch(bcv, conv_w, w_out):
    b_gate = bcv[..., :SC_DIM]
    c_gate = bcv[..., SC_DIM:2 * SC_DIM]
    v = bcv[..., 2 * SC_DIM:]
    y = b_gate * causal_depthwise_conv(c_gate * v, conv_w)
    return y @ w_out


def moe_ffn(x, w_router, b_router, w_gate_up, b_gate_up, w_down, b_down):
    bsz, s, d = x.shape
    t = bsz * s
    xt = x.reshape(t, d)
    logits = xt.astype(jnp.float32) @ w_router.astype(jnp.float32) + b_router.astype(jnp.float32)
    top_val, top_idx = lax.top_k(logits, TOP_K)
    top_w = jax.nn.softmax(top_val, axis=-1).astype(x.dtype)
    tk = t * TOP_K
    e_flat = top_idx.reshape(tk).astype(jnp.int32)
    tok_flat = jnp.arange(tk, dtype=jnp.int32) // TOP_K
    w_flat = top_w.reshape(tk)
    order = jnp.argsort(e_flat)
    e_sorted = e_flat[order]
    counts = jnp.bincount(e_flat, length=N_EXPERTS).astype(jnp.int32)
    padded = (counts + MOE_BLOCK - 1) // MOE_BLOCK * MOE_BLOCK
    pad_end = jnp.cumsum(padded)
    pad_start = pad_end - padded
    sorted_start = jnp.cumsum(counts) - counts
    dest = pad_start[e_sorted] + jnp.arange(tk, dtype=jnp.int32) - sorted_start[e_sorted]
    n_blocks = -(-tk // MOE_BLOCK) + N_EXPERTS
    p = n_blocks * MOE_BLOCK
    pad_tok = jnp.full((p,), t, dtype=jnp.int32).at[dest].set(tok_flat[order])
    pad_w = jnp.zeros((p,), dtype=x.dtype).at[dest].set(w_flat[order])
    block_e = jnp.minimum(
        jnp.searchsorted(pad_end, jnp.arange(n_blocks, dtype=jnp.int32) * MOE_BLOCK, side="right"),
        N_EXPERTS - 1)
    x_ext = jnp.concatenate([xt, jnp.zeros((1, d), dtype=xt.dtype)], axis=0)
    xb = x_ext[pad_tok].reshape(n_blocks, MOE_BLOCK, d)

    def expert_block(args):
        xblk, e = args
        hgu = xblk @ w_gate_up[e] + b_gate_up[e]
        gate = jnp.minimum(hgu[:, :D_FF], SWIGLU_LIMIT)
        up = jnp.clip(hgu[:, D_FF:], -SWIGLU_LIMIT, SWIGLU_LIMIT)
        act = (up + 1.0) * gate * jax.nn.sigmoid(SWIGLU_ALPHA * gate)
        return act @ w_down[e] + b_down[e]

    yb = lax.map(expert_block, (xb, block_e)).reshape(p, d)
    y = jnp.zeros((t + 1, d), dtype=x.dtype).at[pad_tok].add(yb * pad_w[:, None])[:t]
    return y.reshape(bsz, s, d)


def setup_inputs(seed: int = 0) -> dict:
    key = jax.random.key(seed)
    ks = jax.random.split(key, 24)
    f32 = jnp.float32

    def nrm(k, shape, scale):
        return jax.random.normal(k, shape, dtype=f32) * scale

    L = DEPTH
    dt0 = jnp.exp(jax.random.uniform(ks[5], (L, SSM_N_HEADS), dtype=f32)
                  * (math.log(0.1) - math.log(0.001)) + math.log(0.001))
    dt0 = jnp.maximum(dt0, 1e-4)
    return {
        "x": nrm(ks[0], (BATCH, SEQ, D_MODEL), 1.0),
        "g_mix": 1.0 + nrm(ks[1], (L, D_MODEL), 0.02),
        "w_in": nrm(ks[2], (L, D_MODEL, D_IN_PROJ), D_MODEL ** -0.5),
        "ssm_conv_w": nrm(ks[3], (L, SSM_CONV, SSM_CONV_DIM), SSM_CONV ** -0.5),
        "ssm_conv_b": nrm(ks[4], (L, SSM_CONV_DIM), 0.01),
        "ssm_dt_bias": dt0 + jnp.log(-jnp.expm1(-dt0)),
        "ssm_a_log": jnp.log(jax.random.uniform(ks[6], (L, SSM_N_HEADS), dtype=f32, minval=1.0, maxval=16.0)),
        "ssm_d": 1.0 + nrm(ks[7], (L, SSM_N_HEADS), 0.02),
        "ssm_norm_g": 1.0 + nrm(ks[8], (L, SSM_D_INNER), 0.02),
        "w_ssm_out": nrm(ks[9], (L, SSM_D_INNER, D_MODEL), SSM_D_INNER ** -0.5),
        "sc_conv_w": nrm(ks[10], (L, SC_WIDTH, SC_DIM), SC_WIDTH ** -0.5),
        "w_sc_out": nrm(ks[11], (L, SC_DIM, D_MODEL), SC_DIM ** -0.5),
        "b_gate": nrm(ks[12], (L, N_BRANCHES * D_MODEL), 0.01),
        "w_o": nrm(ks[13], (L, D_MODEL, D_MODEL), D_MODEL ** -0.5),
        "g_ffn": 1.0 + nrm(ks[14], (L, D_MODEL), 0.02),
        "w_router": nrm(ks[15], (L, D_MODEL, N_EXPERTS), D_MODEL ** -0.5),
        "b_router": nrm(ks[16], (L, N_EXPERTS), 0.01),
        "w_gate_up": nrm(ks[17], (L, N_EXPERTS, D_MODEL, 2 * D_FF), D_MODEL ** -0.5),
        "b_gate_up": nrm(ks[18], (L, N_EXPERTS, 2 * D_FF), 0.01),
        "w_down": nrm(ks[19], (L, N_EXPERTS, D_FF, D_MODEL), D_FF ** -0.5),
        "b_down": nrm(ks[20], (L, N_EXPERTS, D_MODEL), 0.01),
        "g_final": 1.0 + nrm(ks[21], (D_MODEL,), 0.02),
    }


def reference(x, g_mix, w_in, ssm_conv_w, ssm_conv_b, ssm_dt_bias, ssm_a_log, ssm_d,
              ssm_norm_g, w_ssm_out, sc_conv_w, w_sc_out, b_gate, w_o, g_ffn, w_router,
              b_router, w_gate_up, b_gate_up, w_down, b_down, g_final):
    h = x
    for i in range(DEPTH):
        u = rms_norm(h, g_mix[i])
        proj = u @ w_in[i]
        y_ssm = mamba2_branch(proj[..., OFF_Z:OFF_XBC], proj[..., OFF_XBC:OFF_DT],
                              proj[..., OFF_DT:OFF_SC], ssm_conv_w[i], ssm_conv_b[i],
                              ssm_dt_bias[i], ssm_a_log[i], ssm_d[i], ssm_norm_g[i],
                              w_ssm_out[i])
        y_sc = short_conv_branch(proj[..., OFF_SC:OFF_GATE], sc_conv_w[i], w_sc_out[i])
        gates = jax.nn.sigmoid(proj[..., OFF_GATE:] + b_gate[i])
        mixed = gates[..., :D_MODEL] * y_ssm + gates[..., D_MODEL:] * y_sc
        h = h + mixed @ w_o[i]
        h = h + moe_ffn(rms_norm(h, g_ffn[i]), w_router[i], b_router[i], w_gate_up[i],
                        b_gate_up[i], w_down[i], b_down[i])
    return rms_norm(h, g_final)
```

```python
import functools

import jax
import jax.numpy as jnp
from jax import lax
from jax.experimental import pallas as pl
from jax.experimental.pallas import tpu as pltpu

D_MODEL = 2048
SSM_D_INNER = 2 * D_MODEL
SSM_HEAD_DIM = 64
SSM_N_HEADS = SSM_D_INNER // SSM_HEAD_DIM
SSM_N_GROUPS = 8
SSM_HEADS_PER_GROUP = SSM_N_HEADS // SSM_N_GROUPS
SSM_D_STATE = 128
SSM_CONV = 4
SSM_GN = SSM_N_GROUPS * SSM_D_STATE
SSM_CONV_DIM = SSM_D_INNER + 2 * SSM_GN
SSM_GROUP_CH = SSM_D_INNER // SSM_N_GROUPS
SC_DIM = D_MODEL
SC_WIDTH = 3
N_EXPERTS = 32
TOP_K = 4
D_FF = D_MODEL
SWIGLU_LIMIT = 7.0
SWIGLU_ALPHA = 1.702
EPS = 1e-5

OFF_Z = 0
OFF_XBC = OFF_Z + SSM_D_INNER
OFF_DT = OFF_XBC + SSM_CONV_DIM
OFF_SC = OFF_DT + SSM_N_HEADS
OFF_GATE = OFF_SC + 3 * SC_DIM
D_IN_PROJ = OFF_GATE + 2 * D_MODEL

P_Z = 0
P_XBC = P_Z + SSM_D_INNER
P_SC = P_XBC + SSM_CONV_DIM
P_GATE = P_SC + 3 * SC_DIM
P_TOTAL = P_GATE + 2 * D_MODEL

SUBLANES = 8
VMEM_LIMIT = 56 * 1024 * 1024

F32 = jnp.float32
BF16 = jnp.bfloat16
HIGHEST = lax.Precision.HIGHEST
NT_DIMS = (((1,), (1,)), ((), ()))
TN_DIMS = (((0,), (0,)), ((), ()))


def _sigmoid(v):
    return 1.0 / (1.0 + jnp.exp(-v))


def _params(semantics):
    return pltpu.CompilerParams(dimension_semantics=semantics, vmem_limit_bytes=VMEM_LIMIT)


INPROJ_TM = 1024
INPROJ_TN = 1024


def _inproj_body(x_ref, g_ref, w_ref, wdt_ref, proj_ref, dt_ref, u_ref):
    @pl.when(pl.program_id(1) == 0)
    def _():
        x = x_ref[...]
        u = x * lax.rsqrt(jnp.mean(x * x, axis=-1, keepdims=True) + EPS) * g_ref[...]
        u_ref[...] = u.astype(BF16)
        dt_ref[...] = jnp.dot(u, wdt_ref[...], precision=HIGHEST, preferred_element_type=F32)

    proj_ref[...] = jnp.dot(u_ref[...], w_ref[...], preferred_element_type=F32).astype(BF16)


def _inproj(x, g, w_cat, w_dt):
    t = x.shape[0]
    tm = min(INPROJ_TM, t)
    tn = INPROJ_TN
    return pl.pallas_call(
        _inproj_body,
        grid=(t // tm, P_TOTAL // tn),
        in_specs=[
            pl.BlockSpec((tm, D_MODEL), lambda i, j: (i, 0)),
            pl.BlockSpec((1, D_MODEL), lambda i, j: (0, 0)),
            pl.BlockSpec((D_MODEL, tn), lambda i, j: (0, j)),
            pl.BlockSpec((D_MODEL, SSM_N_HEADS), lambda i, j: (0, 0)),
        ],
        out_specs=[
            pl.BlockSpec((tm, tn), lambda i, j: (i, j)),
            pl.BlockSpec((tm, SSM_N_HEADS), lambda i, j: (i, 0)),
        ],
        out_shape=[
            jax.ShapeDtypeStruct((t, P_TOTAL), BF16),
            jax.ShapeDtypeStruct((t, SSM_N_HEADS), F32),
        ],
        scratch_shapes=[pltpu.VMEM((tm, D_MODEL), BF16)],
        compiler_params=_params(("parallel", "arbitrary")),
        name="inproj",
    )(x, g, w_cat, w_dt)


SSD_L = 256
HEAD_PAIR = 2 * SSM_HEAD_DIM


def _ssd_body(z_ref, x_ref, b_ref, c_ref, dt_ref, wx_ref, wb_ref, wc_ref, bx_ref, bb_ref, bc_ref,
              dtb_ref, alog_ref, d_ref, ng_ref, o_ref, s_ref, xbuf, bbuf, cbuf):
    L = x_ref.shape[0]
    tail = SUBLANES

    @pl.when(pl.program_id(1) == 0)
    def _():
        s_ref[...] = jnp.zeros_like(s_ref)
        xbuf[0:tail, :] = jnp.zeros((tail, xbuf.shape[1]), F32)
        bbuf[0:tail, :] = jnp.zeros((tail, bbuf.shape[1]), F32)
        cbuf[0:tail, :] = jnp.zeros((tail, cbuf.shape[1]), F32)

    def conv_silu(buf, in_ref, w_ref, bias_ref):
        buf[tail:tail + L, :] = in_ref[...].astype(F32)
        acc = bias_ref[...] + w_ref[SSM_CONV - 1:SSM_CONV, :] * buf[tail:tail + L, :]
        for k in range(SSM_CONV - 1):
            off = tail - (SSM_CONV - 1) + k
            acc = acc + w_ref[k:k + 1, :] * buf[off:off + L, :]
        buf[0:tail, :] = buf[L:L + tail, :]
        return acc * _sigmoid(acc)

    xs = conv_silu(xbuf, x_ref, wx_ref, bx_ref)
    bm = conv_silu(bbuf, b_ref, wb_ref, bb_ref).astype(BF16)
    cm = conv_silu(cbuf, c_ref, wc_ref, bc_ref).astype(BF16)

    dt_raw = dt_ref[...] + dtb_ref[...]
    dt = jnp.maximum(dt_raw, 0.0) + jnp.log(1.0 + jnp.exp(-jnp.abs(dt_raw)))
    da = dt * (-jnp.exp(alog_ref[...]))
    row = lax.broadcasted_iota(jnp.int32, (L, L), 0)
    col = lax.broadcasted_iota(jnp.int32, (L, L), 1)
    causal = row >= col
    incl = (row <= col).astype(F32)
    eye = (row == col).astype(F32)
    cs = jnp.dot(da, incl, precision=HIGHEST, preferred_element_type=F32)
    cs_t = lax.dot_general(causal.astype(F32), da, NT_DIMS, precision=HIGHEST, preferred_element_type=F32)
    dt_t = lax.dot_general(eye, dt, NT_DIMS, precision=HIGHEST, preferred_element_type=F32)
    cs_end = cs[:, L - 1:L]
    to_end_t = lax.dot_general(eye, jnp.exp(cs_end - cs), NT_DIMS, precision=HIGHEST,
                               preferred_element_type=F32)
    ecs_t = jnp.exp(cs_t)

    cb = lax.dot_general(cm, bm, NT_DIMS, preferred_element_type=F32)
    y_off = lax.dot_general(cm, s_ref[...].astype(BF16), NT_DIMS, preferred_element_type=F32)

    lane = lax.broadcasted_iota(jnp.int32, (L, HEAD_PAIR), 1)
    first = lane < SSM_HEAD_DIM
    ys = []
    xws = []
    for p in range(SSM_HEADS_PER_GROUP // 2):
        h0, h1 = 2 * p, 2 * p + 1
        sl = slice(p * HEAD_PAIR, (p + 1) * HEAD_PAIR)
        xp = xs[:, sl]
        pick = lambda a: jnp.where(first, a[:, h0:h0 + 1], a[:, h1:h1 + 1])
        xdt = xp * pick(dt_t)
        xdt_b = xdt.astype(BF16)
        yd = []
        for h in (h0, h1):
            seg = cs_t[:, h:h + 1] - cs[h:h + 1, :]
            m = cb * jnp.exp(jnp.where(causal, seg, -jnp.inf))
            yd.append(jnp.dot(m.astype(BF16), xdt_b, preferred_element_type=F32))
        y = jnp.where(first, yd[0], yd[1]) + y_off[:, sl] * pick(ecs_t)
        y = y + xp * jnp.where(first[0:1, :], d_ref[h0:h0 + 1, :], d_ref[h1:h1 + 1, :])
        ys.append(y)
        xws.append((xdt * pick(to_end_t)).astype(BF16))
    y = jnp.concatenate(ys, axis=1)
    xw = jnp.concatenate(xws, axis=1)

    upd = lax.dot_general(xw, bm, TN_DIMS, preferred_element_type=F32)
    chunk_decay = jnp.exp(cs_end)
    for h in range(SSM_HEADS_PER_GROUP):
        rows = slice(h * SSM_HEAD_DIM, (h + 1) * SSM_HEAD_DIM)
        s_ref[rows, :] = s_ref[rows, :] * chunk_decay[h:h + 1, :] + upd[rows, :]

    z = z_ref[...].astype(F32)
    v = y * (z * _sigmoid(z))
    v = v * lax.rsqrt(jnp.mean(v * v, axis=-1, keepdims=True) + EPS)
    o_ref[...] = (v * ng_ref[...]).astype(BF16)


def _ssd(proj, dt_rows, conv_w, conv_b, dt_bias, a_log, d_skip, norm_g):
    t = proj.shape[0]
    L = min(SSD_L, t)
    gc = SSM_GROUP_CH
    n = SSM_D_STATE
    zc, xc = P_Z // gc, P_XBC // gc
    bc, cc = (P_XBC + SSM_D_INNER) // n, (P_XBC + SSM_D_INNER + SSM_GN) // n
    wbc, wcc = SSM_D_INNER // n, (SSM_D_INNER + SSM_GN) // n
    hg = SSM_HEADS_PER_GROUP
    return pl.pallas_call(
        _ssd_body,
        grid=(SSM_N_GROUPS, t // L),
        in_specs=[
            pl.BlockSpec((L, gc), lambda g, i: (i, zc + g)),
            pl.BlockSpec((L, gc), lambda g, i: (i, xc + g)),
            pl.BlockSpec((L, n), lambda g, i: (i, bc + g)),
            pl.BlockSpec((L, n), lambda g, i: (i, cc + g)),
            pl.BlockSpec((hg, L), lambda g, i: (g, i)),
            pl.BlockSpec((SSM_CONV, gc), lambda g, i: (0, g)),
            pl.BlockSpec((SSM_CONV, n), lambda g, i: (0, wbc + g)),
            pl.BlockSpec((SSM_CONV, n), lambda g, i: (0, wcc + g)),
            pl.BlockSpec((1, gc), lambda g, i: (0, g)),
            pl.BlockSpec((1, n), lambda g, i: (0, wbc + g)),
            pl.BlockSpec((1, n), lambda g, i: (0, wcc + g)),
            pl.BlockSpec((hg, 1), lambda g, i: (g, 0)),
            pl.BlockSpec((hg, 1), lambda g, i: (g, 0)),
            pl.BlockSpec((hg, 1), lambda g, i: (g, 0)),
            pl.BlockSpec((1, gc), lambda g, i: (0, g)),
        ],
        out_specs=pl.BlockSpec((L, gc), lambda g, i: (i, g)),
        out_shape=jax.ShapeDtypeStruct((t, SSM_D_INNER), BF16),
        scratch_shapes=[
            pltpu.VMEM((gc, n), F32),
            pltpu.VMEM((L + SUBLANES, gc), F32),
            pltpu.VMEM((L + SUBLANES, n), F32),
            pltpu.VMEM((L + SUBLANES, n), F32),
        ],
        compiler_params=_params(("parallel", "arbitrary")),
        name="ssd",
    )(proj, proj, proj, proj, dt_rows, conv_w, conv_w, conv_w, conv_b, conv_b, conv_b,
      dt_bias, a_log, d_skip, norm_g)


MIX_TM = 256


def _mix_body(yn_ref, b_ref, c_ref, v_ref, cp_ref, vp_ref, g1_ref, g2_ref, bg1_ref, bg2_ref,
              wc_ref, wssm_ref, wsc_ref, o_ref, buf):
    tm = yn_ref.shape[0]
    tail = SUBLANES
    prev = cp_ref[...].astype(F32) * vp_ref[...].astype(F32)
    buf[0:tail, :] = jnp.where(pl.program_id(0) == 0, 0.0, prev)
    cv = c_ref[...].astype(F32) * v_ref[...].astype(F32)
    buf[tail:tail + tm, :] = cv
    conv = wc_ref[SC_WIDTH - 1:SC_WIDTH, :] * cv
    for k in range(SC_WIDTH - 1):
        off = tail - (SC_WIDTH - 1) + k
        conv = conv + wc_ref[k:k + 1, :] * buf[off:off + tm, :]
    sc_in = (b_ref[...].astype(F32) * conv).astype(BF16)
    y_sc = jnp.dot(sc_in, wsc_ref[...], preferred_element_type=F32)
    y_ssm = jnp.dot(yn_ref[...], wssm_ref[...], preferred_element_type=F32)
    g1 = _sigmoid(g1_ref[...].astype(F32) + bg1_ref[...])
    g2 = _sigmoid(g2_ref[...].astype(F32) + bg2_ref[...])
    o_ref[...] = (g1 * y_ssm + g2 * y_sc).astype(BF16)


def _resident(shape):
    return pl.BlockSpec(shape, lambda *_: (0,) * len(shape), pipeline_mode=pl.Buffered(1))


def _mix(y_norm, proj, b_gate, sc_conv_w, w_ssm_out, w_sc_out):
    t = y_norm.shape[0]
    tm = min(MIX_TM, t)
    d = D_MODEL
    sb, gb = P_SC // d, P_GATE // d
    prev_rows = lambda i: jnp.maximum(i * (tm // SUBLANES) - 1, 0)
    return pl.pallas_call(
        _mix_body,
        grid=(t // tm,),
        in_specs=[
            pl.BlockSpec((tm, SSM_D_INNER), lambda i: (i, 0)),
            pl.BlockSpec((tm, d), lambda i: (i, sb)),
            pl.BlockSpec((tm, d), lambda i: (i, sb + 1)),
            pl.BlockSpec((tm, d), lambda i: (i, sb + 2)),
            pl.BlockSpec((SUBLANES, d), lambda i: (prev_rows(i), sb + 1)),
            pl.BlockSpec((SUBLANES, d), lambda i: (prev_rows(i), sb + 2)),
            pl.BlockSpec((tm, d), lambda i: (i, gb)),
            pl.BlockSpec((tm, d), lambda i: (i, gb + 1)),
            pl.BlockSpec((1, d), lambda i: (0, 0)),
            pl.BlockSpec((1, d), lambda i: (0, 1)),
            pl.BlockSpec((SC_WIDTH, d), lambda i: (0, 0)),
            _resident((SSM_D_INNER, d)),
            _resident((d, d)),
        ],
        out_specs=pl.BlockSpec((tm, d), lambda i: (i, 0)),
        out_shape=jax.ShapeDtypeStruct((t, d), BF16),
        scratch_shapes=[pltpu.VMEM((tm + SUBLANES, d), F32)],
        compiler_params=_params(("parallel",)),
        name="mix",
    )(y_norm, proj, proj, proj, proj, proj, proj, proj, b_gate, b_gate, sc_conv_w, w_ssm_out, w_sc_out)


ROUTE_TM = 256
PACK_W = D_MODEL // 2


def _pack_rows(v):
    lo = lax.bitcast_convert_type(v[:, :PACK_W].astype(BF16).astype(F32), jnp.uint32)
    hi = lax.bitcast_convert_type(v[:, PACK_W:].astype(BF16).astype(F32), jnp.uint32)
    return (hi & jnp.uint32(0xFFFF0000)) | (lo >> 16)


def _unpack_rows(w):
    lo = lax.bitcast_convert_type(w << 16, F32).astype(BF16)
    hi = lax.bitcast_convert_type(w & jnp.uint32(0xFFFF0000), F32).astype(BF16)
    return jnp.concatenate([lo, hi], axis=1)


def _route_body(m_ref, x_ref, wo_ref, g_ref, wr_ref, br_ref,
                h_ref, xp_ref, e_ref, w_ref, r_ref, cnt_ref, carry):
    tm = m_ref.shape[0]

    @pl.when(pl.program_id(0) == 0)
    def _():
        carry[...] = jnp.zeros_like(carry)

    h = x_ref[...] + jnp.dot(m_ref[...], wo_ref[...], preferred_element_type=F32)
    h_ref[...] = h
    xn = h * lax.rsqrt(jnp.mean(h * h, axis=-1, keepdims=True) + EPS) * g_ref[...]
    xp_ref[...] = _pack_rows(xn)

    logits = lax.dot_general(wr_ref[...], xn, NT_DIMS, precision=HIGHEST, preferred_element_type=F32) + br_ref[...]
    eidx = lax.broadcasted_iota(jnp.int32, (N_EXPERTS, tm), 0)
    vals, hots = [], []
    for k in range(TOP_K):
        best = jnp.max(logits, axis=0, keepdims=True)
        arg = jnp.min(jnp.where(logits == best, eidx, N_EXPERTS), axis=0, keepdims=True)
        hot = eidx == arg
        e_ref[k:k + 1, :] = arg
        vals.append(best)
        hots.append(hot)
        logits = jnp.where(hot, -jnp.inf, logits)
    exps = [jnp.exp(v - vals[0]) for v in vals]
    denom = exps[0] + exps[1] + exps[2] + exps[3]
    for k in range(TOP_K):
        w_ref[k:k + 1, :] = exps[k] / denom

    cnt = (hots[0] | hots[1] | hots[2] | hots[3]).astype(F32)
    r_i = lax.broadcasted_iota(jnp.int32, (tm, tm), 0)
    c_i = lax.broadcasted_iota(jnp.int32, (tm, tm), 1)
    before = (r_i < c_i).astype(BF16)
    prior = carry[:, 0:1] + jnp.dot(cnt.astype(BF16), before, preferred_element_type=F32)
    for k in range(TOP_K):
        r_ref[k:k + 1, :] = jnp.sum(jnp.where(hots[k], prior, 0.0), axis=0, keepdims=True).astype(jnp.int32)
    total = carry[...] + jnp.sum(cnt, axis=1, keepdims=True)
    carry[...] = total
    cnt_ref[...] = total.astype(jnp.int32)


def _route(mixed, x, w_o, g_ffn, w_router_t, b_router):
    t = x.shape[0]
    tm = min(ROUTE_TM, t)
    d = D_MODEL
    return pl.pallas_call(
        _route_body,
        grid=(t // tm,),
        in_specs=[
            pl.BlockSpec((tm, d), lambda i: (i, 0)),
            pl.BlockSpec((tm, d), lambda i: (i, 0)),
            _resident((d, d)),
            pl.BlockSpec((1, d), lambda i: (0, 0)),
            pl.BlockSpec((N_EXPERTS, d), lambda i: (0, 0)),
            pl.BlockSpec((N_EXPERTS, 1), lambda i: (0, 0)),
        ],
        out_specs=[
            pl.BlockSpec((tm, d), lambda i: (i, 0)),
            pl.BlockSpec((tm, PACK_W), lambda i: (i, 0)),
            pl.BlockSpec((TOP_K, tm), lambda i: (0, i)),
            pl.BlockSpec((TOP_K, tm), lambda i: (0, i)),
            pl.BlockSpec((TOP_K, tm), lambda i: (0, i)),
            pl.BlockSpec((N_EXPERTS, 128), lambda i: (0, 0)),
        ],
        out_shape=[
            jax.ShapeDtypeStruct((t, d), F32),
            jax.ShapeDtypeStruct((t, PACK_W), jnp.uint32),
            jax.ShapeDtypeStruct((TOP_K, t), jnp.int32),
            jax.ShapeDtypeStruct((TOP_K, t), F32),
            jax.ShapeDtypeStruct((TOP_K, t), jnp.int32),
            jax.ShapeDtypeStruct((N_EXPERTS, 128), jnp.int32),
        ],
        scratch_shapes=[pltpu.VMEM((N_EXPERTS, 128), F32)],
        compiler_params=_params(("arbitrary",)),
        name="route",
    )(mixed, x, w_o, g_ffn, w_router_t, b_router)


MOE_BM = 256
DISPATCH_TM = 256


def _dispatch_body(dest_ref, padrow_ref, npad_ref, nused_ref, xp_ref, xs_hbm, zblk, sem, zsem):
    tm = xp_ref.shape[0]
    bm = zblk.shape[0]
    nb = xs_hbm.shape[0] // bm

    @pl.when(pl.program_id(0) == 0)
    def _():
        zblk[...] = jnp.zeros_like(zblk)
        npad = npad_ref[0]
        nused = nused_ref[0]

        def zero_row(j):
            return pltpu.make_async_copy(zblk.at[pl.ds(0, 1), :], xs_hbm.at[pl.ds(padrow_ref[j], 1), :], zsem)

        def zero_block(b):
            return pltpu.make_async_copy(zblk, xs_hbm.at[pl.ds(pl.multiple_of(b * bm, bm), bm), :], zsem)

        def start_row(j, c):
            zero_row(j).start()
            return c

        def wait_row(j, c):
            zero_row(j).wait()
            return c

        def start_block(b, c):
            zero_block(b).start()
            return c

        def wait_block(b, c):
            zero_block(b).wait()
            return c

        lax.fori_loop(0, npad, start_row, 0)
        lax.fori_loop(nused, nb, start_block, 0)
        lax.fori_loop(0, npad, wait_row, 0)
        lax.fori_loop(nused, nb, wait_block, 0)

    def row_copy(r, k):
        return pltpu.make_async_copy(xp_ref.at[pl.ds(r, 1), :],
                                     xs_hbm.at[pl.ds(dest_ref[0, 0, k * tm + r], 1), :], sem)

    def start(r, c):
        for k in range(TOP_K):
            row_copy(r, k).start()
        return c

    def wait(r, c):
        for k in range(TOP_K):
            row_copy(r, k).wait()
        return c

    lax.fori_loop(0, tm, start, 0)
    lax.fori_loop(0, tm, wait, 0)


def _dispatch(xp, dest_tiles, pad_rows, n_pad, n_used, n_rows):
    t = xp.shape[0]
    tm = dest_tiles.shape[2] // TOP_K
    return pl.pallas_call(
        _dispatch_body,
        grid=(t // tm,),
        in_specs=[
            pl.BlockSpec((1, 1, TOP_K * tm), lambda i: (i, 0, 0), memory_space=pltpu.SMEM),
            pl.BlockSpec(memory_space=pltpu.SMEM),
            pl.BlockSpec(memory_space=pltpu.SMEM),
            pl.BlockSpec(memory_space=pltpu.SMEM),
            pl.BlockSpec((tm, PACK_W), lambda i: (i, 0)),
        ],
        out_specs=pl.BlockSpec(memory_space=pl.ANY),
        out_shape=jax.ShapeDtypeStruct((n_rows, PACK_W), jnp.uint32),
        scratch_shapes=[
            pltpu.VMEM((MOE_BM, PACK_W), jnp.uint32),
            pltpu.SemaphoreType.DMA(()),
            pltpu.SemaphoreType.DMA(()),
        ],
        compiler_params=_params(("arbitrary",)),
        name="dispatch",
    )(dest_tiles, pad_rows, n_pad, n_used, xp)


FFN_TF = 512
FFN_TN = 1024


def _expert_changed(be_ref, b):
    return jnp.logical_or(b == 0, be_ref[b] != be_ref[jnp.maximum(b - 1, 0)])


def _ffn_up_body(be_ref, nb_ref, xs_ref, wg_ref, wu_ref, bg_ref, bu_ref, h_ref, wg_bf, wu_bf):
    b = pl.program_id(1)

    @pl.when(b < nb_ref[0])
    def _():
        @pl.when(_expert_changed(be_ref, b))
        def _():
            wg_bf[...] = wg_ref[...].astype(BF16)
            wu_bf[...] = wu_ref[...].astype(BF16)

        x = _unpack_rows(xs_ref[...])
        gate = jnp.dot(x, wg_bf[...], preferred_element_type=F32) + bg_ref[...]
        up = jnp.dot(x, wu_bf[...], preferred_element_type=F32) + bu_ref[...]
        gate = jnp.minimum(gate, SWIGLU_LIMIT)
        up = jnp.clip(up, -SWIGLU_LIMIT, SWIGLU_LIMIT)
        h_ref[...] = ((up + 1.0) * gate * _sigmoid(SWIGLU_ALPHA * gate)).astype(BF16)

    @pl.when(b >= nb_ref[0])
    def _():
        h_ref[...] = jnp.zeros_like(h_ref)


def _ffn_up(block_e, n_used, xs, w_gate_up, b_gate_up):
    n_rows = xs.shape[0]
    nb = n_rows // MOE_BM
    tf = FFN_TF
    nf = D_FF // tf
    live = lambda b, nbr: jnp.minimum(b, nbr[0] - 1)
    return pl.pallas_call(
        _ffn_up_body,
        grid_spec=pltpu.PrefetchScalarGridSpec(
            num_scalar_prefetch=2,
            grid=(nf, nb),
            in_specs=[
                pl.BlockSpec((MOE_BM, PACK_W), lambda j, b, be, nbr: (live(b, nbr), 0)),
                pl.BlockSpec((None, D_MODEL, tf), lambda j, b, be, nbr: (be[live(b, nbr)], 0, j)),
                pl.BlockSpec((None, D_MODEL, tf), lambda j, b, be, nbr: (be[live(b, nbr)], 0, nf + j)),
                pl.BlockSpec((None, 1, tf), lambda j, b, be, nbr: (be[live(b, nbr)], 0, j)),
                pl.BlockSpec((None, 1, tf), lambda j, b, be, nbr: (be[live(b, nbr)], 0, nf + j)),
            ],
            out_specs=pl.BlockSpec((MOE_BM, tf), lambda j, b, be, nbr: (b, j)),
            scratch_shapes=[pltpu.VMEM((D_MODEL, tf), BF16), pltpu.VMEM((D_MODEL, tf), BF16)],
        ),
        out_shape=jax.ShapeDtypeStruct((n_rows, D_FF), BF16),
        compiler_params=_params(("arbitrary", "arbitrary")),
        name="ffn_up",
    )(block_e, n_used, xs, w_gate_up, w_gate_up, b_gate_up, b_gate_up)


def _ffn_down_body(be_ref, nb_ref, h_ref, wd_ref, bd_ref, y_ref, wd_bf):
    b = pl.program_id(1)

    @pl.when(b < nb_ref[0])
    def _():
        @pl.when(_expert_changed(be_ref, b))
        def _():
            wd_bf[...] = wd_ref[...].astype(BF16)

        y_ref[...] = jnp.dot(h_ref[...], wd_bf[...], preferred_element_type=F32) + bd_ref[...]

    @pl.when(b >= nb_ref[0])
    def _():
        y_ref[...] = jnp.zeros_like(y_ref)


def _ffn_down(block_e, n_used, h, w_down, b_down):
    n_rows = h.shape[0]
    nb = n_rows // MOE_BM
    tn = FFN_TN
    live = lambda b, nbr: jnp.minimum(b, nbr[0] - 1)
    return pl.pallas_call(
        _ffn_down_body,
        grid_spec=pltpu.PrefetchScalarGridSpec(
            num_scalar_prefetch=2,
            grid=(D_MODEL // tn, nb),
            in_specs=[
                pl.BlockSpec((MOE_BM, D_FF), lambda j, b, be, nbr: (live(b, nbr), 0)),
                pl.BlockSpec((None, D_FF, tn), lambda j, b, be, nbr: (be[live(b, nbr)], 0, j)),
                pl.BlockSpec((None, 1, tn), lambda j, b, be, nbr: (be[live(b, nbr)], 0, j)),
            ],
            out_specs=pl.BlockSpec((MOE_BM, tn), lambda j, b, be, nbr: (b, j)),
            scratch_shapes=[pltpu.VMEM((D_FF, tn), BF16)],
        ),
        out_shape=jax.ShapeDtypeStruct((n_rows, D_MODEL), F32),
        compiler_params=_params(("arbitrary", "arbitrary")),
        name="ffn_down",
    )(block_e, n_used, h, w_down, b_down)


COMBINE_TM = 128


def _combine_body(dest_ref, h_ref, w_ref, g_ref, y_hbm, o_ref, gbuf, sem):
    tm = h_ref.shape[0]

    def row_copy(r, k):
        return pltpu.make_async_copy(y_hbm.at[pl.ds(dest_ref[0, 0, k * tm + r], 1), :],
                                     gbuf.at[k, pl.ds(r, 1), :], sem)

    def start(r, c):
        for k in range(TOP_K):
            row_copy(r, k).start()
        return c

    def wait(r, c):
        for k in range(TOP_K):
            row_copy(r, k).wait()
        return c

    lax.fori_loop(0, tm, start, 0)
    lax.fori_loop(0, tm, wait, 0)
    h = h_ref[...]
    for k in range(TOP_K):
        h = h + w_ref[:, k:k + 1] * gbuf[k]
    o_ref[...] = h * lax.rsqrt(jnp.mean(h * h, axis=-1, keepdims=True) + EPS) * g_ref[...]


def _combine(dest_tiles, h1, w_cols, g_final, y):
    t = h1.shape[0]
    tm = dest_tiles.shape[2] // TOP_K
    d = D_MODEL
    return pl.pallas_call(
        _combine_body,
        grid=(t // tm,),
        in_specs=[
            pl.BlockSpec((1, 1, TOP_K * tm), lambda i: (i, 0, 0), memory_space=pltpu.SMEM),
            pl.BlockSpec((tm, d), lambda i: (i, 0)),
            pl.BlockSpec((tm, TOP_K), lambda i: (i, 0)),
            pl.BlockSpec((1, d), lambda i: (0, 0)),
            pl.BlockSpec(memory_space=pl.ANY),
        ],
        out_specs=pl.BlockSpec((tm, d), lambda i: (i, 0)),
        out_shape=jax.ShapeDtypeStruct((t, d), F32),
        scratch_shapes=[pltpu.VMEM((TOP_K, tm, d), F32), pltpu.SemaphoreType.DMA(())],
        compiler_params=_params(("arbitrary",)),
        name="combine",
    )(dest_tiles, h1, w_cols, g_final, y)


def _tile_major(a, tm):
    k, t = a.shape
    return a.reshape(k, t // tm, tm).transpose(1, 0, 2).reshape(t // tm, 1, k * tm)


def _routing_tables(top_e, rank, counts, t):
    bm = MOE_BM
    nb = (t * TOP_K) // bm + N_EXPERTS
    padded = (counts + bm - 1) // bm * bm
    pad_end = jnp.cumsum(padded)
    pad_start = pad_end - padded
    onehot = top_e[:, :, None] == jnp.arange(N_EXPERTS, dtype=jnp.int32)
    dest = rank + jnp.sum(jnp.where(onehot, pad_start, 0), axis=-1)
    block_row = jnp.arange(nb, dtype=jnp.int32) * bm
    block_e = jnp.minimum(jnp.sum(block_row[:, None] >= pad_end[None, :], axis=1), N_EXPERTS - 1).astype(jnp.int32)
    n_used = (pad_end[-1] // bm).astype(jnp.int32).reshape(1)
    gap = padded - counts
    gap_end = jnp.cumsum(gap)
    j = jnp.arange(N_EXPERTS * bm, dtype=jnp.int32)
    ej = jnp.minimum(jnp.sum(j[:, None] >= gap_end[None, :], axis=1), N_EXPERTS - 1)
    pad_rows = (pad_start + counts)[ej] + j - (gap_end - gap)[ej]
    pad_rows = jnp.clip(pad_rows, 0, nb * bm - 1).astype(jnp.int32)
    n_pad = gap_end[-1].astype(jnp.int32).reshape(1)
    return dest.astype(jnp.int32), block_e, n_used, pad_rows, n_pad, nb * bm


def kernel(x, g_mix, w_in, ssm_conv_w, ssm_conv_b, ssm_dt_bias, ssm_a_log, ssm_d, ssm_norm_g, w_ssm_out,
           sc_conv_w, w_sc_out, b_gate, w_o, g_ffn, w_router, b_router, w_gate_up, b_gate_up, w_down,
           b_down, g_final):
    bsz, seq, d = x.shape
    t = bsz * seq
    assert bsz == 1 and d == D_MODEL and w_in.shape[0] == 1
    xt = x.reshape(t, d)
    wi = w_in[0]
    w_cat = jnp.concatenate(
        [wi[:, OFF_Z:OFF_XBC], wi[:, OFF_XBC:OFF_DT], wi[:, OFF_SC:OFF_GATE], wi[:, OFF_GATE:]], axis=1).astype(BF16)
    w_dt = wi[:, OFF_DT:OFF_SC]
    col = lambda a: a.reshape(-1, 1)
    row = lambda a: a.reshape(1, -1)

    proj, dt_raw = _inproj(xt, row(g_mix[0]), w_cat, w_dt)
    y_norm = _ssd(proj, dt_raw.T, ssm_conv_w[0], row(ssm_conv_b[0]), col(ssm_dt_bias[0]), col(ssm_a_log[0]),
                  col(ssm_d[0]), row(ssm_norm_g[0]))
    mixed = _mix(y_norm, proj, row(b_gate[0]), sc_conv_w[0], w_ssm_out[0].astype(BF16), w_sc_out[0].astype(BF16))
    h1, xp, top_e, top_w, rank, counts = _route(mixed, xt, w_o[0].astype(BF16), row(g_ffn[0]),
                                                w_router[0].T, col(b_router[0]))
    dest, block_e, n_used, pad_rows, n_pad, n_rows = _routing_tables(top_e, rank, counts[:, 0], t)
    xs = _dispatch(xp, _tile_major(dest, min(DISPATCH_TM, t)), pad_rows, n_pad, n_used, n_rows)
    hid = _ffn_up(block_e, n_used, xs, w_gate_up[0], b_gate_up[0].reshape(N_EXPERTS, 1, 2 * D_FF))
    y = _ffn_down(block_e, n_used, hid, w_down[0], b_down[0].reshape(N_EXPERTS, 1, D_MODEL))
    out = _combine(_tile_major(dest, min(COMBINE_TM, t)), h1, top_w.T, row(g_final), y)
    return out.reshape(bsz, seq, d)
```

```python
import functools

import jax
import jax.numpy as jnp
from jax import lax
from jax.experimental import pallas as pl
from jax.experimental.pallas import tpu as pltpu

D_MODEL = 2048
SSM_D_INNER = 2 * D_MODEL
SSM_HEAD_DIM = 64
SSM_N_HEADS = SSM_D_INNER // SSM_HEAD_DIM
SSM_N_GROUPS = 8
SSM_HEADS_PER_GROUP = SSM_N_HEADS // SSM_N_GROUPS
SSM_D_STATE = 128
SSM_CONV = 4
SSM_GN = SSM_N_GROUPS * SSM_D_STATE
SSM_CONV_DIM = SSM_D_INNER + 2 * SSM_GN
SSM_GROUP_CH = SSM_D_INNER // SSM_N_GROUPS
SC_DIM = D_MODEL
SC_WIDTH = 3
N_EXPERTS = 32
TOP_K = 4
D_FF = D_MODEL
SWIGLU_LIMIT = 7.0
SWIGLU_ALPHA = 1.702
EPS = 1e-5

OFF_Z = 0
OFF_XBC = OFF_Z + SSM_D_INNER
OFF_DT = OFF_XBC + SSM_CONV_DIM
OFF_SC = OFF_DT + SSM_N_HEADS
OFF_GATE = OFF_SC + 3 * SC_DIM
D_IN_PROJ = OFF_GATE + 2 * D_MODEL

P_Z = 0
P_XBC = P_Z + SSM_D_INNER
P_SC = P_XBC + SSM_CONV_DIM
P_GATE = P_SC + 3 * SC_DIM
P_TOTAL = P_GATE + 2 * D_MODEL

SUBLANES = 8
VMEM_LIMIT = 56 * 1024 * 1024

F32 = jnp.float32
BF16 = jnp.bfloat16
HIGHEST = lax.Precision.HIGHEST
NT_DIMS = (((1,), (1,)), ((), ()))
TN_DIMS = (((0,), (0,)), ((), ()))


def _sigmoid(v):
    return 1.0 / (1.0 + jnp.exp(-v))


def _params(semantics):
    return pltpu.CompilerParams(dimension_semantics=semantics, vmem_limit_bytes=VMEM_LIMIT)


INPROJ_TM = 1024
INPROJ_TN = 1024


def _inproj_body(x_ref, g_ref, w_ref, wdt_ref, proj_ref, dt_ref, u_ref):
    @pl.when(pl.program_id(1) == 0)
    def _():
        x = x_ref[...]
        u = x * lax.rsqrt(jnp.mean(x * x, axis=-1, keepdims=True) + EPS) * g_ref[...]
        u_ref[...] = u.astype(BF16)
        dt_ref[...] = jnp.dot(u, wdt_ref[...], precision=HIGHEST, preferred_element_type=F32)

    proj_ref[...] = jnp.dot(u_ref[...], w_ref[...], preferred_element_type=F32).astype(BF16)


def _inproj(x, g, w_cat, w_dt):
    t = x.shape[0]
    tm = min(INPROJ_TM, t)
    tn = INPROJ_TN
    return pl.pallas_call(
        _inproj_body,
        grid=(t // tm, P_TOTAL // tn),
        in_specs=[
            pl.BlockSpec((tm, D_MODEL), lambda i, j: (i, 0)),
            pl.BlockSpec((1, D_MODEL), lambda i, j: (0, 0)),
            pl.BlockSpec((D_MODEL, tn), lambda i, j: (0, j)),
            pl.BlockSpec((D_MODEL, SSM_N_HEADS), lambda i, j: (0, 0)),
        ],
        out_specs=[
            pl.BlockSpec((tm, tn), lambda i, j: (i, j)),
            pl.BlockSpec((tm, SSM_N_HEADS), lambda i, j: (i, 0)),
        ],
        out_shape=[
            jax.ShapeDtypeStruct((t, P_TOTAL), BF16),
            jax.ShapeDtypeStruct((t, SSM_N_HEADS), F32),
        ],
        scratch_shapes=[pltpu.VMEM((tm, D_MODEL), BF16)],
        compiler_params=_params(("parallel", "arbitrary")),
        name="inproj",
    )(x, g, w_cat, w_dt)


SSD_L = 256
HEAD_PAIR = 2 * SSM_HEAD_DIM


def _ssd_body(z_ref, x_ref, b_ref, c_ref, dt_ref, wx_ref, wb_ref, wc_ref, bx_ref, bb_ref, bc_ref,
              dtb_ref, alog_ref, d_ref, ng_ref, o_ref, s_ref, xbuf, bbuf, cbuf):
    L = x_ref.shape[0]
    tail = SUBLANES

    @pl.when(pl.program_id(1) == 0)
    def _():
        s_ref[...] = jnp.zeros_like(s_ref)
        xbuf[0:tail, :] = jnp.zeros((tail, xbuf.shape[1]), F32)
        bbuf[0:tail, :] = jnp.zeros((tail, bbuf.shape[1]), F32)
        cbuf[0:tail, :] = jnp.zeros((tail, cbuf.shape[1]), F32)

    def conv_silu(buf, in_ref, w_ref, bias_ref):
        buf[tail:tail + L, :] = in_ref[...].astype(F32)
        acc = bias_ref[...] + w_ref[SSM_CONV - 1:SSM_CONV, :] * buf[tail:tail + L, :]
        for k in range(SSM_CONV - 1):
            off = tail - (SSM_CONV - 1) + k
            acc = acc + w_ref[k:k + 1, :] * buf[off:off + L, :]
        buf[0:tail, :] = buf[L:L + tail, :]
        return acc * _sigmoid(acc)

    xs = conv_silu(xbuf, x_ref, wx_ref, bx_ref)
    bm = conv_silu(bbuf, b_ref, wb_ref, bb_ref).astype(BF16)
    cm = conv_silu(cbuf, c_ref, wc_ref, bc_ref).astype(BF16)

    dt_raw = dt_ref[...] + dtb_ref[...]
    dt = jnp.maximum(dt_raw, 0.0) + jnp.log(1.0 + jnp.exp(-jnp.abs(dt_raw)))
    da = dt * (-jnp.exp(alog_ref[...]))
    row = lax.broadcasted_iota(jnp.int32, (L, L), 0)
    col = lax.broadcasted_iota(jnp.int32, (L, L), 1)
    causal = row >= col
    incl = (row <= col).astype(BF16)
    da_hi = da.astype(BF16)
    rem = da - da_hi.astype(F32)
    da_mid = rem.astype(BF16)
    da_lo = (rem - da_mid.astype(F32)).astype(BF16)
    parts = jnp.dot(jnp.concatenate([da_hi, da_mid, da_lo], axis=0), incl, preferred_element_type=F32)
    hg = SSM_HEADS_PER_GROUP
    cs = parts[0:hg] + parts[hg:2 * hg] + parts[2 * hg:3 * hg]
    cs_end = cs[:, L - 1:L]
    flipped = jnp.concatenate([cs, dt, jnp.exp(cs_end - cs)], axis=0).T
    cs_t, dt_t, to_end_t = flipped[:, 0:hg], flipped[:, hg:2 * hg], flipped[:, 2 * hg:3 * hg]
    ecs_t = jnp.exp(cs_t)

    cb = lax.dot_general(cm, bm, NT_DIMS, preferred_element_type=F32)
    cb = jnp.where(causal, cb, 0.0)
    y_off = lax.dot_general(cm, s_ref[...].astype(BF16), NT_DIMS, preferred_element_type=F32)

    lane = lax.broadcasted_iota(jnp.int32, (L, HEAD_PAIR), 1)
    first = lane < SSM_HEAD_DIM
    ys = []
    xws = []
    for p in range(SSM_HEADS_PER_GROUP // 2):
        h0, h1 = 2 * p, 2 * p + 1
        sl = slice(p * HEAD_PAIR, (p + 1) * HEAD_PAIR)
        xp = xs[:, sl]
        pick = lambda a: jnp.where(first, a[:, h0:h0 + 1], a[:, h1:h1 + 1])
        xdt = xp * pick(dt_t)
        xdt_b = xdt.astype(BF16)
        yd = []
        for h in (h0, h1):
            seg = cs_t[:, h:h + 1] - cs[h:h + 1, :]
            m = cb * jnp.exp(jnp.minimum(seg, 0.0))
            yd.append(jnp.dot(m.astype(BF16), xdt_b, preferred_element_type=F32))
        y = jnp.where(first, yd[0], yd[1]) + y_off[:, sl] * pick(ecs_t)
        y = y + xp * jnp.where(first[0:1, :], d_ref[h0:h0 + 1, :], d_ref[h1:h1 + 1, :])
        ys.append(y)
        xws.append((xdt * pick(to_end_t)).astype(BF16))
    y = jnp.concatenate(ys, axis=1)
    xw = jnp.concatenate(xws, axis=1)

    upd = lax.dot_general(xw, bm, TN_DIMS, preferred_element_type=F32)
    chunk_decay = jnp.exp(cs_end)
    for h in range(SSM_HEADS_PER_GROUP):
        rows = slice(h * SSM_HEAD_DIM, (h + 1) * SSM_HEAD_DIM)
        s_ref[rows, :] = s_ref[rows, :] * chunk_decay[h:h + 1, :] + upd[rows, :]

    z = z_ref[...].astype(F32)
    v = y * (z * _sigmoid(z))
    v = v * lax.rsqrt(jnp.mean(v * v, axis=-1, keepdims=True) + EPS)
    o_ref[...] = (v * ng_ref[...]).astype(BF16)


def _ssd(proj, dt_rows, conv_w, conv_b, dt_bias, a_log, d_skip, norm_g):
    t = proj.shape[0]
    L = min(SSD_L, t)
    gc = SSM_GROUP_CH
    n = SSM_D_STATE
    zc, xc = P_Z // gc, P_XBC // gc
    bc, cc = (P_XBC + SSM_D_INNER) // n, (P_XBC + SSM_D_INNER + SSM_GN) // n
    wbc, wcc = SSM_D_INNER // n, (SSM_D_INNER + SSM_GN) // n
    hg = SSM_HEADS_PER_GROUP
    return pl.pallas_call(
        _ssd_body,
        grid=(SSM_N_GROUPS, t // L),
        in_specs=[
            pl.BlockSpec((L, gc), lambda g, i: (i, zc + g)),
            pl.BlockSpec((L, gc), lambda g, i: (i, xc + g)),
            pl.BlockSpec((L, n), lambda g, i: (i, bc + g)),
            pl.BlockSpec((L, n), lambda g, i: (i, cc + g)),
            pl.BlockSpec((hg, L), lambda g, i: (g, i)),
            pl.BlockSpec((SSM_CONV, gc), lambda g, i: (0, g)),
            pl.BlockSpec((SSM_CONV, n), lambda g, i: (0, wbc + g)),
            pl.BlockSpec((SSM_CONV, n), lambda g, i: (0, wcc + g)),
            pl.BlockSpec((1, gc), lambda g, i: (0, g)),
            pl.BlockSpec((1, n), lambda g, i: (0, wbc + g)),
            pl.BlockSpec((1, n), lambda g, i: (0, wcc + g)),
            pl.BlockSpec((hg, 1), lambda g, i: (g, 0)),
            pl.BlockSpec((hg, 1), lambda g, i: (g, 0)),
            pl.BlockSpec((hg, 1), lambda g, i: (g, 0)),
            pl.BlockSpec((1, gc), lambda g, i: (0, g)),
        ],
        out_specs=pl.BlockSpec((L, gc), lambda g, i: (i, g)),
        out_shape=jax.ShapeDtypeStruct((t, SSM_D_INNER), BF16),
        scratch_shapes=[
            pltpu.VMEM((gc, n), F32),
            pltpu.VMEM((L + SUBLANES, gc), F32),
            pltpu.VMEM((L + SUBLANES, n), F32),
            pltpu.VMEM((L + SUBLANES, n), F32),
        ],
        compiler_params=_params(("parallel", "arbitrary")),
        name="ssd",
    )(proj, proj, proj, proj, dt_rows, conv_w, conv_w, conv_w, conv_b, conv_b, conv_b,
      dt_bias, a_log, d_skip, norm_g)


MIX_TM = 256


def _mix_body(yn_ref, b_ref, c_ref, v_ref, cp_ref, vp_ref, g1_ref, g2_ref, bg1_ref, bg2_ref,
              wc_ref, wssm_ref, wsc_ref, o_ref, buf):
    tm = yn_ref.shape[0]
    tail = SUBLANES
    prev = cp_ref[...].astype(F32) * vp_ref[...].astype(F32)
    buf[0:tail, :] = jnp.where(pl.program_id(0) == 0, 0.0, prev)
    cv = c_ref[...].astype(F32) * v_ref[...].astype(F32)
    buf[tail:tail + tm, :] = cv
    conv = wc_ref[SC_WIDTH - 1:SC_WIDTH, :] * cv
    for k in range(SC_WIDTH - 1):
        off = tail - (SC_WIDTH - 1) + k
        conv = conv + wc_ref[k:k + 1, :] * buf[off:off + tm, :]
    sc_in = (b_ref[...].astype(F32) * conv).astype(BF16)
    y_sc = jnp.dot(sc_in, wsc_ref[...], preferred_element_type=F32)
    y_ssm = jnp.dot(yn_ref[...], wssm_ref[...], preferred_element_type=F32)
    g1 = _sigmoid(g1_ref[...].astype(F32) + bg1_ref[...])
    g2 = _sigmoid(g2_ref[...].astype(F32) + bg2_ref[...])
    o_ref[...] = (g1 * y_ssm + g2 * y_sc).astype(BF16)


def _resident(shape):
    return pl.BlockSpec(shape, lambda *_: (0,) * len(shape), pipeline_mode=pl.Buffered(1))


def _mix(y_norm, proj, b_gate, sc_conv_w, w_ssm_out, w_sc_out):
    t = y_norm.shape[0]
    tm = min(MIX_TM, t)
    d = D_MODEL
    sb, gb = P_SC // d, P_GATE // d
    prev_rows = lambda i: jnp.maximum(i * (tm // SUBLANES) - 1, 0)
    return pl.pallas_call(
        _mix_body,
        grid=(t // tm,),
        in_specs=[
            pl.BlockSpec((tm, SSM_D_INNER), lambda i: (i, 0)),
            pl.BlockSpec((tm, d), lambda i: (i, sb)),
            pl.BlockSpec((tm, d), lambda i: (i, sb + 1)),
            pl.BlockSpec((tm, d), lambda i: (i, sb + 2)),
            pl.BlockSpec((SUBLANES, d), lambda i: (prev_rows(i), sb + 1)),
            pl.BlockSpec((SUBLANES, d), lambda i: (prev_rows(i), sb + 2)),
            pl.BlockSpec((tm, d), lambda i: (i, gb)),
            pl.BlockSpec((tm, d), lambda i: (i, gb + 1)),
            pl.BlockSpec((1, d), lambda i: (0, 0)),
            pl.BlockSpec((1, d), lambda i: (0, 1)),
            pl.BlockSpec((SC_WIDTH, d), lambda i: (0, 0)),
            _resident((SSM_D_INNER, d)),
            _resident((d, d)),
        ],
        out_specs=pl.BlockSpec((tm, d), lambda i: (i, 0)),
        out_shape=jax.ShapeDtypeStruct((t, d), BF16),
        scratch_shapes=[pltpu.VMEM((tm + SUBLANES, d), F32)],
        compiler_params=_params(("parallel",)),
        name="mix",
    )(y_norm, proj, proj, proj, proj, proj, proj, proj, b_gate, b_gate, sc_conv_w, w_ssm_out, w_sc_out)


ROUTE_TM = 512
PACK_W = D_MODEL // 2


def _pack_rows(v):
    lo = lax.bitcast_convert_type(v[:, :PACK_W].astype(F32), jnp.uint32)
    hi = lax.bitcast_convert_type(v[:, PACK_W:].astype(F32), jnp.uint32)
    return hi | (lo >> 16)


def _unpack_rows(w):
    lo = lax.bitcast_convert_type(w << 16, F32).astype(BF16)
    hi = lax.bitcast_convert_type(w & jnp.uint32(0xFFFF0000), F32).astype(BF16)
    return jnp.concatenate([lo, hi], axis=1)


def _route_body(m_ref, x_ref, wo_ref, g_ref, wr_ref, br_ref,
                h_ref, xp_ref, e_ref, w_ref, r_ref, cnt_ref, carry):
    tm = m_ref.shape[0]

    @pl.when(pl.program_id(0) == 0)
    def _():
        carry[...] = jnp.zeros_like(carry)

    h = x_ref[...] + jnp.dot(m_ref[...], wo_ref[...], preferred_element_type=F32)
    h_ref[...] = h
    xn = h * lax.rsqrt(jnp.mean(h * h, axis=-1, keepdims=True) + EPS) * g_ref[...]
    xn_hi = xn.astype(BF16)
    xn_lo = (xn - xn_hi.astype(F32)).astype(BF16)
    xp_ref[...] = _pack_rows(xn_hi)

    ne = N_EXPERTS
    both = lax.dot_general(wr_ref[...], xn_hi, NT_DIMS, preferred_element_type=F32)
    cross = lax.dot_general(wr_ref[0:ne, :], xn_lo, NT_DIMS, preferred_element_type=F32)
    logits = both[0:ne] + both[ne:2 * ne] + cross + br_ref[...]
    eidx = lax.broadcasted_iota(jnp.int32, (N_EXPERTS, tm), 0)
    vals, hots = [], []
    for k in range(TOP_K):
        best = jnp.max(logits, axis=0, keepdims=True)
        arg = jnp.min(jnp.where(logits == best, eidx, N_EXPERTS), axis=0, keepdims=True)
        hot = eidx == arg
        e_ref[k:k + 1, :] = arg
        vals.append(best)
        hots.append(hot)
        logits = jnp.where(hot, -jnp.inf, logits)
    exps = [jnp.exp(v - vals[0]) for v in vals]
    denom = exps[0] + exps[1] + exps[2] + exps[3]
    for k in range(TOP_K):
        w_ref[k:k + 1, :] = exps[k] / denom

    cnt = (hots[0] | hots[1] | hots[2] | hots[3]).astype(F32)
    r_i = lax.broadcasted_iota(jnp.int32, (tm, tm), 0)
    c_i = lax.broadcasted_iota(jnp.int32, (tm, tm), 1)
    before = (r_i < c_i).astype(BF16)
    prior = carry[:, 0:1] + jnp.dot(cnt.astype(BF16), before, preferred_element_type=F32)
    for k in range(TOP_K):
        r_ref[k:k + 1, :] = jnp.sum(jnp.where(hots[k], prior, 0.0), axis=0, keepdims=True).astype(jnp.int32)
    total = carry[...] + jnp.sum(cnt, axis=1, keepdims=True)
    carry[...] = total
    cnt_ref[...] = total.astype(jnp.int32)


def _route(mixed, x, w_o, g_ffn, w_router_t, b_router):
    t = x.shape[0]
    tm = min(ROUTE_TM, t)
    d = D_MODEL
    return pl.pallas_call(
        _route_body,
        grid=(t // tm,),
        in_specs=[
            pl.BlockSpec((tm, d), lambda i: (i, 0)),
            pl.BlockSpec((tm, d), lambda i: (i, 0)),
            _resident((d, d)),
            pl.BlockSpec((1, d), lambda i: (0, 0)),
            pl.BlockSpec((2 * N_EXPERTS, d), lambda i: (0, 0)),
            pl.BlockSpec((N_EXPERTS, 1), lambda i: (0, 0)),
        ],
        out_specs=[
            pl.BlockSpec((tm, d), lambda i: (i, 0)),
            pl.BlockSpec((tm, PACK_W), lambda i: (i, 0)),
            pl.BlockSpec((TOP_K, tm), lambda i: (0, i)),
            pl.BlockSpec((TOP_K, tm), lambda i: (0, i)),
            pl.BlockSpec((TOP_K, tm), lambda i: (0, i)),
            pl.BlockSpec((N_EXPERTS, 128), lambda i: (0, 0)),
        ],
        out_shape=[
            jax.ShapeDtypeStruct((t, d), F32),
            jax.ShapeDtypeStruct((t, PACK_W), jnp.uint32),
            jax.ShapeDtypeStruct((TOP_K, t), jnp.int32),
            jax.ShapeDtypeStruct((TOP_K, t), F32),
            jax.ShapeDtypeStruct((TOP_K, t), jnp.int32),
            jax.ShapeDtypeStruct((N_EXPERTS, 128), jnp.int32),
        ],
        scratch_shapes=[pltpu.VMEM((N_EXPERTS, 128), F32)],
        compiler_params=_params(("arbitrary",)),
        name="route",
    )(mixed, x, w_o, g_ffn, w_router_t, b_router)


MOE_BM = 256
DISPATCH_TM = 256


def _dispatch_body(dest_ref, padrow_ref, npad_ref, nused_ref, xp_ref, xs_hbm, zblk, sem, zsem):
    tm = xp_ref.shape[0]
    bm = zblk.shape[0]
    nb = xs_hbm.shape[0] // bm

    @pl.when(pl.program_id(0) == 0)
    def _():
        zblk[...] = jnp.zeros_like(zblk)
        npad = npad_ref[0]
        nused = nused_ref[0]

        def zero_row(j):
            return pltpu.make_async_copy(zblk.at[pl.ds(0, 1), :], xs_hbm.at[pl.ds(padrow_ref[j], 1), :], zsem)

        def zero_block(b):
            return pltpu.make_async_copy(zblk, xs_hbm.at[pl.ds(pl.multiple_of(b * bm, bm), bm), :], zsem)

        def start_row(j, c):
            zero_row(j).start()
            return c

        def wait_row(j, c):
            zero_row(j).wait()
            return c

        def start_block(b, c):
            zero_block(b).start()
            return c

        def wait_block(b, c):
            zero_block(b).wait()
            return c

        lax.fori_loop(0, npad, start_row, 0)
        lax.fori_loop(nused, nb, start_block, 0)
        lax.fori_loop(0, npad, wait_row, 0)
        lax.fori_loop(nused, nb, wait_block, 0)

    def row_copy(r, k):
        return pltpu.make_async_copy(xp_ref.at[pl.ds(r, 1), :],
                                     xs_hbm.at[pl.ds(dest_ref[0, 0, k * tm + r], 1), :], sem)

    def start(r, c):
        for k in range(TOP_K):
            row_copy(r, k).start()
        return c

    def wait(r, c):
        for k in range(TOP_K):
            row_copy(r, k).wait()
        return c

    lax.fori_loop(0, tm, start, 0)
    lax.fori_loop(0, tm, wait, 0)


def _dispatch(xp, dest_tiles, pad_rows, n_pad, n_used, n_rows):
    t = xp.shape[0]
    tm = dest_tiles.shape[2] // TOP_K
    return pl.pallas_call(
        _dispatch_body,
        grid=(t // tm,),
        in_specs=[
            pl.BlockSpec((1, 1, TOP_K * tm), lambda i: (i, 0, 0), memory_space=pltpu.SMEM),
            pl.BlockSpec(memory_space=pltpu.SMEM),
            pl.BlockSpec(memory_space=pltpu.SMEM),
            pl.BlockSpec(memory_space=pltpu.SMEM),
            pl.BlockSpec((tm, PACK_W), lambda i: (i, 0)),
        ],
        out_specs=pl.BlockSpec(memory_space=pl.ANY),
        out_shape=jax.ShapeDtypeStruct((n_rows, PACK_W), jnp.uint32),
        scratch_shapes=[
            pltpu.VMEM((MOE_BM, PACK_W), jnp.uint32),
            pltpu.SemaphoreType.DMA(()),
            pltpu.SemaphoreType.DMA(()),
        ],
        compiler_params=_params(("arbitrary",)),
        name="dispatch",
    )(dest_tiles, pad_rows, n_pad, n_used, xp)


FFN_TF = 1024
FFN_TN = 1024
N_SLOTS = 2


def _stream_expert_blocks(first_ref, nblk_ref, nused_ref, src_hbm, dst_hbm, ibuf, obuf, isem, osem,
                          prepare, compute):
    j, e = pl.program_id(0), pl.program_id(1)
    bm, width = obuf.shape[1], obuf.shape[2]
    col = pl.multiple_of(j * width, width)
    nblk = nblk_ref[e]
    first = first_ref[e]

    def rows(b):
        return pl.ds(pl.multiple_of(b * bm, bm), bm)

    def fetch(b, slot):
        return pltpu.make_async_copy(src_hbm.at[rows(b), :], ibuf.at[slot], isem.at[slot])

    def flush(b, slot):
        return pltpu.make_async_copy(obuf.at[slot], dst_hbm.at[rows(b), pl.ds(col, width)], osem.at[slot])

    @pl.when(nblk > 0)
    def _():
        fetch(first, 0).start()
        prepare()

    def step(i, carry):
        slot = lax.rem(i, N_SLOTS)
        fetch(first + i, slot).wait()

        @pl.when(i + 1 < nblk)
        def _():
            fetch(first + i + 1, 1 - slot).start()

        out = compute(ibuf[slot])

        @pl.when(i >= N_SLOTS)
        def _():
            flush(first + i - N_SLOTS, slot).wait()

        obuf[slot] = out
        flush(first + i, slot).start()
        return carry

    lax.fori_loop(0, nblk, step, 0)

    @pl.when(nblk >= 2)
    def _():
        flush(first + nblk - 2, lax.rem(nblk, N_SLOTS)).wait()

    @pl.when(nblk >= 1)
    def _():
        flush(first + nblk - 1, lax.rem(nblk - 1, N_SLOTS)).wait()

    @pl.when(e == pl.num_programs(1) - 1)
    def _():
        obuf[0] = jnp.zeros(obuf.shape[1:], obuf.dtype)

        def zero(b, carry):
            cp = flush(b, 0)
            cp.start()
            cp.wait()
            return carry

        lax.fori_loop(nused_ref[0], dst_hbm.shape[0] // bm, zero, 0)


def _ffn_up_body(first_ref, nblk_ref, nused_ref, xs_hbm, wg_ref, wu_ref, bg_ref, bu_ref, h_hbm,
                 wg_bf, wu_bf, ibuf, obuf, isem, osem):
    def prepare():
        wg_bf[...] = wg_ref[...].astype(BF16)
        wu_bf[...] = wu_ref[...].astype(BF16)

    def compute(packed):
        x = _unpack_rows(packed)
        gate = jnp.dot(x, wg_bf[...], preferred_element_type=F32) + bg_ref[...]
        up = jnp.dot(x, wu_bf[...], preferred_element_type=F32) + bu_ref[...]
        gate = jnp.minimum(gate, SWIGLU_LIMIT)
        up = jnp.clip(up, -SWIGLU_LIMIT, SWIGLU_LIMIT)
        return ((up + 1.0) * gate * _sigmoid(SWIGLU_ALPHA * gate)).astype(BF16)

    _stream_expert_blocks(first_ref, nblk_ref, nused_ref, xs_hbm, h_hbm, ibuf, obuf, isem, osem,
                          prepare, compute)


def _ffn_up(first_blk, n_blk, n_used, xs, w_gate_up, b_gate_up):
    n_rows = xs.shape[0]
    tf = FFN_TF
    nf = D_FF // tf
    return pl.pallas_call(
        _ffn_up_body,
        grid_spec=pltpu.PrefetchScalarGridSpec(
            num_scalar_prefetch=3,
            grid=(nf, N_EXPERTS),
            in_specs=[
                pl.BlockSpec(memory_space=pl.ANY),
                pl.BlockSpec((None, D_MODEL, tf), lambda j, e, *_: (e, 0, j)),
                pl.BlockSpec((None, D_MODEL, tf), lambda j, e, *_: (e, 0, nf + j)),
                pl.BlockSpec((None, 1, tf), lambda j, e, *_: (e, 0, j)),
                pl.BlockSpec((None, 1, tf), lambda j, e, *_: (e, 0, nf + j)),
            ],
            out_specs=pl.BlockSpec(memory_space=pl.ANY),
            scratch_shapes=[
                pltpu.VMEM((D_MODEL, tf), BF16),
                pltpu.VMEM((D_MODEL, tf), BF16),
                pltpu.VMEM((N_SLOTS, MOE_BM, PACK_W), jnp.uint32),
                pltpu.VMEM((N_SLOTS, MOE_BM, tf), BF16),
                pltpu.SemaphoreType.DMA((N_SLOTS,)),
                pltpu.SemaphoreType.DMA((N_SLOTS,)),
            ],
        ),
        out_shape=jax.ShapeDtypeStruct((n_rows, D_FF), BF16),
        compiler_params=_params(("arbitrary", "arbitrary")),
        name="ffn_up",
    )(first_blk, n_blk, n_used, xs, w_gate_up, w_gate_up, b_gate_up, b_gate_up)


def _ffn_down_body(first_ref, nblk_ref, nused_ref, h_hbm, wd_ref, bd_ref, y_hbm, wd_bf, ibuf, obuf, isem, osem):
    def prepare():
        wd_bf[...] = wd_ref[...].astype(BF16)

    def compute(hid):
        return jnp.dot(hid, wd_bf[...], preferred_element_type=F32) + bd_ref[...]

    _stream_expert_blocks(first_ref, nblk_ref, nused_ref, h_hbm, y_hbm, ibuf, obuf, isem, osem,
                          prepare, compute)


def _ffn_down(first_blk, n_blk, n_used, h, w_down, b_down):
    n_rows = h.shape[0]
    tn = FFN_TN
    return pl.pallas_call(
        _ffn_down_body,
        grid_spec=pltpu.PrefetchScalarGridSpec(
            num_scalar_prefetch=3,
            grid=(D_MODEL // tn, N_EXPERTS),
            in_specs=[
                pl.BlockSpec(memory_space=pl.ANY),
                pl.BlockSpec((None, D_FF, tn), lambda j, e, *_: (e, 0, j)),
                pl.BlockSpec((None, 1, tn), lambda j, e, *_: (e, 0, j)),
            ],
            out_specs=pl.BlockSpec(memory_space=pl.ANY),
            scratch_shapes=[
                pltpu.VMEM((D_FF, tn), BF16),
                pltpu.VMEM((N_SLOTS, MOE_BM, D_FF), BF16),
                pltpu.VMEM((N_SLOTS, MOE_BM, tn), F32),
                pltpu.SemaphoreType.DMA((N_SLOTS,)),
                pltpu.SemaphoreType.DMA((N_SLOTS,)),
            ],
        ),
        out_shape=jax.ShapeDtypeStruct((n_rows, D_MODEL), F32),
        compiler_params=_params(("arbitrary", "arbitrary")),
        name="ffn_down",
    )(first_blk, n_blk, n_used, h, w_down, b_down)


COMBINE_TM = 128


def _combine_body(dest_ref, h_ref, w_ref, g_ref, y_hbm, o_ref, gbuf, sem):
    tm = h_ref.shape[0]

    def row_copy(r, k):
        return pltpu.make_async_copy(y_hbm.at[pl.ds(dest_ref[0, 0, k * tm + r], 1), :],
                                     gbuf.at[k, pl.ds(r, 1), :], sem)

    def start(r, c):
        for k in range(TOP_K):
            row_copy(r, k).start()
        return c

    def wait(r, c):
        for k in range(TOP_K):
            row_copy(r, k).wait()
        return c

    lax.fori_loop(0, tm, start, 0)
    lax.fori_loop(0, tm, wait, 0)
    h = h_ref[...]
    for k in range(TOP_K):
        h = h + w_ref[:, k:k + 1] * gbuf[k]
    o_ref[...] = h * lax.rsqrt(jnp.mean(h * h, axis=-1, keepdims=True) + EPS) * g_ref[...]


def _combine(dest_tiles, h1, w_cols, g_final, y):
    t = h1.shape[0]
    tm = dest_tiles.shape[2] // TOP_K
    d = D_MODEL
    return pl.pallas_call(
        _combine_body,
        grid=(t // tm,),
        in_specs=[
            pl.BlockSpec((1, 1, TOP_K * tm), lambda i: (i, 0, 0), memory_space=pltpu.SMEM),
            pl.BlockSpec((tm, d), lambda i: (i, 0)),
            pl.BlockSpec((tm, TOP_K), lambda i: (i, 0)),
            pl.BlockSpec((1, d), lambda i: (0, 0)),
            pl.BlockSpec(memory_space=pl.ANY),
        ],
        out_specs=pl.BlockSpec((tm, d), lambda i: (i, 0)),
        out_shape=jax.ShapeDtypeStruct((t, d), F32),
        scratch_shapes=[pltpu.VMEM((TOP_K, tm, d), F32), pltpu.SemaphoreType.DMA(())],
        compiler_params=_params(("arbitrary",)),
        name="combine",
    )(dest_tiles, h1, w_cols, g_final, y)


def _tile_major(a, tm):
    k, t = a.shape
    return a.reshape(k, t // tm, tm).transpose(1, 0, 2).reshape(t // tm, 1, k * tm)


def _routing_tables(top_e, rank, counts, t):
    bm = MOE_BM
    nb = (t * TOP_K) // bm + N_EXPERTS
    padded = (counts + bm - 1) // bm * bm
    pad_end = jnp.cumsum(padded)
    pad_start = pad_end - padded
    onehot = top_e[:, :, None] == jnp.arange(N_EXPERTS, dtype=jnp.int32)
    dest = rank + jnp.sum(jnp.where(onehot, pad_start, 0), axis=-1)
    first_blk = (pad_start // bm).astype(jnp.int32)
    n_blk = (padded // bm).astype(jnp.int32)
    n_used = (pad_end[-1] // bm).astype(jnp.int32).reshape(1)
    gap = padded - counts
    gap_end = jnp.cumsum(gap)
    j = jnp.arange(N_EXPERTS * bm, dtype=jnp.int32)
    ej = jnp.minimum(jnp.sum(j[:, None] >= gap_end[None, :], axis=1), N_EXPERTS - 1)
    pad_rows = (pad_start + counts)[ej] + j - (gap_end - gap)[ej]
    pad_rows = jnp.clip(pad_rows, 0, nb * bm - 1).astype(jnp.int32)
    n_pad = gap_end[-1].astype(jnp.int32).reshape(1)
    return dest.astype(jnp.int32), first_blk, n_blk, n_used, pad_rows, n_pad, nb * bm


def kernel(x, g_mix, w_in, ssm_conv_w, ssm_conv_b, ssm_dt_bias, ssm_a_log, ssm_d, ssm_norm_g, w_ssm_out,
           sc_conv_w, w_sc_out, b_gate, w_o, g_ffn, w_router, b_router, w_gate_up, b_gate_up, w_down,
           b_down, g_final):
    bsz, seq, d = x.shape
    t = bsz * seq
    assert bsz == 1 and d == D_MODEL and w_in.shape[0] == 1
    xt = x.reshape(t, d)
    wi = w_in[0]
    w_cat = jnp.concatenate(
        [wi[:, OFF_Z:OFF_XBC], wi[:, OFF_XBC:OFF_DT], wi[:, OFF_SC:OFF_GATE], wi[:, OFF_GATE:]], axis=1).astype(BF16)
    w_dt = wi[:, OFF_DT:OFF_SC]
    col = lambda a: a.reshape(-1, 1)
    row = lambda a: a.reshape(1, -1)

    proj, dt_raw = _inproj(xt, row(g_mix[0]), w_cat, w_dt)
    y_norm = _ssd(proj, dt_raw.T, ssm_conv_w[0], row(ssm_conv_b[0]), col(ssm_dt_bias[0]), col(ssm_a_log[0]),
                  col(ssm_d[0]), row(ssm_norm_g[0]))
    mixed = _mix(y_norm, proj, row(b_gate[0]), sc_conv_w[0], w_ssm_out[0].astype(BF16), w_sc_out[0].astype(BF16))
    wr_t = w_router[0].T
    wr_hi = wr_t.astype(BF16)
    wr_lo = (wr_t - wr_hi.astype(F32)).astype(BF16)
    h1, xp, top_e, top_w, rank, counts = _route(mixed, xt, w_o[0].astype(BF16), row(g_ffn[0]),
                                                jnp.concatenate([wr_hi, wr_lo], axis=0), col(b_router[0]))
    dest, first_blk, n_blk, n_used, pad_rows, n_pad, n_rows = _routing_tables(top_e, rank, counts[:, 0], t)
    xs = _dispatch(xp, _tile_major(dest, min(DISPATCH_TM, t)), pad_rows, n_pad, n_used, n_rows)
    hid = _ffn_up(first_blk, n_blk, n_used, xs, w_gate_up[0], b_gate_up[0].reshape(N_EXPERTS, 1, 2 * D_FF))
    y = _ffn_down(first_blk, n_blk, n_used, hid, w_down[0], b_down[0].reshape(N_EXPERTS, 1, D_MODEL))
    out = _combine(_tile_major(dest, min(COMBINE_TM, t)), h1, top_w.T, row(g_final), y)
    return out.reshape(bsz, seq, d)
```

```python
import functools

import jax
import jax.numpy as jnp
from jax import lax
from jax.experimental import pallas as pl
from jax.experimental.pallas import tpu as pltpu

D_MODEL = 2048
SSM_D_INNER = 2 * D_MODEL
SSM_HEAD_DIM = 64
SSM_N_HEADS = SSM_D_INNER // SSM_HEAD_DIM
SSM_N_GROUPS = 8
SSM_HEADS_PER_GROUP = SSM_N_HEADS // SSM_N_GROUPS
SSM_D_STATE = 128
SSM_CONV = 4
SSM_GN = SSM_N_GROUPS * SSM_D_STATE
SSM_CONV_DIM = SSM_D_INNER + 2 * SSM_GN
SSM_GROUP_CH = SSM_D_INNER // SSM_N_GROUPS
SC_DIM = D_MODEL
SC_WIDTH = 3
N_EXPERTS = 32
TOP_K = 4
D_FF = D_MODEL
SWIGLU_LIMIT = 7.0
SWIGLU_ALPHA = 1.702
EPS = 1e-5

OFF_Z = 0
OFF_XBC = OFF_Z + SSM_D_INNER
OFF_DT = OFF_XBC + SSM_CONV_DIM
OFF_SC = OFF_DT + SSM_N_HEADS
OFF_GATE = OFF_SC + 3 * SC_DIM
D_IN_PROJ = OFF_GATE + 2 * D_MODEL

P_Z = 0
P_XBC = P_Z + SSM_D_INNER
P_SC = P_XBC + SSM_CONV_DIM
P_GATE = P_SC + 3 * SC_DIM
P_TOTAL = P_GATE + 2 * D_MODEL

SUBLANES = 8
VMEM_LIMIT = 56 * 1024 * 1024

F32 = jnp.float32
BF16 = jnp.bfloat16
HIGHEST = lax.Precision.HIGHEST
NT_DIMS = (((1,), (1,)), ((), ()))
TN_DIMS = (((0,), (0,)), ((), ()))


def _sigmoid(v):
    return 1.0 / (1.0 + jnp.exp(-v))


def _params(semantics):
    return pltpu.CompilerParams(dimension_semantics=semantics, vmem_limit_bytes=VMEM_LIMIT)


INPROJ_TM = 1024
INPROJ_TN = 1024


def _inproj_body(x_ref, g_ref, w_ref, wdt_ref, proj_ref, dt_ref, u_ref):
    @pl.when(pl.program_id(1) == 0)
    def _():
        x = x_ref[...]
        u = x * lax.rsqrt(jnp.mean(x * x, axis=-1, keepdims=True) + EPS) * g_ref[...]
        u_hi = u.astype(BF16)
        u_ref[...] = u_hi
        u_lo = (u - u_hi.astype(F32)).astype(BF16)
        both = jnp.dot(u_hi, wdt_ref[...], preferred_element_type=F32)
        cross = jnp.dot(u_lo, wdt_ref[...], preferred_element_type=F32)
        nh = SSM_N_HEADS
        dt_ref[...] = both[:, :nh] + both[:, nh:] + cross[:, :nh]

    proj_ref[...] = jnp.dot(u_ref[...], w_ref[...], preferred_element_type=F32).astype(BF16)


def _inproj(x, g, w_cat, w_dt):
    t = x.shape[0]
    tm = min(INPROJ_TM, t)
    tn = INPROJ_TN
    return pl.pallas_call(
        _inproj_body,
        grid=(t // tm, P_TOTAL // tn),
        in_specs=[
            pl.BlockSpec((tm, D_MODEL), lambda i, j: (i, 0)),
            pl.BlockSpec((1, D_MODEL), lambda i, j: (0, 0)),
            pl.BlockSpec((D_MODEL, tn), lambda i, j: (0, j)),
            pl.BlockSpec((D_MODEL, 2 * SSM_N_HEADS), lambda i, j: (0, 0)),
        ],
        out_specs=[
            pl.BlockSpec((tm, tn), lambda i, j: (i, j)),
            pl.BlockSpec((tm, SSM_N_HEADS), lambda i, j: (i, 0)),
        ],
        out_shape=[
            jax.ShapeDtypeStruct((t, P_TOTAL), BF16),
            jax.ShapeDtypeStruct((t, SSM_N_HEADS), F32),
        ],
        scratch_shapes=[pltpu.VMEM((tm, D_MODEL), BF16)],
        compiler_params=_params(("parallel", "arbitrary")),
        name="inproj",
    )(x, g, w_cat, w_dt)


SSD_L = 256
HEAD_PAIR = 2 * SSM_HEAD_DIM


def _ssd_body(z_ref, x_ref, b_ref, c_ref, dt_ref, wx_ref, wb_ref, wc_ref, bx_ref, bb_ref, bc_ref,
              dtb_ref, alog_ref, d_ref, ng_ref, o_ref, s_ref, xbuf, bbuf, cbuf):
    L = x_ref.shape[0]
    tail = SUBLANES

    @pl.when(pl.program_id(1) == 0)
    def _():
        s_ref[...] = jnp.zeros_like(s_ref)
        xbuf[0:tail, :] = jnp.zeros((tail, xbuf.shape[1]), F32)
        bbuf[0:tail, :] = jnp.zeros((tail, bbuf.shape[1]), F32)
        cbuf[0:tail, :] = jnp.zeros((tail, cbuf.shape[1]), F32)

    def conv_silu(buf, in_ref, w_ref, bias_ref):
        buf[tail:tail + L, :] = in_ref[...].astype(F32)
        acc = bias_ref[...] + w_ref[SSM_CONV - 1:SSM_CONV, :] * buf[tail:tail + L, :]
        for k in range(SSM_CONV - 1):
            off = tail - (SSM_CONV - 1) + k
            acc = acc + w_ref[k:k + 1, :] * buf[off:off + L, :]
        buf[0:tail, :] = buf[L:L + tail, :]
        return acc * _sigmoid(acc)

    xs = conv_silu(xbuf, x_ref, wx_ref, bx_ref)
    bm = conv_silu(bbuf, b_ref, wb_ref, bb_ref).astype(BF16)
    cm = conv_silu(cbuf, c_ref, wc_ref, bc_ref).astype(BF16)

    dt_raw = dt_ref[...] + dtb_ref[...]
    dt = jnp.maximum(dt_raw, 0.0) + jnp.log(1.0 + jnp.exp(-jnp.abs(dt_raw)))
    da = dt * (-jnp.exp(alog_ref[...]))
    row = lax.broadcasted_iota(jnp.int32, (L, L), 0)
    col = lax.broadcasted_iota(jnp.int32, (L, L), 1)
    causal = row >= col
    incl = (row <= col).astype(BF16)
    da_hi = da.astype(BF16)
    rem = da - da_hi.astype(F32)
    da_mid = rem.astype(BF16)
    da_lo = (rem - da_mid.astype(F32)).astype(BF16)
    parts = jnp.dot(jnp.concatenate([da_hi, da_mid, da_lo], axis=0), incl, preferred_element_type=F32)
    hg = SSM_HEADS_PER_GROUP
    cs = parts[0:hg] + parts[hg:2 * hg] + parts[2 * hg:3 * hg]
    cs_end = cs[:, L - 1:L]
    flipped = jnp.concatenate([cs, dt, jnp.exp(cs_end - cs)], axis=0).T
    cs_t, dt_t, to_end_t = flipped[:, 0:hg], flipped[:, hg:2 * hg], flipped[:, 2 * hg:3 * hg]
    ecs_t = jnp.exp(cs_t)

    cb = lax.dot_general(cm, bm, NT_DIMS, preferred_element_type=F32)
    cb = jnp.where(causal, cb, 0.0)
    y_off = lax.dot_general(cm, s_ref[...].astype(BF16), NT_DIMS, preferred_element_type=F32)

    lane = lax.broadcasted_iota(jnp.int32, (L, HEAD_PAIR), 1)
    first = lane < SSM_HEAD_DIM
    ys = []
    xws = []
    for p in range(SSM_HEADS_PER_GROUP // 2):
        h0, h1 = 2 * p, 2 * p + 1
        sl = slice(p * HEAD_PAIR, (p + 1) * HEAD_PAIR)
        xp = xs[:, sl]
        pick = lambda a: jnp.where(first, a[:, h0:h0 + 1], a[:, h1:h1 + 1])
        xdt = xp * pick(dt_t)
        xdt_b = xdt.astype(BF16)
        yd = []
        for h in (h0, h1):
            seg = cs_t[:, h:h + 1] - cs[h:h + 1, :]
            m = cb * jnp.exp(jnp.minimum(seg, 0.0))
            yd.append(jnp.dot(m.astype(BF16), xdt_b, preferred_element_type=F32))
        y = jnp.where(first, yd[0], yd[1]) + y_off[:, sl] * pick(ecs_t)
        y = y + xp * jnp.where(first[0:1, :], d_ref[h0:h0 + 1, :], d_ref[h1:h1 + 1, :])
        ys.append(y)
        xws.append((xdt * pick(to_end_t)).astype(BF16))
    y = jnp.concatenate(ys, axis=1)
    xw = jnp.concatenate(xws, axis=1)

    upd = lax.dot_general(xw, bm, TN_DIMS, preferred_element_type=F32)
    chunk_decay = jnp.exp(cs_end)
    for h in range(SSM_HEADS_PER_GROUP):
        rows = slice(h * SSM_HEAD_DIM, (h + 1) * SSM_HEAD_DIM)
        s_ref[rows, :] = s_ref[rows, :] * chunk_decay[h:h + 1, :] + upd[rows, :]

    z = z_ref[...].astype(F32)
    v = y * (z * _sigmoid(z))
    v = v * lax.rsqrt(jnp.mean(v * v, axis=-1, keepdims=True) + EPS)
    o_ref[...] = (v * ng_ref[...]).astype(BF16)


def _ssd(proj, dt_rows, conv_w, conv_b, dt_bias, a_log, d_skip, norm_g):
    t = proj.shape[0]
    L = min(SSD_L, t)
    gc = SSM_GROUP_CH
    n = SSM_D_STATE
    zc, xc = P_Z // gc, P_XBC // gc
    bc, cc = (P_XBC + SSM_D_INNER) // n, (P_XBC + SSM_D_INNER + SSM_GN) // n
    wbc, wcc = SSM_D_INNER // n, (SSM_D_INNER + SSM_GN) // n
    hg = SSM_HEADS_PER_GROUP
    return pl.pallas_call(
        _ssd_body,
        grid=(SSM_N_GROUPS, t // L),
        in_specs=[
            pl.BlockSpec((L, gc), lambda g, i: (i, zc + g)),
            pl.BlockSpec((L, gc), lambda g, i: (i, xc + g)),
            pl.BlockSpec((L, n), lambda g, i: (i, bc + g)),
            pl.BlockSpec((L, n), lambda g, i: (i, cc + g)),
            pl.BlockSpec((hg, L), lambda g, i: (g, i)),
            pl.BlockSpec((SSM_CONV, gc), lambda g, i: (0, g)),
            pl.BlockSpec((SSM_CONV, n), lambda g, i: (0, wbc + g)),
            pl.BlockSpec((SSM_CONV, n), lambda g, i: (0, wcc + g)),
            pl.BlockSpec((1, gc), lambda g, i: (0, g)),
            pl.BlockSpec((1, n), lambda g, i: (0, wbc + g)),
            pl.BlockSpec((1, n), lambda g, i: (0, wcc + g)),
            pl.BlockSpec((hg, 1), lambda g, i: (g, 0)),
            pl.BlockSpec((hg, 1), lambda g, i: (g, 0)),
            pl.BlockSpec((hg, 1), lambda g, i: (g, 0)),
            pl.BlockSpec((1, gc), lambda g, i: (0, g)),
        ],
        out_specs=pl.BlockSpec((L, gc), lambda g, i: (i, g)),
        out_shape=jax.ShapeDtypeStruct((t, SSM_D_INNER), BF16),
        scratch_shapes=[
            pltpu.VMEM((gc, n), F32),
            pltpu.VMEM((L + SUBLANES, gc), F32),
            pltpu.VMEM((L + SUBLANES, n), F32),
            pltpu.VMEM((L + SUBLANES, n), F32),
        ],
        compiler_params=_params(("parallel", "arbitrary")),
        name="ssd",
    )(proj, proj, proj, proj, dt_rows, conv_w, conv_w, conv_w, conv_b, conv_b, conv_b,
      dt_bias, a_log, d_skip, norm_g)


MIX_TM = 256


def _mix_body(yn_ref, b_ref, c_ref, v_ref, cp_ref, vp_ref, g1_ref, g2_ref, bg1_ref, bg2_ref,
              wc_ref, wssm_ref, wsc_ref, o_ref, buf):
    tm = yn_ref.shape[0]
    tail = SUBLANES
    prev = cp_ref[...].astype(F32) * vp_ref[...].astype(F32)
    buf[0:tail, :] = jnp.where(pl.program_id(0) == 0, 0.0, prev)
    cv = c_ref[...].astype(F32) * v_ref[...].astype(F32)
    buf[tail:tail + tm, :] = cv
    conv = wc_ref[SC_WIDTH - 1:SC_WIDTH, :] * cv
    for k in range(SC_WIDTH - 1):
        off = tail - (SC_WIDTH - 1) + k
        conv = conv + wc_ref[k:k + 1, :] * buf[off:off + tm, :]
    sc_in = (b_ref[...].astype(F32) * conv).astype(BF16)
    y_sc = jnp.dot(sc_in, wsc_ref[...], preferred_element_type=F32)
    y_ssm = jnp.dot(yn_ref[...], wssm_ref[...], preferred_element_type=F32)
    g1 = _sigmoid(g1_ref[...].astype(F32) + bg1_ref[...])
    g2 = _sigmoid(g2_ref[...].astype(F32) + bg2_ref[...])
    o_ref[...] = (g1 * y_ssm + g2 * y_sc).astype(BF16)


def _resident(shape):
    return pl.BlockSpec(shape, lambda *_: (0,) * len(shape), pipeline_mode=pl.Buffered(1))


def _mix(y_norm, proj, b_gate, sc_conv_w, w_ssm_out, w_sc_out):
    t = y_norm.shape[0]
    tm = min(MIX_TM, t)
    d = D_MODEL
    sb, gb = P_SC // d, P_GATE // d
    prev_rows = lambda i: jnp.maximum(i * (tm // SUBLANES) - 1, 0)
    return pl.pallas_call(
        _mix_body,
        grid=(t // tm,),
        in_specs=[
            pl.BlockSpec((tm, SSM_D_INNER), lambda i: (i, 0)),
            pl.BlockSpec((tm, d), lambda i: (i, sb)),
            pl.BlockSpec((tm, d), lambda i: (i, sb + 1)),
            pl.BlockSpec((tm, d), lambda i: (i, sb + 2)),
            pl.BlockSpec((SUBLANES, d), lambda i: (prev_rows(i), sb + 1)),
            pl.BlockSpec((SUBLANES, d), lambda i: (prev_rows(i), sb + 2)),
            pl.BlockSpec((tm, d), lambda i: (i, gb)),
            pl.BlockSpec((tm, d), lambda i: (i, gb + 1)),
            pl.BlockSpec((1, d), lambda i: (0, 0)),
            pl.BlockSpec((1, d), lambda i: (0, 1)),
            pl.BlockSpec((SC_WIDTH, d), lambda i: (0, 0)),
            _resident((SSM_D_INNER, d)),
            _resident((d, d)),
        ],
        out_specs=pl.BlockSpec((tm, d), lambda i: (i, 0)),
        out_shape=jax.ShapeDtypeStruct((t, d), BF16),
        scratch_shapes=[pltpu.VMEM((tm + SUBLANES, d), F32)],
        compiler_params=_params(("parallel",)),
        name="mix",
    )(y_norm, proj, proj, proj, proj, proj, proj, proj, b_gate, b_gate, sc_conv_w, w_ssm_out, w_sc_out)


ROUTE_TM = 512
PACK_W = D_MODEL // 2


def _pack_rows(v):
    lo = lax.bitcast_convert_type(v[:, :PACK_W].astype(F32), jnp.uint32)
    hi = lax.bitcast_convert_type(v[:, PACK_W:].astype(F32), jnp.uint32)
    return hi | (lo >> 16)


def _unpack_rows(w):
    lo = lax.bitcast_convert_type(w << 16, F32).astype(BF16)
    hi = lax.bitcast_convert_type(w & jnp.uint32(0xFFFF0000), F32).astype(BF16)
    return jnp.concatenate([lo, hi], axis=1)


def _route_body(m_ref, x_ref, wo_ref, g_ref, wr_ref, br_ref,
                h_ref, xp_ref, e_ref, w_ref, r_ref, cnt_ref, carry):
    tm = m_ref.shape[0]

    @pl.when(pl.program_id(0) == 0)
    def _():
        carry[...] = jnp.zeros_like(carry)

    h = x_ref[...] + jnp.dot(m_ref[...], wo_ref[...], preferred_element_type=F32)
    h_ref[...] = h
    xn = h * lax.rsqrt(jnp.mean(h * h, axis=-1, keepdims=True) + EPS) * g_ref[...]
    xn_hi = xn.astype(BF16)
    xn_lo = (xn - xn_hi.astype(F32)).astype(BF16)
    xp_ref[...] = _pack_rows(xn_hi)

    ne = N_EXPERTS
    both = lax.dot_general(wr_ref[...], xn_hi, NT_DIMS, preferred_element_type=F32)
    cross = lax.dot_general(wr_ref[0:ne, :], xn_lo, NT_DIMS, preferred_element_type=F32)
    logits = both[0:ne] + both[ne:2 * ne] + cross + br_ref[...]
    eidx = lax.broadcasted_iota(jnp.int32, (N_EXPERTS, tm), 0)
    vals, hots = [], []
    for k in range(TOP_K):
        best = jnp.max(logits, axis=0, keepdims=True)
        arg = jnp.min(jnp.where(logits == best, eidx, N_EXPERTS), axis=0, keepdims=True)
        hot = eidx == arg
        e_ref[k:k + 1, :] = arg
        vals.append(best)
        hots.append(hot)
        logits = jnp.where(hot, -jnp.inf, logits)
    exps = [jnp.exp(v - vals[0]) for v in vals]
    denom = exps[0] + exps[1] + exps[2] + exps[3]
    for k in range(TOP_K):
        w_ref[k:k + 1, :] = exps[k] / denom

    cnt = (hots[0] | hots[1] | hots[2] | hots[3]).astype(F32)
    r_i = lax.broadcasted_iota(jnp.int32, (tm, tm), 0)
    c_i = lax.broadcasted_iota(jnp.int32, (tm, tm), 1)
    before = (r_i < c_i).astype(BF16)
    prior = carry[:, 0:1] + jnp.dot(cnt.astype(BF16), before, preferred_element_type=F32)
    for k in range(TOP_K):
        r_ref[k:k + 1, :] = jnp.sum(jnp.where(hots[k], prior, 0.0), axis=0, keepdims=True).astype(jnp.int32)
    total = carry[...] + jnp.sum(cnt, axis=1, keepdims=True)
    carry[...] = total
    cnt_ref[...] = total.astype(jnp.int32)


def _route(mixed, x, w_o, g_ffn, w_router_t, b_router):
    t = x.shape[0]
    tm = min(ROUTE_TM, t)
    d = D_MODEL
    return pl.pallas_call(
        _route_body,
        grid=(t // tm,),
        in_specs=[
            pl.BlockSpec((tm, d), lambda i: (i, 0)),
            pl.BlockSpec((tm, d), lambda i: (i, 0)),
            _resident((d, d)),
            pl.BlockSpec((1, d), lambda i: (0, 0)),
            pl.BlockSpec((2 * N_EXPERTS, d), lambda i: (0, 0)),
            pl.BlockSpec((N_EXPERTS, 1), lambda i: (0, 0)),
        ],
        out_specs=[
            pl.BlockSpec((tm, d), lambda i: (i, 0)),
            pl.BlockSpec((tm, PACK_W), lambda i: (i, 0)),
            pl.BlockSpec((TOP_K, tm), lambda i: (0, i)),
            pl.BlockSpec((TOP_K, tm), lambda i: (0, i)),
            pl.BlockSpec((TOP_K, tm), lambda i: (0, i)),
            pl.BlockSpec((N_EXPERTS, 128), lambda i: (0, 0)),
        ],
        out_shape=[
            jax.ShapeDtypeStruct((t, d), F32),
            jax.ShapeDtypeStruct((t, PACK_W), jnp.uint32),
            jax.ShapeDtypeStruct((TOP_K, t), jnp.int32),
            jax.ShapeDtypeStruct((TOP_K, t), F32),
            jax.ShapeDtypeStruct((TOP_K, t), jnp.int32),
            jax.ShapeDtypeStruct((N_EXPERTS, 128), jnp.int32),
        ],
        scratch_shapes=[pltpu.VMEM((N_EXPERTS, 128), F32)],
        compiler_params=_params(("arbitrary",)),
        name="route",
    )(mixed, x, w_o, g_ffn, w_router_t, b_router)


MOE_BM = 256
DISPATCH_TM = 256


def _dispatch_body(dest_ref, padrow_ref, npad_ref, nused_ref, xp_ref, xs_hbm, zblk, sem, zsem):
    tm = xp_ref.shape[0]
    bm = zblk.shape[0]
    nb = xs_hbm.shape[0] // bm

    @pl.when(pl.program_id(0) == 0)
    def _():
        zblk[...] = jnp.zeros_like(zblk)
        npad = npad_ref[0]
        nused = nused_ref[0]

        def zero_row(j):
            return pltpu.make_async_copy(zblk.at[pl.ds(0, 1), :], xs_hbm.at[pl.ds(padrow_ref[j], 1), :], zsem)

        def zero_block(b):
            return pltpu.make_async_copy(zblk, xs_hbm.at[pl.ds(pl.multiple_of(b * bm, bm), bm), :], zsem)

        def start_row(j, c):
            zero_row(j).start()
            return c

        def wait_row(j, c):
            zero_row(j).wait()
            return c

        def start_block(b, c):
            zero_block(b).start()
            return c

        def wait_block(b, c):
            zero_block(b).wait()
            return c

        lax.fori_loop(0, npad, start_row, 0)
        lax.fori_loop(nused, nb, start_block, 0)
        lax.fori_loop(0, npad, wait_row, 0)
        lax.fori_loop(nused, nb, wait_block, 0)

    def row_copy(r, k):
        return pltpu.make_async_copy(xp_ref.at[pl.ds(r, 1), :],
                                     xs_hbm.at[pl.ds(dest_ref[0, 0, k * tm + r], 1), :], sem)

    def start(r, c):
        for k in range(TOP_K):
            row_copy(r, k).start(priority=k % 2)
        return c

    lax.fori_loop(0, tm, start, 0)
    for k in range(TOP_K):
        pltpu.make_async_copy(xp_ref, xs_hbm.at[pl.ds(0, tm), :], sem).wait()


def _dispatch(xp, dest_tiles, pad_rows, n_pad, n_used, n_rows):
    t = xp.shape[0]
    tm = dest_tiles.shape[2] // TOP_K
    return pl.pallas_call(
        _dispatch_body,
        grid=(t // tm,),
        in_specs=[
            pl.BlockSpec((1, 1, TOP_K * tm), lambda i: (i, 0, 0), memory_space=pltpu.SMEM),
            pl.BlockSpec(memory_space=pltpu.SMEM),
            pl.BlockSpec(memory_space=pltpu.SMEM),
            pl.BlockSpec(memory_space=pltpu.SMEM),
            pl.BlockSpec((tm, PACK_W), lambda i: (i, 0)),
        ],
        out_specs=pl.BlockSpec(memory_space=pl.ANY),
        out_shape=jax.ShapeDtypeStruct((n_rows, PACK_W), jnp.uint32),
        scratch_shapes=[
            pltpu.VMEM((MOE_BM, PACK_W), jnp.uint32),
            pltpu.SemaphoreType.DMA(()),
            pltpu.SemaphoreType.DMA(()),
        ],
        compiler_params=_params(("arbitrary",)),
        name="dispatch",
    )(dest_tiles, pad_rows, n_pad, n_used, xp)


FFN_TF = 1024
FFN_TN = 1024
N_SLOTS = 2
STREAM_PRIORITY = 1


def _stream_expert_blocks(first_ref, nblk_ref, nused_ref, src_hbm, dst_hbm, ibuf, obuf, isem, osem,
                          prepare, compute):
    j, e = pl.program_id(0), pl.program_id(1)
    bm, width = obuf.shape[1], obuf.shape[2]
    col = pl.multiple_of(j * width, width)
    nblk = nblk_ref[e]
    first = first_ref[e]

    def rows(b):
        return pl.ds(pl.multiple_of(b * bm, bm), bm)

    def fetch(b, slot):
        return pltpu.make_async_copy(src_hbm.at[rows(b), :], ibuf.at[slot], isem.at[slot])

    def flush(b, slot):
        return pltpu.make_async_copy(obuf.at[slot], dst_hbm.at[rows(b), pl.ds(col, width)], osem.at[slot])

    @pl.when(nblk > 0)
    def _():
        fetch(first, 0).start(priority=STREAM_PRIORITY)
        prepare()

    def step(i, carry):
        slot = lax.rem(i, N_SLOTS)
        fetch(first + i, slot).wait()

        @pl.when(i + 1 < nblk)
        def _():
            fetch(first + i + 1, 1 - slot).start(priority=STREAM_PRIORITY)

        out = compute(ibuf[slot])

        @pl.when(i >= N_SLOTS)
        def _():
            flush(first + i - N_SLOTS, slot).wait()

        obuf[slot] = out
        flush(first + i, slot).start(priority=STREAM_PRIORITY)
        return carry

    lax.fori_loop(0, nblk, step, 0)

    @pl.when(nblk >= 2)
    def _():
        flush(first + nblk - 2, lax.rem(nblk, N_SLOTS)).wait()

    @pl.when(nblk >= 1)
    def _():
        flush(first + nblk - 1, lax.rem(nblk - 1, N_SLOTS)).wait()

    @pl.when(e == pl.num_programs(1) - 1)
    def _():
        obuf[0] = jnp.zeros(obuf.shape[1:], obuf.dtype)

        def zero(b, carry):
            cp = flush(b, 0)
            cp.start()
            cp.wait()
            return carry

        lax.fori_loop(nused_ref[0], dst_hbm.shape[0] // bm, zero, 0)


def _ffn_up_body(first_ref, nblk_ref, nused_ref, xs_hbm, wg_ref, wu_ref, bg_ref, bu_ref, h_hbm,
                 wg_bf, wu_bf, ibuf, obuf, isem, osem):
    def prepare():
        wg_bf[...] = wg_ref[...].astype(BF16)
        wu_bf[...] = wu_ref[...].astype(BF16)

    def compute(packed):
        x = _unpack_rows(packed)
        gate = jnp.dot(x, wg_bf[...], preferred_element_type=F32) + bg_ref[...]
        up = jnp.dot(x, wu_bf[...], preferred_element_type=F32) + bu_ref[...]
        gate = jnp.minimum(gate, SWIGLU_LIMIT)
        up = jnp.clip(up, -SWIGLU_LIMIT, SWIGLU_LIMIT)
        return ((up + 1.0) * gate * _sigmoid(SWIGLU_ALPHA * gate)).astype(BF16)

    _stream_expert_blocks(first_ref, nblk_ref, nused_ref, xs_hbm, h_hbm, ibuf, obuf, isem, osem,
                          prepare, compute)


def _ffn_up(first_blk, n_blk, n_used, xs, w_gate_up, b_gate_up):
    n_rows = xs.shape[0]
    tf = FFN_TF
    nf = D_FF // tf
    return pl.pallas_call(
        _ffn_up_body,
        grid_spec=pltpu.PrefetchScalarGridSpec(
            num_scalar_prefetch=3,
            grid=(nf, N_EXPERTS),
            in_specs=[
                pl.BlockSpec(memory_space=pl.ANY),
                pl.BlockSpec((None, D_MODEL, tf), lambda j, e, *_: (e, 0, j)),
                pl.BlockSpec((None, D_MODEL, tf), lambda j, e, *_: (e, 0, nf + j)),
                pl.BlockSpec((None, 1, tf), lambda j, e, *_: (e, 0, j)),
                pl.BlockSpec((None, 1, tf), lambda j, e, *_: (e, 0, nf + j)),
            ],
            out_specs=pl.BlockSpec(memory_space=pl.ANY),
            scratch_shapes=[
                pltpu.VMEM((D_MODEL, tf), BF16),
                pltpu.VMEM((D_MODEL, tf), BF16),
                pltpu.VMEM((N_SLOTS, MOE_BM, PACK_W), jnp.uint32),
                pltpu.VMEM((N_SLOTS, MOE_BM, tf), BF16),
                pltpu.SemaphoreType.DMA((N_SLOTS,)),
                pltpu.SemaphoreType.DMA((N_SLOTS,)),
            ],
        ),
        out_shape=jax.ShapeDtypeStruct((n_rows, D_FF), BF16),
        compiler_params=_params(("arbitrary", "arbitrary")),
        name="ffn_up",
    )(first_blk, n_blk, n_used, xs, w_gate_up, w_gate_up, b_gate_up, b_gate_up)


def _ffn_down_body(first_ref, nblk_ref, nused_ref, h_hbm, wd_ref, bd_ref, y_hbm, wd_bf, ibuf, obuf, isem, osem):
    def prepare():
        wd_bf[...] = wd_ref[...].astype(BF16)

    def compute(hid):
        return jnp.dot(hid, wd_bf[...], preferred_element_type=F32) + bd_ref[...]

    _stream_expert_blocks(first_ref, nblk_ref, nused_ref, h_hbm, y_hbm, ibuf, obuf, isem, osem,
                          prepare, compute)


def _ffn_down(first_blk, n_blk, n_used, h, w_down, b_down):
    n_rows = h.shape[0]
    tn = FFN_TN
    return pl.pallas_call(
        _ffn_down_body,
        grid_spec=pltpu.PrefetchScalarGridSpec(
            num_scalar_prefetch=3,
            grid=(D_MODEL // tn, N_EXPERTS),
            in_specs=[
                pl.BlockSpec(memory_space=pl.ANY),
                pl.BlockSpec((None, D_FF, tn), lambda j, e, *_: (e, 0, j)),
                pl.BlockSpec((None, 1, tn), lambda j, e, *_: (e, 0, j)),
            ],
            out_specs=pl.BlockSpec(memory_space=pl.ANY),
            scratch_shapes=[
                pltpu.VMEM((D_FF, tn), BF16),
                pltpu.VMEM((N_SLOTS, MOE_BM, D_FF), BF16),
                pltpu.VMEM((N_SLOTS, MOE_BM, tn), F32),
                pltpu.SemaphoreType.DMA((N_SLOTS,)),
                pltpu.SemaphoreType.DMA((N_SLOTS,)),
            ],
        ),
        out_shape=jax.ShapeDtypeStruct((n_rows, D_MODEL), F32),
        compiler_params=_params(("arbitrary", "arbitrary")),
        name="ffn_down",
    )(first_blk, n_blk, n_used, h, w_down, b_down)


COMBINE_TM = 128


def _combine_body(dest_ref, next_ref, h_ref, w_ref, g_ref, y_hbm, o_ref, gbuf, sem):
    i = pl.program_id(0)
    tm = h_ref.shape[0]

    def gather_tile(table_ref, slot):
        def start(r, c):
            for k in range(TOP_K):
                pltpu.make_async_copy(y_hbm.at[pl.ds(table_ref[0, 0, k * tm + r], 1), :],
                                      gbuf.at[slot, k, pl.ds(r, 1), :], sem.at[slot]).start(priority=k % 2)
            return c

        lax.fori_loop(0, tm, start, 0)

    slot = lax.rem(i, N_SLOTS)

    @pl.when(i == 0)
    def _():
        gather_tile(dest_ref, 0)

    @pl.when(i + 1 < pl.num_programs(0))
    def _():
        gather_tile(next_ref, 1 - slot)

    for k in range(TOP_K):
        pltpu.make_async_copy(y_hbm.at[pl.ds(0, tm), :], gbuf.at[slot, k], sem.at[slot]).wait()
    h = h_ref[...]
    for k in range(TOP_K):
        h = h + w_ref[:, k:k + 1] * gbuf[slot, k]
    o_ref[...] = h * lax.rsqrt(jnp.mean(h * h, axis=-1, keepdims=True) + EPS) * g_ref[...]


def _combine(dest_tiles, h1, w_cols, g_final, y):
    t = h1.shape[0]
    tm = dest_tiles.shape[2] // TOP_K
    d = D_MODEL
    last = t // tm - 1
    return pl.pallas_call(
        _combine_body,
        grid=(t // tm,),
        in_specs=[
            pl.BlockSpec((1, 1, TOP_K * tm), lambda i: (i, 0, 0), memory_space=pltpu.SMEM),
            pl.BlockSpec((1, 1, TOP_K * tm), lambda i: (jnp.minimum(i + 1, last), 0, 0), memory_space=pltpu.SMEM),
            pl.BlockSpec((tm, d), lambda i: (i, 0)),
            pl.BlockSpec((tm, TOP_K), lambda i: (i, 0)),
            pl.BlockSpec((1, d), lambda i: (0, 0)),
            pl.BlockSpec(memory_space=pl.ANY),
        ],
        out_specs=pl.BlockSpec((tm, d), lambda i: (i, 0)),
        out_shape=jax.ShapeDtypeStruct((t, d), F32),
        scratch_shapes=[pltpu.VMEM((N_SLOTS, TOP_K, tm, d), F32), pltpu.SemaphoreType.DMA((N_SLOTS,))],
        compiler_params=_params(("arbitrary",)),
        name="combine",
    )(dest_tiles, dest_tiles, h1, w_cols, g_final, y)


def _tile_major(a, tm):
    k, t = a.shape
    return a.reshape(k, t // tm, tm).transpose(1, 0, 2).reshape(t // tm, 1, k * tm)


def _routing_tables(top_e, rank, counts, t):
    bm = MOE_BM
    nb = (t * TOP_K) // bm + N_EXPERTS
    padded = (counts + bm - 1) // bm * bm
    pad_end = jnp.cumsum(padded)
    pad_start = pad_end - padded
    onehot = top_e[:, :, None] == jnp.arange(N_EXPERTS, dtype=jnp.int32)
    dest = rank + jnp.sum(jnp.where(onehot, pad_start, 0), axis=-1)
    first_blk = (pad_start // bm).astype(jnp.int32)
    n_blk = (padded // bm).astype(jnp.int32)
    n_used = (pad_end[-1] // bm).astype(jnp.int32).reshape(1)
    gap = padded - counts
    gap_end = jnp.cumsum(gap)
    j = jnp.arange(N_EXPERTS * bm, dtype=jnp.int32)
    ej = jnp.minimum(jnp.sum(j[:, None] >= gap_end[None, :], axis=1), N_EXPERTS - 1)
    pad_rows = (pad_start + counts)[ej] + j - (gap_end - gap)[ej]
    pad_rows = jnp.clip(pad_rows, 0, nb * bm - 1).astype(jnp.int32)
    n_pad = gap_end[-1].astype(jnp.int32).reshape(1)
    return dest.astype(jnp.int32), first_blk, n_blk, n_used, pad_rows, n_pad, nb * bm


def kernel(x, g_mix, w_in, ssm_conv_w, ssm_conv_b, ssm_dt_bias, ssm_a_log, ssm_d, ssm_norm_g, w_ssm_out,
           sc_conv_w, w_sc_out, b_gate, w_o, g_ffn, w_router, b_router, w_gate_up, b_gate_up, w_down,
           b_down, g_final):
    bsz, seq, d = x.shape
    t = bsz * seq
    assert bsz == 1 and d == D_MODEL and w_in.shape[0] == 1
    xt = x.reshape(t, d)
    wi = w_in[0]
    w_cat = jnp.concatenate(
        [wi[:, OFF_Z:OFF_XBC], wi[:, OFF_XBC:OFF_DT], wi[:, OFF_SC:OFF_GATE], wi[:, OFF_GATE:]], axis=1).astype(BF16)
    w_dt = wi[:, OFF_DT:OFF_SC]
    w_dt_hi = w_dt.astype(BF16)
    w_dt = jnp.concatenate([w_dt_hi, (w_dt - w_dt_hi.astype(F32)).astype(BF16)], axis=1)
    col = lambda a: a.reshape(-1, 1)
    row = lambda a: a.reshape(1, -1)

    proj, dt_raw = _inproj(xt, row(g_mix[0]), w_cat, w_dt)
    y_norm = _ssd(proj, dt_raw.T, ssm_conv_w[0], row(ssm_conv_b[0]), col(ssm_dt_bias[0]), col(ssm_a_log[0]),
                  col(ssm_d[0]), row(ssm_norm_g[0]))
    mixed = _mix(y_norm, proj, row(b_gate[0]), sc_conv_w[0], w_ssm_out[0].astype(BF16), w_sc_out[0].astype(BF16))
    wr_t = w_router[0].T
    wr_hi = wr_t.astype(BF16)
    wr_lo = (wr_t - wr_hi.astype(F32)).astype(BF16)
    h1, xp, top_e, top_w, rank, counts = _route(mixed, xt, w_o[0].astype(BF16), row(g_ffn[0]),
                                                jnp.concatenate([wr_hi, wr_lo], axis=0), col(b_router[0]))
    dest, first_blk, n_blk, n_used, pad_rows, n_pad, n_rows = _routing_tables(top_e, rank, counts[:, 0], t)
    xs = _dispatch(xp, _tile_major(dest, min(DISPATCH_TM, t)), pad_rows, n_pad, n_used, n_rows)
    hid = _ffn_up(first_blk, n_blk, n_used, xs, w_gate_up[0], b_gate_up[0].reshape(N_EXPERTS, 1, 2 * D_FF))
    y = _ffn_down(first_blk, n_blk, n_used, hid, w_down[0], b_down[0].reshape(N_EXPERTS, 1, D_MODEL))
    out = _combine(_tile_major(dest, min(COMBINE_TM, t)), h1, top_w.T, row(g_final), y)
    return out.reshape(bsz, seq, d)
```

```python
import functools

import jax
import jax.numpy as jnp
from jax import lax
from jax.experimental import pallas as pl
from jax.experimental.pallas import tpu as pltpu

D_MODEL = 2048
SSM_D_INNER = 2 * D_MODEL
SSM_HEAD_DIM = 64
SSM_N_HEADS = SSM_D_INNER // SSM_HEAD_DIM
SSM_N_GROUPS = 8
SSM_HEADS_PER_GROUP = SSM_N_HEADS // SSM_N_GROUPS
SSM_D_STATE = 128
SSM_CONV = 4
SSM_GN = SSM_N_GROUPS * SSM_D_STATE
SSM_CONV_DIM = SSM_D_INNER + 2 * SSM_GN
SSM_GROUP_CH = SSM_D_INNER // SSM_N_GROUPS
SC_DIM = D_MODEL
SC_WIDTH = 3
N_EXPERTS = 32
TOP_K = 4
D_FF = D_MODEL
SWIGLU_LIMIT = 7.0
SWIGLU_ALPHA = 1.702
EPS = 1e-5

OFF_Z = 0
OFF_XBC = OFF_Z + SSM_D_INNER
OFF_DT = OFF_XBC + SSM_CONV_DIM
OFF_SC = OFF_DT + SSM_N_HEADS
OFF_GATE = OFF_SC + 3 * SC_DIM
D_IN_PROJ = OFF_GATE + 2 * D_MODEL

P_Z = 0
P_XBC = P_Z + SSM_D_INNER
P_SC = P_XBC + SSM_CONV_DIM
P_GATE = P_SC + 3 * SC_DIM
P_TOTAL = P_GATE + 2 * D_MODEL

SUBLANES = 8
VMEM_LIMIT = 56 * 1024 * 1024

F32 = jnp.float32
BF16 = jnp.bfloat16
HIGHEST = lax.Precision.HIGHEST
NT_DIMS = (((1,), (1,)), ((), ()))
TN_DIMS = (((0,), (0,)), ((), ()))


def _sigmoid(v):
    return 1.0 / (1.0 + jnp.exp(-v))


def _params(semantics):
    return pltpu.CompilerParams(dimension_semantics=semantics, vmem_limit_bytes=VMEM_LIMIT)


PRENORM_TM = 512
INPROJ_TM = 1024
INPROJ_TN = 1024
LANES = 128
DT_SHIFT = OFF_SC - OFF_DT
ALIGNED_TILES = OFF_DT // INPROJ_TN
CAST_ROWS = 256


def _prenorm_body(x_ref, g_ref, wdt_ref, u_ref, dt_ref):
    x = x_ref[...]
    u = x * lax.rsqrt(jnp.mean(x * x, axis=-1, keepdims=True) + EPS) * g_ref[...]
    u_hi = u.astype(BF16)
    u_ref[...] = u_hi
    u_lo = (u - u_hi.astype(F32)).astype(BF16)
    both = jnp.dot(u_hi, wdt_ref[...], preferred_element_type=F32)
    cross = jnp.dot(u_lo, wdt_ref[...], preferred_element_type=F32)
    nh = SSM_N_HEADS
    dt_ref[...] = both[:, :nh] + both[:, nh:] + cross[:, :nh]


def _prenorm(x, g, w_dt):
    t = x.shape[0]
    tm = min(PRENORM_TM, t)
    return pl.pallas_call(
        _prenorm_body,
        grid=(t // tm,),
        in_specs=[
            pl.BlockSpec((tm, D_MODEL), lambda i: (i, 0)),
            pl.BlockSpec((1, D_MODEL), lambda i: (0, 0)),
            pl.BlockSpec((D_MODEL, 2 * SSM_N_HEADS), lambda i: (0, 0)),
        ],
        out_specs=[
            pl.BlockSpec((tm, D_MODEL), lambda i: (i, 0)),
            pl.BlockSpec((tm, SSM_N_HEADS), lambda i: (i, 0)),
        ],
        out_shape=[
            jax.ShapeDtypeStruct((t, D_MODEL), BF16),
            jax.ShapeDtypeStruct((t, SSM_N_HEADS), F32),
        ],
        compiler_params=_params(("parallel",)),
        name="prenorm",
    )(x, g, w_dt)


def _inproj_body(u_ref, w_ref, wx_ref, proj_ref, w_bf):
    j = pl.program_id(0)

    @pl.when(pl.program_id(1) == 0)
    def _():
        @pl.when(j < ALIGNED_TILES)
        def _():
            for r in range(0, D_MODEL, CAST_ROWS):
                w_bf[r:r + CAST_ROWS, :] = w_ref[r:r + CAST_ROWS, :].astype(BF16)

        @pl.when(j >= ALIGNED_TILES)
        def _():
            tn = w_bf.shape[1]
            for r in range(0, D_MODEL, CAST_ROWS):
                wide = jnp.concatenate([w_ref[r:r + CAST_ROWS, :], wx_ref[r:r + CAST_ROWS, :]], axis=1)
                w_bf[r:r + CAST_ROWS, :] = wide[:, DT_SHIFT:DT_SHIFT + tn].astype(BF16)

    proj_ref[...] = jnp.dot(u_ref[...], w_bf[...], preferred_element_type=F32).astype(BF16)


def _inproj(u, w_in):
    t = u.shape[0]
    tm = min(INPROJ_TM, t)
    tn = INPROJ_TN
    assert OFF_DT % tn == 0 and DT_SHIFT < LANES
    return pl.pallas_call(
        _inproj_body,
        grid=(P_TOTAL // tn, t // tm),
        in_specs=[
            pl.BlockSpec((tm, D_MODEL), lambda j, i: (i, 0)),
            pl.BlockSpec((None, D_MODEL, tn), lambda j, i: (0, 0, j)),
            pl.BlockSpec((None, D_MODEL, LANES), lambda j, i: (0, 0, (j + 1) * (tn // LANES))),
        ],
        out_specs=pl.BlockSpec((tm, tn), lambda j, i: (i, j)),
        out_shape=jax.ShapeDtypeStruct((t, P_TOTAL), BF16),
        scratch_shapes=[pltpu.VMEM((D_MODEL, tn), BF16)],
        compiler_params=_params(("arbitrary", "arbitrary")),
        name="inproj",
    )(u, w_in, w_in)


SSD_L = 256
HEAD_PAIR = 2 * SSM_HEAD_DIM


def _ssd_body(z_ref, x_ref, b_ref, c_ref, dt_ref, wx_ref, wb_ref, wc_ref, bx_ref, bb_ref, bc_ref,
              dtb_ref, alog_ref, d_ref, ng_ref, o_ref, s_ref, xbuf, bbuf, cbuf):
    L = x_ref.shape[0]
    tail = SUBLANES

    @pl.when(pl.program_id(1) == 0)
    def _():
        s_ref[...] = jnp.zeros_like(s_ref)
        xbuf[0:tail, :] = jnp.zeros((tail, xbuf.shape[1]), F32)
        bbuf[0:tail, :] = jnp.zeros((tail, bbuf.shape[1]), F32)
        cbuf[0:tail, :] = jnp.zeros((tail, cbuf.shape[1]), F32)

    def conv_silu(buf, in_ref, w_ref, bias_ref):
        buf[tail:tail + L, :] = in_ref[...].astype(F32)
        acc = bias_ref[...] + w_ref[SSM_CONV - 1:SSM_CONV, :] * buf[tail:tail + L, :]
        for k in range(SSM_CONV - 1):
            off = tail - (SSM_CONV - 1) + k
            acc = acc + w_ref[k:k + 1, :] * buf[off:off + L, :]
        buf[0:tail, :] = buf[L:L + tail, :]
        return acc * _sigmoid(acc)

    xs = conv_silu(xbuf, x_ref, wx_ref, bx_ref)
    bm = conv_silu(bbuf, b_ref, wb_ref, bb_ref).astype(BF16)
    cm = conv_silu(cbuf, c_ref, wc_ref, bc_ref).astype(BF16)

    dt_raw = dt_ref[...] + dtb_ref[...]
    dt = jnp.maximum(dt_raw, 0.0) + jnp.log(1.0 + jnp.exp(-jnp.abs(dt_raw)))
    da = dt * (-jnp.exp(alog_ref[...]))
    row = lax.broadcasted_iota(jnp.int32, (L, L), 0)
    col = lax.broadcasted_iota(jnp.int32, (L, L), 1)
    causal = row >= col
    incl = (row <= col).astype(BF16)
    da_hi = da.astype(BF16)
    rem = da - da_hi.astype(F32)
    da_mid = rem.astype(BF16)
    da_lo = (rem - da_mid.astype(F32)).astype(BF16)
    parts = jnp.dot(jnp.concatenate([da_hi, da_mid, da_lo], axis=0), incl, preferred_element_type=F32)
    hg = SSM_HEADS_PER_GROUP
    cs = parts[0:hg] + parts[hg:2 * hg] + parts[2 * hg:3 * hg]
    cs_end = cs[:, L - 1:L]
    flipped = jnp.concatenate([cs, dt, jnp.exp(cs_end - cs)], axis=0).T
    cs_t, dt_t, to_end_t = flipped[:, 0:hg], flipped[:, hg:2 * hg], flipped[:, 2 * hg:3 * hg]
    ecs_t = jnp.exp(cs_t)

    cb = lax.dot_general(cm, bm, NT_DIMS, preferred_element_type=F32)
    cb = jnp.where(causal, cb, 0.0)
    y_off = lax.dot_general(cm, s_ref[...].astype(BF16), NT_DIMS, preferred_element_type=F32)

    lane = lax.broadcasted_iota(jnp.int32, (L, HEAD_PAIR), 1)
    first = lane < SSM_HEAD_DIM
    ys = []
    xws = []
    for p in range(SSM_HEADS_PER_GROUP // 2):
        h0, h1 = 2 * p, 2 * p + 1
        sl = slice(p * HEAD_PAIR, (p + 1) * HEAD_PAIR)
        xp = xs[:, sl]
        pick = lambda a: jnp.where(first, a[:, h0:h0 + 1], a[:, h1:h1 + 1])
        xdt = xp * pick(dt_t)
        xdt_b = xdt.astype(BF16)
        yd = []
        for h in (h0, h1):
            seg = cs_t[:, h:h + 1] - cs[h:h + 1, :]
            m = cb * jnp.exp(jnp.minimum(seg, 0.0))
            yd.append(jnp.dot(m.astype(BF16), xdt_b, preferred_element_type=F32))
        y = jnp.where(first, yd[0], yd[1]) + y_off[:, sl] * pick(ecs_t)
        y = y + xp * jnp.where(first[0:1, :], d_ref[h0:h0 + 1, :], d_ref[h1:h1 + 1, :])
        ys.append(y)
        xws.append((xdt * pick(to_end_t)).astype(BF16))
    y = jnp.concatenate(ys, axis=1)
    xw = jnp.concatenate(xws, axis=1)

    upd = lax.dot_general(xw, bm, TN_DIMS, preferred_element_type=F32)
    chunk_decay = jnp.exp(cs_end)
    for h in range(SSM_HEADS_PER_GROUP):
        rows = slice(h * SSM_HEAD_DIM, (h + 1) * SSM_HEAD_DIM)
        s_ref[rows, :] = s_ref[rows, :] * chunk_decay[h:h + 1, :] + upd[rows, :]

    z = z_ref[...].astype(F32)
    v = y * (z * _sigmoid(z))
    v = v * lax.rsqrt(jnp.mean(v * v, axis=-1, keepdims=True) + EPS)
    o_ref[...] = (v * ng_ref[...]).astype(BF16)


def _ssd(proj, dt_rows, conv_w, conv_b, dt_bias, a_log, d_skip, norm_g):
    t = proj.shape[0]
    L = min(SSD_L, t)
    gc = SSM_GROUP_CH
    n = SSM_D_STATE
    zc, xc = P_Z // gc, P_XBC // gc
    bc, cc = (P_XBC + SSM_D_INNER) // n, (P_XBC + SSM_D_INNER + SSM_GN) // n
    wbc, wcc = SSM_D_INNER // n, (SSM_D_INNER + SSM_GN) // n
    hg = SSM_HEADS_PER_GROUP
    return pl.pallas_call(
        _ssd_body,
        grid=(SSM_N_GROUPS, t // L),
        in_specs=[
            pl.BlockSpec((L, gc), lambda g, i: (i, zc + g)),
            pl.BlockSpec((L, gc), lambda g, i: (i, xc + g)),
            pl.BlockSpec((L, n), lambda g, i: (i, bc + g)),
            pl.BlockSpec((L, n), lambda g, i: (i, cc + g)),
            pl.BlockSpec((hg, L), lambda g, i: (g, i)),
            pl.BlockSpec((SSM_CONV, gc), lambda g, i: (0, g)),
            pl.BlockSpec((SSM_CONV, n), lambda g, i: (0, wbc + g)),
            pl.BlockSpec((SSM_CONV, n), lambda g, i: (0, wcc + g)),
            pl.BlockSpec((1, gc), lambda g, i: (0, g)),
            pl.BlockSpec((1, n), lambda g, i: (0, wbc + g)),
            pl.BlockSpec((1, n), lambda g, i: (0, wcc + g)),
            pl.BlockSpec((hg, 1), lambda g, i: (g, 0)),
            pl.BlockSpec((hg, 1), lambda g, i: (g, 0)),
            pl.BlockSpec((hg, 1), lambda g, i: (g, 0)),
            pl.BlockSpec((1, gc), lambda g, i: (0, g)),
        ],
        out_specs=pl.BlockSpec((L, gc), lambda g, i: (i, g)),
        out_shape=jax.ShapeDtypeStruct((t, SSM_D_INNER), BF16),
        scratch_shapes=[
            pltpu.VMEM((gc, n), F32),
            pltpu.VMEM((L + SUBLANES, gc), F32),
            pltpu.VMEM((L + SUBLANES, n), F32),
            pltpu.VMEM((L + SUBLANES, n), F32),
        ],
        compiler_params=_params(("parallel", "arbitrary")),
        name="ssd",
    )(proj, proj, proj, proj, dt_rows, conv_w, conv_w, conv_w, conv_b, conv_b, conv_b,
      dt_bias, a_log, d_skip, norm_g)


MIX_TM = 256


def _mix_body(yn_ref, b_ref, c_ref, v_ref, cp_ref, vp_ref, g1_ref, g2_ref, bg1_ref, bg2_ref,
              wc_ref, wssm_ref, wsc_ref, o_ref, buf):
    tm = yn_ref.shape[0]
    tail = SUBLANES
    prev = cp_ref[...].astype(F32) * vp_ref[...].astype(F32)
    buf[0:tail, :] = jnp.where(pl.program_id(0) == 0, 0.0, prev)
    cv = c_ref[...].astype(F32) * v_ref[...].astype(F32)
    buf[tail:tail + tm, :] = cv
    conv = wc_ref[SC_WIDTH - 1:SC_WIDTH, :] * cv
    for k in range(SC_WIDTH - 1):
        off = tail - (SC_WIDTH - 1) + k
        conv = conv + wc_ref[k:k + 1, :] * buf[off:off + tm, :]
    sc_in = (b_ref[...].astype(F32) * conv).astype(BF16)
    y_sc = jnp.dot(sc_in, wsc_ref[...], preferred_element_type=F32)
    y_ssm = jnp.dot(yn_ref[...], wssm_ref[...], preferred_element_type=F32)
    g1 = _sigmoid(g1_ref[...].astype(F32) + bg1_ref[...])
    g2 = _sigmoid(g2_ref[...].astype(F32) + bg2_ref[...])
    o_ref[...] = (g1 * y_ssm + g2 * y_sc).astype(BF16)


def _resident(shape):
    return pl.BlockSpec(shape, lambda *_: (0,) * len(shape), pipeline_mode=pl.Buffered(1))


def _mix(y_norm, proj, b_gate, sc_conv_w, w_ssm_out, w_sc_out):
    t = y_norm.shape[0]
    tm = min(MIX_TM, t)
    d = D_MODEL
    sb, gb = P_SC // d, P_GATE // d
    prev_rows = lambda i: jnp.maximum(i * (tm // SUBLANES) - 1, 0)
    return pl.pallas_call(
        _mix_body,
        grid=(t // tm,),
        in_specs=[
            pl.BlockSpec((tm, SSM_D_INNER), lambda i: (i, 0)),
            pl.BlockSpec((tm, d), lambda i: (i, sb)),
            pl.BlockSpec((tm, d), lambda i: (i, sb + 1)),
            pl.BlockSpec((tm, d), lambda i: (i, sb + 2)),
            pl.BlockSpec((SUBLANES, d), lambda i: (prev_rows(i), sb + 1)),
            pl.BlockSpec((SUBLANES, d), lambda i: (prev_rows(i), sb + 2)),
            pl.BlockSpec((tm, d), lambda i: (i, gb)),
            pl.BlockSpec((tm, d), lambda i: (i, gb + 1)),
            pl.BlockSpec((1, d), lambda i: (0, 0)),
            pl.BlockSpec((1, d), lambda i: (0, 1)),
            pl.BlockSpec((SC_WIDTH, d), lambda i: (0, 0)),
            _resident((SSM_D_INNER, d)),
            _resident((d, d)),
        ],
        out_specs=pl.BlockSpec((tm, d), lambda i: (i, 0)),
        out_shape=jax.ShapeDtypeStruct((t, d), BF16),
        scratch_shapes=[pltpu.VMEM((tm + SUBLANES, d), F32)],
        compiler_params=_params(("parallel",)),
        name="mix",
    )(y_norm, proj, proj, proj, proj, proj, proj, proj, b_gate, b_gate, sc_conv_w, w_ssm_out, w_sc_out)


ROUTE_TM = 512
PACK_W = D_MODEL // 2


def _pack_rows(v):
    lo = lax.bitcast_convert_type(v[:, :PACK_W].astype(F32), jnp.uint32)
    hi = lax.bitcast_convert_type(v[:, PACK_W:].astype(F32), jnp.uint32)
    return hi | (lo >> 16)


def _unpack_rows(w):
    lo = lax.bitcast_convert_type(w << 16, F32).astype(BF16)
    hi = lax.bitcast_convert_type(w & jnp.uint32(0xFFFF0000), F32).astype(BF16)
    return jnp.concatenate([lo, hi], axis=1)


def _route_body(m_ref, x_ref, wo_ref, g_ref, wr_ref, br_ref,
                h_ref, xp_ref, e_ref, w_ref, r_ref, cnt_ref, carry):
    tm = m_ref.shape[0]

    @pl.when(pl.program_id(0) == 0)
    def _():
        carry[...] = jnp.zeros_like(carry)

    h = x_ref[...] + jnp.dot(m_ref[...], wo_ref[...], preferred_element_type=F32)
    h_ref[...] = h
    xn = h * lax.rsqrt(jnp.mean(h * h, axis=-1, keepdims=True) + EPS) * g_ref[...]
    xn_hi = xn.astype(BF16)
    xn_lo = (xn - xn_hi.astype(F32)).astype(BF16)
    xp_ref[...] = _pack_rows(xn_hi)

    ne = N_EXPERTS
    both = lax.dot_general(wr_ref[...], xn_hi, NT_DIMS, preferred_element_type=F32)
    cross = lax.dot_general(wr_ref[0:ne, :], xn_lo, NT_DIMS, preferred_element_type=F32)
    logits = both[0:ne] + both[ne:2 * ne] + cross + br_ref[...]
    eidx = lax.broadcasted_iota(jnp.int32, (N_EXPERTS, tm), 0)
    vals, hots = [], []
    for k in range(TOP_K):
        best = jnp.max(logits, axis=0, keepdims=True)
        arg = jnp.min(jnp.where(logits == best, eidx, N_EXPERTS), axis=0, keepdims=True)
        hot = eidx == arg
        e_ref[k:k + 1, :] = arg
        vals.append(best)
        hots.append(hot)
        logits = jnp.where(hot, -jnp.inf, logits)
    exps = [jnp.exp(v - vals[0]) for v in vals]
    denom = exps[0] + exps[1] + exps[2] + exps[3]
    for k in range(TOP_K):
        w_ref[k:k + 1, :] = exps[k] / denom

    cnt = (hots[0] | hots[1] | hots[2] | hots[3]).astype(F32)
    r_i = lax.broadcasted_iota(jnp.int32, (tm, tm), 0)
    c_i = lax.broadcasted_iota(jnp.int32, (tm, tm), 1)
    before = (r_i < c_i).astype(BF16)
    prior = carry[:, 0:1] + jnp.dot(cnt.astype(BF16), before, preferred_element_type=F32)
    for k in range(TOP_K):
        r_ref[k:k + 1, :] = jnp.sum(jnp.where(hots[k], prior, 0.0), axis=0, keepdims=True).astype(jnp.int32)
    total = carry[...] + jnp.sum(cnt, axis=1, keepdims=True)
    carry[...] = total
    cnt_ref[...] = total.astype(jnp.int32)


def _route(mixed, x, w_o, g_ffn, w_router_t, b_router):
    t = x.shape[0]
    tm = min(ROUTE_TM, t)
    d = D_MODEL
    return pl.pallas_call(
        _route_body,
        grid=(t // tm,),
        in_specs=[
            pl.BlockSpec((tm, d), lambda i: (i, 0)),
            pl.BlockSpec((tm, d), lambda i: (i, 0)),
            _resident((d, d)),
            pl.BlockSpec((1, d), lambda i: (0, 0)),
            pl.BlockSpec((2 * N_EXPERTS, d), lambda i: (0, 0)),
            pl.BlockSpec((N_EXPERTS, 1), lambda i: (0, 0)),
        ],
        out_specs=[
            pl.BlockSpec((tm, d), lambda i: (i, 0)),
            pl.BlockSpec((tm, PACK_W), lambda i: (i, 0)),
            pl.BlockSpec((TOP_K, tm), lambda i: (0, i)),
            pl.BlockSpec((TOP_K, tm), lambda i: (0, i)),
            pl.BlockSpec((TOP_K, tm), lambda i: (0, i)),
            pl.BlockSpec((N_EXPERTS, 128), lambda i: (0, 0)),
        ],
        out_shape=[
            jax.ShapeDtypeStruct((t, d), F32),
            jax.ShapeDtypeStruct((t, PACK_W), jnp.uint32),
            jax.ShapeDtypeStruct((TOP_K, t), jnp.int32),
            jax.ShapeDtypeStruct((TOP_K, t), F32),
            jax.ShapeDtypeStruct((TOP_K, t), jnp.int32),
            jax.ShapeDtypeStruct((N_EXPERTS, 128), jnp.int32),
        ],
        scratch_shapes=[pltpu.VMEM((N_EXPERTS, 128), F32)],
        compiler_params=_params(("arbitrary",)),
        name="route",
    )(mixed, x, w_o, g_ffn, w_router_t, b_router)


MOE_BM = 256
DISPATCH_TM = 256


def _dispatch_body(dest_ref, padrow_ref, npad_ref, nused_ref, xp_ref, xs_hbm, zblk, sem, zsem):
    tm = xp_ref.shape[0]
    bm = zblk.shape[0]
    nb = xs_hbm.shape[0] // bm

    @pl.when(pl.program_id(0) == 0)
    def _():
        zblk[...] = jnp.zeros_like(zblk)
        npad = npad_ref[0]
        nused = nused_ref[0]

        def zero_row(j):
            return pltpu.make_async_copy(zblk.at[pl.ds(0, 1), :], xs_hbm.at[pl.ds(padrow_ref[j], 1), :], zsem)

        def zero_block(b):
            return pltpu.make_async_copy(zblk, xs_hbm.at[pl.ds(pl.multiple_of(b * bm, bm), bm), :], zsem)

        def start_row(j, c):
            zero_row(j).start()
            return c

        def wait_row(j, c):
            zero_row(j).wait()
            return c

        def start_block(b, c):
            zero_block(b).start()
            return c

        def wait_block(b, c):
            zero_block(b).wait()
            return c

        lax.fori_loop(0, npad, start_row, 0)
        lax.fori_loop(nused, nb, start_block, 0)
        lax.fori_loop(0, npad, wait_row, 0)
        lax.fori_loop(nused, nb, wait_block, 0)

    def row_copy(r, k):
        return pltpu.make_async_copy(xp_ref.at[pl.ds(r, 1), :],
                                     xs_hbm.at[pl.ds(dest_ref[0, 0, k * tm + r], 1), :], sem)

    def start(r, c):
        for k in range(TOP_K):
            row_copy(r, k).start(priority=k % 2)
        return c

    lax.fori_loop(0, tm, start, 0)
    for k in range(TOP_K):
        pltpu.make_async_copy(xp_ref, xs_hbm.at[pl.ds(0, tm), :], sem).wait()


def _dispatch(xp, dest_tiles, pad_rows, n_pad, n_used, n_rows):
    t = xp.shape[0]
    tm = dest_tiles.shape[2] // TOP_K
    return pl.pallas_call(
        _dispatch_body,
        grid=(t // tm,),
        in_specs=[
            pl.BlockSpec((1, 1, TOP_K * tm), lambda i: (i, 0, 0), memory_space=pltpu.SMEM),
            pl.BlockSpec(memory_space=pltpu.SMEM),
            pl.BlockSpec(memory_space=pltpu.SMEM),
            pl.BlockSpec(memory_space=pltpu.SMEM),
            pl.BlockSpec((tm, PACK_W), lambda i: (i, 0)),
        ],
        out_specs=pl.BlockSpec(memory_space=pl.ANY),
        out_shape=jax.ShapeDtypeStruct((n_rows, PACK_W), jnp.uint32),
        scratch_shapes=[
            pltpu.VMEM((MOE_BM, PACK_W), jnp.uint32),
            pltpu.SemaphoreType.DMA(()),
            pltpu.SemaphoreType.DMA(()),
        ],
        compiler_params=_params(("arbitrary",)),
        name="dispatch",
    )(dest_tiles, pad_rows, n_pad, n_used, xp)


FFN_TF = 1024
FFN_TN = 1024
N_SLOTS = 2
IN_SLOTS = 3
STREAM_PRIORITY = 1


def _stream_expert_blocks(first_ref, nblk_ref, nused_ref, src_hbm, dst_hbm, ibuf, obuf, isem, osem,
                          prepare, compute):
    j, e = pl.program_id(0), pl.program_id(1)
    n_in = ibuf.shape[0]
    ahead = n_in - 1
    bm, width = obuf.shape[1], obuf.shape[2]
    half = bm // 2
    col = pl.multiple_of(j * width, width)
    nblk = nblk_ref[e]
    first = first_ref[e]

    def rows(b):
        return pl.ds(pl.multiple_of(b * bm, bm), bm)

    def fetch(b, slot):
        return pltpu.make_async_copy(src_hbm.at[rows(b), :], ibuf.at[slot], isem.at[slot])

    def fetch_start(b, slot):
        for p in range(2):
            src = src_hbm.at[pl.ds(pl.multiple_of(b * bm + p * half, half), half), :]
            pltpu.make_async_copy(src, ibuf.at[slot, pl.ds(p * half, half), :], isem.at[slot]).start(priority=p)

    def flush(b, slot):
        return pltpu.make_async_copy(obuf.at[slot], dst_hbm.at[rows(b), pl.ds(col, width)], osem.at[slot])

    for d in range(ahead):
        @pl.when(nblk > d)
        def _():
            fetch_start(first + d, d)

    @pl.when(nblk > 0)
    def _():
        prepare()

    def step(i, carry):
        slot = lax.rem(i, n_in)
        fetch(first + i, slot).wait()

        @pl.when(i + ahead < nblk)
        def _():
            fetch_start(first + i + ahead, lax.rem(i + ahead, n_in))

        out = compute(ibuf[slot])
        oslot = lax.rem(i, N_SLOTS)

        @pl.when(i >= N_SLOTS)
        def _():
            flush(first + i - N_SLOTS, oslot).wait()

        obuf[oslot] = out
        flush(first + i, oslot).start(priority=STREAM_PRIORITY)
        return carry

    lax.fori_loop(0, nblk, step, 0)

    @pl.when(nblk >= 2)
    def _():
        flush(first + nblk - 2, lax.rem(nblk, N_SLOTS)).wait()

    @pl.when(nblk >= 1)
    def _():
        flush(first + nblk - 1, lax.rem(nblk - 1, N_SLOTS)).wait()

    @pl.when(e == pl.num_programs(1) - 1)
    def _():
        obuf[0] = jnp.zeros(obuf.shape[1:], obuf.dtype)

        def zero(b, carry):
            cp = flush(b, 0)
            cp.start()
            cp.wait()
            return carry

        lax.fori_loop(nused_ref[0], dst_hbm.shape[0] // bm, zero, 0)


def _ffn_up_body(first_ref, nblk_ref, nused_ref, xs_hbm, wg_ref, wu_ref, bg_ref, bu_ref, h_hbm,
                 wg_bf, wu_bf, ibuf, obuf, isem, osem):
    def prepare():
        wg_bf[...] = wg_ref[...].astype(BF16)
        wu_bf[...] = wu_ref[...].astype(BF16)

    def compute(packed):
        x = _unpack_rows(packed)
        gate = jnp.dot(x, wg_bf[...], preferred_element_type=F32) + bg_ref[...]
        up = jnp.dot(x, wu_bf[...], preferred_element_type=F32) + bu_ref[...]
        gate = jnp.minimum(gate, SWIGLU_LIMIT)
        up = jnp.clip(up, -SWIGLU_LIMIT, SWIGLU_LIMIT)
        return ((up + 1.0) * gate * _sigmoid(SWIGLU_ALPHA * gate)).astype(BF16)

    _stream_expert_blocks(first_ref, nblk_ref, nused_ref, xs_hbm, h_hbm, ibuf, obuf, isem, osem,
                          prepare, compute)


def _ffn_up(first_blk, n_blk, n_used, xs, w_gate_up, b_gate_up):
    n_rows = xs.shape[0]
    tf = FFN_TF
    nf = D_FF // tf
    return pl.pallas_call(
        _ffn_up_body,
        grid_spec=pltpu.PrefetchScalarGridSpec(
            num_scalar_prefetch=3,
            grid=(nf, N_EXPERTS),
            in_specs=[
                pl.BlockSpec(memory_space=pl.ANY),
                pl.BlockSpec((None, D_MODEL, tf), lambda j, e, *_: (e, 0, j)),
                pl.BlockSpec((None, D_MODEL, tf), lambda j, e, *_: (e, 0, nf + j)),
                pl.BlockSpec((None, 1, tf), lambda j, e, *_: (e, 0, j)),
                pl.BlockSpec((None, 1, tf), lambda j, e, *_: (e, 0, nf + j)),
            ],
            out_specs=pl.BlockSpec(memory_space=pl.ANY),
            scratch_shapes=[
                pltpu.VMEM((D_MODEL, tf), BF16),
                pltpu.VMEM((D_MODEL, tf), BF16),
                pltpu.VMEM((IN_SLOTS, MOE_BM, PACK_W), jnp.uint32),
                pltpu.VMEM((N_SLOTS, MOE_BM, tf), BF16),
                pltpu.SemaphoreType.DMA((IN_SLOTS,)),
                pltpu.SemaphoreType.DMA((N_SLOTS,)),
            ],
        ),
        out_shape=jax.ShapeDtypeStruct((n_rows, D_FF), BF16),
        compiler_params=_params(("arbitrary", "arbitrary")),
        name="ffn_up",
    )(first_blk, n_blk, n_used, xs, w_gate_up, w_gate_up, b_gate_up, b_gate_up)


def _ffn_down_body(first_ref, nblk_ref, nused_ref, h_hbm, wd_ref, bd_ref, y_hbm, wd_bf, ibuf, obuf, isem, osem):
    def prepare():
        wd_bf[...] = wd_ref[...].astype(BF16)

    def compute(hid):
        return jnp.dot(hid, wd_bf[...], preferred_element_type=F32) + bd_ref[...]

    _stream_expert_blocks(first_ref, nblk_ref, nused_ref, h_hbm, y_hbm, ibuf, obuf, isem, osem,
                          prepare, compute)


def _ffn_down(first_blk, n_blk, n_used, h, w_down, b_down):
    n_rows = h.shape[0]
    tn = FFN_TN
    return pl.pallas_call(
        _ffn_down_body,
        grid_spec=pltpu.PrefetchScalarGridSpec(
            num_scalar_prefetch=3,
            grid=(D_MODEL // tn, N_EXPERTS),
            in_specs=[
                pl.BlockSpec(memory_space=pl.ANY),
                pl.BlockSpec((None, D_FF, tn), lambda j, e, *_: (e, 0, j)),
                pl.BlockSpec((None, 1, tn), lambda j, e, *_: (e, 0, j)),
            ],
            out_specs=pl.BlockSpec(memory_space=pl.ANY),
            scratch_shapes=[
                pltpu.VMEM((D_FF, tn), BF16),
                pltpu.VMEM((IN_SLOTS, MOE_BM, D_FF), BF16),
                pltpu.VMEM((N_SLOTS, MOE_BM, tn), F32),
                pltpu.SemaphoreType.DMA((IN_SLOTS,)),
                pltpu.SemaphoreType.DMA((N_SLOTS,)),
            ],
        ),
        out_shape=jax.ShapeDtypeStruct((n_rows, D_MODEL), F32),
        compiler_params=_params(("arbitrary", "arbitrary")),
        name="ffn_down",
    )(first_blk, n_blk, n_used, h, w_down, b_down)


COMBINE_TM = 128


def _combine_body(dest_ref, next_ref, h_ref, w_ref, g_ref, y_hbm, o_ref, gbuf, sem):
    i = pl.program_id(0)
    tm = h_ref.shape[0]

    def gather_tile(table_ref, slot):
        def start(r, c):
            for k in range(TOP_K):
                pltpu.make_async_copy(y_hbm.at[pl.ds(table_ref[0, 0, k * tm + r], 1), :],
                                      gbuf.at[slot, k, pl.ds(r, 1), :], sem.at[slot]).start(priority=k % 2)
            return c

        lax.fori_loop(0, tm, start, 0)

    slot = lax.rem(i, N_SLOTS)

    @pl.when(i == 0)
    def _():
        gather_tile(dest_ref, 0)

    @pl.when(i + 1 < pl.num_programs(0))
    def _():
        gather_tile(next_ref, 1 - slot)

    for k in range(TOP_K):
        pltpu.make_async_copy(y_hbm.at[pl.ds(0, tm), :], gbuf.at[slot, k], sem.at[slot]).wait()
    h = h_ref[...]
    for k in range(TOP_K):
        h = h + w_ref[:, k:k + 1] * gbuf[slot, k]
    o_ref[...] = h * lax.rsqrt(jnp.mean(h * h, axis=-1, keepdims=True) + EPS) * g_ref[...]


def _combine(dest_tiles, h1, w_cols, g_final, y):
    t = h1.shape[0]
    tm = dest_tiles.shape[2] // TOP_K
    d = D_MODEL
    last = t // tm - 1
    return pl.pallas_call(
        _combine_body,
        grid=(t // tm,),
        in_specs=[
            pl.BlockSpec((1, 1, TOP_K * tm), lambda i: (i, 0, 0), memory_space=pltpu.SMEM),
            pl.BlockSpec((1, 1, TOP_K * tm), lambda i: (jnp.minimum(i + 1, last), 0, 0), memory_space=pltpu.SMEM),
            pl.BlockSpec((tm, d), lambda i: (i, 0)),
            pl.BlockSpec((tm, TOP_K), lambda i: (i, 0)),
            pl.BlockSpec((1, d), lambda i: (0, 0)),
            pl.BlockSpec(memory_space=pl.ANY),
        ],
        out_specs=pl.BlockSpec((tm, d), lambda i: (i, 0)),
        out_shape=jax.ShapeDtypeStruct((t, d), F32),
        scratch_shapes=[pltpu.VMEM((N_SLOTS, TOP_K, tm, d), F32), pltpu.SemaphoreType.DMA((N_SLOTS,))],
        compiler_params=_params(("arbitrary",)),
        name="combine",
    )(dest_tiles, dest_tiles, h1, w_cols, g_final, y)


def _tile_major(a, tm):
    k, t = a.shape
    return a.reshape(k, t // tm, tm).transpose(1, 0, 2).reshape(t // tm, 1, k * tm)


def _routing_tables(top_e, rank, counts, t):
    bm = MOE_BM
    nb = (t * TOP_K) // bm + N_EXPERTS
    padded = (counts + bm - 1) // bm * bm
    pad_end = jnp.cumsum(padded)
    pad_start = pad_end - padded
    onehot = top_e[:, :, None] == jnp.arange(N_EXPERTS, dtype=jnp.int32)
    dest = rank + jnp.sum(jnp.where(onehot, pad_start, 0), axis=-1)
    first_blk = (pad_start // bm).astype(jnp.int32)
    n_blk = (padded // bm).astype(jnp.int32)
    n_used = (pad_end[-1] // bm).astype(jnp.int32).reshape(1)
    gap = padded - counts
    gap_end = jnp.cumsum(gap)
    j = jnp.arange(N_EXPERTS * bm, dtype=jnp.int32)
    ej = jnp.minimum(jnp.sum(j[:, None] >= gap_end[None, :], axis=1), N_EXPERTS - 1)
    pad_rows = (pad_start + counts)[ej] + j - (gap_end - gap)[ej]
    pad_rows = jnp.clip(pad_rows, 0, nb * bm - 1).astype(jnp.int32)
    n_pad = gap_end[-1].astype(jnp.int32).reshape(1)
    return dest.astype(jnp.int32), first_blk, n_blk, n_used, pad_rows, n_pad, nb * bm


def kernel(x, g_mix, w_in, ssm_conv_w, ssm_conv_b, ssm_dt_bias, ssm_a_log, ssm_d, ssm_norm_g, w_ssm_out,
           sc_conv_w, w_sc_out, b_gate, w_o, g_ffn, w_router, b_router, w_gate_up, b_gate_up, w_down,
           b_down, g_final):
    bsz, seq, d = x.shape
    t = bsz * seq
    assert bsz == 1 and d == D_MODEL and w_in.shape[0] == 1
    xt = x.reshape(t, d)
    w_dt = w_in[0, :, OFF_DT:OFF_SC]
    w_dt_hi = w_dt.astype(BF16)
    w_dt = jnp.concatenate([w_dt_hi, (w_dt - w_dt_hi.astype(F32)).astype(BF16)], axis=1)
    col = lambda a: a.reshape(-1, 1)
    row = lambda a: a.reshape(1, -1)

    u, dt_raw = _prenorm(xt, row(g_mix[0]), w_dt)
    proj = _inproj(u, w_in)
    y_norm = _ssd(proj, dt_raw.T, ssm_conv_w[0], row(ssm_conv_b[0]), col(ssm_dt_bias[0]), col(ssm_a_log[0]),
                  col(ssm_d[0]), row(ssm_norm_g[0]))
    mixed = _mix(y_norm, proj, row(b_gate[0]), sc_conv_w[0], w_ssm_out[0].astype(BF16), w_sc_out[0].astype(BF16))
    wr_t = w_router[0].T
    wr_hi = wr_t.astype(BF16)
    wr_lo = (wr_t - wr_hi.astype(F32)).astype(BF16)
    h1, xp, top_e, top_w, rank, counts = _route(mixed, xt, w_o[0].astype(BF16), row(g_ffn[0]),
                                                jnp.concatenate([wr_hi, wr_lo], axis=0), col(b_router[0]))
    dest, first_blk, n_blk, n_used, pad_rows, n_pad, n_rows = _routing_tables(top_e, rank, counts[:, 0], t)
    xs = _dispatch(xp, _tile_major(dest, min(DISPATCH_TM, t)), pad_rows, n_pad, n_used, n_rows)
    hid = _ffn_up(first_blk, n_blk, n_used, xs, w_gate_up[0], b_gate_up[0].reshape(N_EXPERTS, 1, 2 * D_FF))
    y = _ffn_down(first_blk, n_blk, n_used, hid, w_down[0], b_down[0].reshape(N_EXPERTS, 1, D_MODEL))
    out = _combine(_tile_major(dest, min(COMBINE_TM, t)), h1, top_w.T, row(g_final), y)
    return out.reshape(bsz, seq, d)
```

```python
import functools

import jax
import jax.numpy as jnp
from jax import lax
from jax.experimental import pallas as pl
from jax.experimental.pallas import tpu as pltpu

D_MODEL = 2048
SSM_D_INNER = 2 * D_MODEL
SSM_HEAD_DIM = 64
SSM_N_HEADS = SSM_D_INNER // SSM_HEAD_DIM
SSM_N_GROUPS = 8
SSM_HEADS_PER_GROUP = SSM_N_HEADS // SSM_N_GROUPS
SSM_D_STATE = 128
SSM_CONV = 4
SSM_GN = SSM_N_GROUPS * SSM_D_STATE
SSM_CONV_DIM = SSM_D_INNER + 2 * SSM_GN
SSM_GROUP_CH = SSM_D_INNER // SSM_N_GROUPS
SC_DIM = D_MODEL
SC_WIDTH = 3
N_EXPERTS = 32
TOP_K = 4
D_FF = D_MODEL
SWIGLU_LIMIT = 7.0
SWIGLU_ALPHA = 1.702
EPS = 1e-5

OFF_Z = 0
OFF_XBC = OFF_Z + SSM_D_INNER
OFF_DT = OFF_XBC + SSM_CONV_DIM
OFF_SC = OFF_DT + SSM_N_HEADS
OFF_GATE = OFF_SC + 3 * SC_DIM
D_IN_PROJ = OFF_GATE + 2 * D_MODEL

P_Z = 0
P_XBC = P_Z + SSM_D_INNER
P_SC = P_XBC + SSM_CONV_DIM
P_GATE = P_SC + 3 * SC_DIM
P_TOTAL = P_GATE + 2 * D_MODEL

SUBLANES = 8
VMEM_LIMIT = 56 * 1024 * 1024

F32 = jnp.float32
BF16 = jnp.bfloat16
HIGHEST = lax.Precision.HIGHEST
NT_DIMS = (((1,), (1,)), ((), ()))
TN_DIMS = (((0,), (0,)), ((), ()))


def _sigmoid(v):
    return 1.0 / (1.0 + jnp.exp(-v))


def _params(semantics):
    return pltpu.CompilerParams(dimension_semantics=semantics, vmem_limit_bytes=VMEM_LIMIT)


PRENORM_TM = 512
INPROJ_TM = 1024
INPROJ_TN = 1024
LANES = 128
DT_SHIFT = OFF_SC - OFF_DT
ALIGNED_TILES = OFF_DT // INPROJ_TN
CAST_ROWS = 256


def _prenorm_body(x_ref, g_ref, wdt_ref, u_ref, dt_ref):
    x = x_ref[...]
    u = x * lax.rsqrt(jnp.mean(x * x, axis=-1, keepdims=True) + EPS) * g_ref[...]
    u_hi = u.astype(BF16)
    u_ref[...] = u_hi
    u_lo = (u - u_hi.astype(F32)).astype(BF16)
    both = jnp.dot(u_hi, wdt_ref[...], preferred_element_type=F32)
    cross = jnp.dot(u_lo, wdt_ref[...], preferred_element_type=F32)
    nh = SSM_N_HEADS
    dt_ref[...] = both[:, :nh] + both[:, nh:] + cross[:, :nh]


def _prenorm(x, g, w_dt):
    t = x.shape[0]
    tm = min(PRENORM_TM, t)
    return pl.pallas_call(
        _prenorm_body,
        grid=(t // tm,),
        in_specs=[
            pl.BlockSpec((tm, D_MODEL), lambda i: (i, 0)),
            pl.BlockSpec((1, D_MODEL), lambda i: (0, 0)),
            pl.BlockSpec((D_MODEL, 2 * SSM_N_HEADS), lambda i: (0, 0)),
        ],
        out_specs=[
            pl.BlockSpec((tm, D_MODEL), lambda i: (i, 0)),
            pl.BlockSpec((tm, SSM_N_HEADS), lambda i: (i, 0)),
        ],
        out_shape=[
            jax.ShapeDtypeStruct((t, D_MODEL), BF16),
            jax.ShapeDtypeStruct((t, SSM_N_HEADS), F32),
        ],
        compiler_params=_params(("parallel",)),
        name="prenorm",
    )(x, g, w_dt)


def _inproj_body(u_ref, wt_ref, proj_ref, w_bf):
    @pl.when(pl.program_id(1) == 0)
    def _():
        for r in range(0, w_bf.shape[0], CAST_ROWS):
            w_bf[r:r + CAST_ROWS, :] = wt_ref[r:r + CAST_ROWS, :].astype(BF16)

    proj_ref[...] = lax.dot_general(u_ref[...], w_bf[...], NT_DIMS, preferred_element_type=F32).astype(BF16)


def _inproj(u, w_in_t):
    t = u.shape[0]
    tm = min(INPROJ_TM, t)
    tn = INPROJ_TN
    assert OFF_DT % tn == 0 and DT_SHIFT % SUBLANES == 0
    first_row = lambda j: pl.multiple_of(j * tn + jnp.where(j >= ALIGNED_TILES, DT_SHIFT, 0), SUBLANES)
    return pl.pallas_call(
        _inproj_body,
        grid=(P_TOTAL // tn, t // tm),
        in_specs=[
            pl.BlockSpec((tm, D_MODEL), lambda j, i: (i, 0)),
            pl.BlockSpec((pl.Element(tn), pl.Element(D_MODEL)), lambda j, i: (first_row(j), 0)),
        ],
        out_specs=pl.BlockSpec((tm, tn), lambda j, i: (i, j)),
        out_shape=jax.ShapeDtypeStruct((t, P_TOTAL), BF16),
        scratch_shapes=[pltpu.VMEM((tn, D_MODEL), BF16)],
        compiler_params=_params(("arbitrary", "arbitrary")),
        name="inproj",
    )(u, w_in_t)


SSD_L = 256
HEAD_PAIR = 2 * SSM_HEAD_DIM


def _ssd_body(z_ref, x_ref, b_ref, c_ref, dt_ref, wx_ref, wb_ref, wc_ref, bx_ref, bb_ref, bc_ref,
              dtb_ref, alog_ref, d_ref, ng_ref, o_ref, s_ref, xbuf, bbuf, cbuf):
    L = x_ref.shape[0]
    tail = SUBLANES

    @pl.when(pl.program_id(1) == 0)
    def _():
        s_ref[...] = jnp.zeros_like(s_ref)
        xbuf[0:tail, :] = jnp.zeros((tail, xbuf.shape[1]), F32)
        bbuf[0:tail, :] = jnp.zeros((tail, bbuf.shape[1]), F32)
        cbuf[0:tail, :] = jnp.zeros((tail, cbuf.shape[1]), F32)

    def conv_silu(buf, in_ref, w_ref, bias_ref):
        buf[tail:tail + L, :] = in_ref[...].astype(F32)
        acc = bias_ref[...] + w_ref[SSM_CONV - 1:SSM_CONV, :] * buf[tail:tail + L, :]
        for k in range(SSM_CONV - 1):
            off = tail - (SSM_CONV - 1) + k
            acc = acc + w_ref[k:k + 1, :] * buf[off:off + L, :]
        buf[0:tail, :] = buf[L:L + tail, :]
        return acc * _sigmoid(acc)

    xs = conv_silu(xbuf, x_ref, wx_ref, bx_ref)
    bm = conv_silu(bbuf, b_ref, wb_ref, bb_ref).astype(BF16)
    cm = conv_silu(cbuf, c_ref, wc_ref, bc_ref).astype(BF16)

    dt_raw = dt_ref[...] + dtb_ref[...]
    dt = jnp.maximum(dt_raw, 0.0) + jnp.log(1.0 + jnp.exp(-jnp.abs(dt_raw)))
    da = dt * (-jnp.exp(alog_ref[...]))
    row = lax.broadcasted_iota(jnp.int32, (L, L), 0)
    col = lax.broadcasted_iota(jnp.int32, (L, L), 1)
    causal = row >= col
    incl = (row <= col).astype(BF16)
    da_hi = da.astype(BF16)
    rem = da - da_hi.astype(F32)
    da_mid = rem.astype(BF16)
    da_lo = (rem - da_mid.astype(F32)).astype(BF16)
    parts = jnp.dot(jnp.concatenate([da_hi, da_mid, da_lo], axis=0), incl, preferred_element_type=F32)
    hg = SSM_HEADS_PER_GROUP
    cs = parts[0:hg] + parts[hg:2 * hg] + parts[2 * hg:3 * hg]
    cs_end = cs[:, L - 1:L]
    flipped = jnp.concatenate([cs, dt, jnp.exp(cs_end - cs)], axis=0).T
    cs_t, dt_t, to_end_t = flipped[:, 0:hg], flipped[:, hg:2 * hg], flipped[:, 2 * hg:3 * hg]
    ecs_t = jnp.exp(cs_t)

    cb = lax.dot_general(cm, bm, NT_DIMS, preferred_element_type=F32)
    cb = jnp.where(causal, cb, 0.0)
    y_off = lax.dot_general(cm, s_ref[...].astype(BF16), NT_DIMS, preferred_element_type=F32)

    lane = lax.broadcasted_iota(jnp.int32, (L, HEAD_PAIR), 1)
    first = lane < SSM_HEAD_DIM
    ys = []
    xws = []
    for p in range(SSM_HEADS_PER_GROUP // 2):
        h0, h1 = 2 * p, 2 * p + 1
        sl = slice(p * HEAD_PAIR, (p + 1) * HEAD_PAIR)
        xp = xs[:, sl]
        pick = lambda a: jnp.where(first, a[:, h0:h0 + 1], a[:, h1:h1 + 1])
        xdt = xp * pick(dt_t)
        xdt_b = xdt.astype(BF16)
        yd = []
        for h in (h0, h1):
            seg = cs_t[:, h:h + 1] - cs[h:h + 1, :]
            m = cb * jnp.exp(jnp.minimum(seg, 0.0))
            yd.append(jnp.dot(m.astype(BF16), xdt_b, preferred_element_type=F32))
        y = jnp.where(first, yd[0], yd[1]) + y_off[:, sl] * pick(ecs_t)
        y = y + xp * jnp.where(first[0:1, :], d_ref[h0:h0 + 1, :], d_ref[h1:h1 + 1, :])
        ys.append(y)
        xws.append((xdt * pick(to_end_t)).astype(BF16))
    y = jnp.concatenate(ys, axis=1)
    xw = jnp.concatenate(xws, axis=1)

    upd = lax.dot_general(xw, bm, TN_DIMS, preferred_element_type=F32)
    chunk_decay = jnp.exp(cs_end)
    for h in range(SSM_HEADS_PER_GROUP):
        rows = slice(h * SSM_HEAD_DIM, (h + 1) * SSM_HEAD_DIM)
        s_ref[rows, :] = s_ref[rows, :] * chunk_decay[h:h + 1, :] + upd[rows, :]

    z = z_ref[...].astype(F32)
    v = y * (z * _sigmoid(z))
    v = v * lax.rsqrt(jnp.mean(v * v, axis=-1, keepdims=True) + EPS)
    o_ref[...] = (v * ng_ref[...]).astype(BF16)


def _ssd(proj, dt_rows, conv_w, conv_b, dt_bias, a_log, d_skip, norm_g):
    t = proj.shape[0]
    L = min(SSD_L, t)
    gc = SSM_GROUP_CH
    n = SSM_D_STATE
    zc, xc = P_Z // gc, P_XBC // gc
    bc, cc = (P_XBC + SSM_D_INNER) // n, (P_XBC + SSM_D_INNER + SSM_GN) // n
    wbc, wcc = SSM_D_INNER // n, (SSM_D_INNER + SSM_GN) // n
    hg = SSM_HEADS_PER_GROUP
    return pl.pallas_call(
        _ssd_body,
        grid=(SSM_N_GROUPS, t // L),
        in_specs=[
            pl.BlockSpec((L, gc), lambda g, i: (i, zc + g)),
            pl.BlockSpec((L, gc), lambda g, i: (i, xc + g)),
            pl.BlockSpec((L, n), lambda g, i: (i, bc + g)),
            pl.BlockSpec((L, n), lambda g, i: (i, cc + g)),
            pl.BlockSpec((hg, L), lambda g, i: (g, i)),
            pl.BlockSpec((SSM_CONV, gc), lambda g, i: (0, g)),
            pl.BlockSpec((SSM_CONV, n), lambda g, i: (0, wbc + g)),
            pl.BlockSpec((SSM_CONV, n), lambda g, i: (0, wcc + g)),
            pl.BlockSpec((1, gc), lambda g, i: (0, g)),
            pl.BlockSpec((1, n), lambda g, i: (0, wbc + g)),
            pl.BlockSpec((1, n), lambda g, i: (0, wcc + g)),
            pl.BlockSpec((hg, 1), lambda g, i: (g, 0)),
            pl.BlockSpec((hg, 1), lambda g, i: (g, 0)),
            pl.BlockSpec((hg, 1), lambda g, i: (g, 0)),
            pl.BlockSpec((1, gc), lambda g, i: (0, g)),
        ],
        out_specs=pl.BlockSpec((L, gc), lambda g, i: (i, g)),
        out_shape=jax.ShapeDtypeStruct((t, SSM_D_INNER), BF16),
        scratch_shapes=[
            pltpu.VMEM((gc, n), F32),
            pltpu.VMEM((L + SUBLANES, gc), F32),
            pltpu.VMEM((L + SUBLANES, n), F32),
            pltpu.VMEM((L + SUBLANES, n), F32),
        ],
        compiler_params=_params(("parallel", "arbitrary")),
        name="ssd",
    )(proj, proj, proj, proj, dt_rows, conv_w, conv_w, conv_w, conv_b, conv_b, conv_b,
      dt_bias, a_log, d_skip, norm_g)


MIX_TM = 256


def _mix_body(yn_ref, b_ref, c_ref, v_ref, cp_ref, vp_ref, g1_ref, g2_ref, bg1_ref, bg2_ref,
              wc_ref, wssm_ref, wsc_ref, o_ref, buf):
    tm = yn_ref.shape[0]
    tail = SUBLANES
    prev = cp_ref[...].astype(F32) * vp_ref[...].astype(F32)
    buf[0:tail, :] = jnp.where(pl.program_id(0) == 0, 0.0, prev)
    cv = c_ref[...].astype(F32) * v_ref[...].astype(F32)
    buf[tail:tail + tm, :] = cv
    conv = wc_ref[SC_WIDTH - 1:SC_WIDTH, :] * cv
    for k in range(SC_WIDTH - 1):
        off = tail - (SC_WIDTH - 1) + k
        conv = conv + wc_ref[k:k + 1, :] * buf[off:off + tm, :]
    sc_in = (b_ref[...].astype(F32) * conv).astype(BF16)
    y_sc = jnp.dot(sc_in, wsc_ref[...], preferred_element_type=F32)
    y_ssm = jnp.dot(yn_ref[...], wssm_ref[...], preferred_element_type=F32)
    g1 = _sigmoid(g1_ref[...].astype(F32) + bg1_ref[...])
    g2 = _sigmoid(g2_ref[...].astype(F32) + bg2_ref[...])
    o_ref[...] = (g1 * y_ssm + g2 * y_sc).astype(BF16)


def _resident(shape):
    return pl.BlockSpec(shape, lambda *_: (0,) * len(shape), pipeline_mode=pl.Buffered(1))


def _mix(y_norm, proj, b_gate, sc_conv_w, w_ssm_out, w_sc_out):
    t = y_norm.shape[0]
    tm = min(MIX_TM, t)
    d = D_MODEL
    sb, gb = P_SC // d, P_GATE // d
    prev_rows = lambda i: jnp.maximum(i * (tm // SUBLANES) - 1, 0)
    return pl.pallas_call(
        _mix_body,
        grid=(t // tm,),
        in_specs=[
            pl.BlockSpec((tm, SSM_D_INNER), lambda i: (i, 0)),
            pl.BlockSpec((tm, d), lambda i: (i, sb)),
            pl.BlockSpec((tm, d), lambda i: (i, sb + 1)),
            pl.BlockSpec((tm, d), lambda i: (i, sb + 2)),
            pl.BlockSpec((SUBLANES, d), lambda i: (prev_rows(i), sb + 1)),
            pl.BlockSpec((SUBLANES, d), lambda i: (prev_rows(i), sb + 2)),
            pl.BlockSpec((tm, d), lambda i: (i, gb)),
            pl.BlockSpec((tm, d), lambda i: (i, gb + 1)),
            pl.BlockSpec((1, d), lambda i: (0, 0)),
            pl.BlockSpec((1, d), lambda i: (0, 1)),
            pl.BlockSpec((SC_WIDTH, d), lambda i: (0, 0)),
            _resident((SSM_D_INNER, d)),
            _resident((d, d)),
        ],
        out_specs=pl.BlockSpec((tm, d), lambda i: (i, 0)),
        out_shape=jax.ShapeDtypeStruct((t, d), BF16),
        scratch_shapes=[pltpu.VMEM((tm + SUBLANES, d), F32)],
        compiler_params=_params(("parallel",)),
        name="mix",
    )(y_norm, proj, proj, proj, proj, proj, proj, proj, b_gate, b_gate, sc_conv_w, w_ssm_out, w_sc_out)


ROUTE_TM = 512
PACK_W = D_MODEL // 2


def _pack_rows(v):
    lo = lax.bitcast_convert_type(v[:, :PACK_W].astype(F32), jnp.uint32)
    hi = lax.bitcast_convert_type(v[:, PACK_W:].astype(F32), jnp.uint32)
    return hi | (lo >> 16)


def _unpack_rows(w):
    lo = lax.bitcast_convert_type(w << 16, F32).astype(BF16)
    hi = lax.bitcast_convert_type(w & jnp.uint32(0xFFFF0000), F32).astype(BF16)
    return jnp.concatenate([lo, hi], axis=1)


def _route_body(m_ref, x_ref, wo_ref, g_ref, wr_ref, br_ref,
                h_ref, xp_ref, e_ref, w_ref, r_ref, cnt_ref, carry):
    tm = m_ref.shape[0]

    @pl.when(pl.program_id(0) == 0)
    def _():
        carry[...] = jnp.zeros_like(carry)

    h = x_ref[...] + jnp.dot(m_ref[...], wo_ref[...], preferred_element_type=F32)
    h_ref[...] = h
    xn = h * lax.rsqrt(jnp.mean(h * h, axis=-1, keepdims=True) + EPS) * g_ref[...]
    xn_hi = xn.astype(BF16)
    xn_lo = (xn - xn_hi.astype(F32)).astype(BF16)
    xp_ref[...] = _pack_rows(xn_hi)

    ne = N_EXPERTS
    both = lax.dot_general(wr_ref[...], xn_hi, NT_DIMS, preferred_element_type=F32)
    cross = lax.dot_general(wr_ref[0:ne, :], xn_lo, NT_DIMS, preferred_element_type=F32)
    logits = both[0:ne] + both[ne:2 * ne] + cross + br_ref[...]
    eidx = lax.broadcasted_iota(jnp.int32, (N_EXPERTS, tm), 0)
    vals, hots = [], []
    for k in range(TOP_K):
        best = jnp.max(logits, axis=0, keepdims=True)
        arg = jnp.min(jnp.where(logits == best, eidx, N_EXPERTS), axis=0, keepdims=True)
        hot = eidx == arg
        e_ref[k:k + 1, :] = arg
        vals.append(best)
        hots.append(hot)
        logits = jnp.where(hot, -jnp.inf, logits)
    exps = [jnp.exp(v - vals[0]) for v in vals]
    denom = exps[0] + exps[1] + exps[2] + exps[3]
    for k in range(TOP_K):
        w_ref[k:k + 1, :] = exps[k] / denom

    cnt = (hots[0] | hots[1] | hots[2] | hots[3]).astype(F32)
    r_i = lax.broadcasted_iota(jnp.int32, (tm, tm), 0)
    c_i = lax.broadcasted_iota(jnp.int32, (tm, tm), 1)
    before = (r_i < c_i).astype(BF16)
    prior = carry[:, 0:1] + jnp.dot(cnt.astype(BF16), before, preferred_element_type=F32)
    for k in range(TOP_K):
        r_ref[k:k + 1, :] = jnp.sum(jnp.where(hots[k], prior, 0.0), axis=0, keepdims=True).astype(jnp.int32)
    total = carry[...] + jnp.sum(cnt, axis=1, keepdims=True)
    carry[...] = total
    cnt_ref[...] = total.astype(jnp.int32)


def _route(mixed, x, w_o, g_ffn, w_router_t, b_router):
    t = x.shape[0]
    tm = min(ROUTE_TM, t)
    d = D_MODEL
    return pl.pallas_call(
        _route_body,
        grid=(t // tm,),
        in_specs=[
            pl.BlockSpec((tm, d), lambda i: (i, 0)),
            pl.BlockSpec((tm, d), lambda i: (i, 0)),
            _resident((d, d)),
            pl.BlockSpec((1, d), lambda i: (0, 0)),
            pl.BlockSpec((2 * N_EXPERTS, d), lambda i: (0, 0)),
            pl.BlockSpec((N_EXPERTS, 1), lambda i: (0, 0)),
        ],
        out_specs=[
            pl.BlockSpec((tm, d), lambda i: (i, 0)),
            pl.BlockSpec((tm, PACK_W), lambda i: (i, 0)),
            pl.BlockSpec((TOP_K, tm), lambda i: (0, i)),
            pl.BlockSpec((TOP_K, tm), lambda i: (0, i)),
            pl.BlockSpec((TOP_K, tm), lambda i: (0, i)),
            pl.BlockSpec((N_EXPERTS, 128), lambda i: (0, 0)),
        ],
        out_shape=[
            jax.ShapeDtypeStruct((t, d), F32),
            jax.ShapeDtypeStruct((t, PACK_W), jnp.uint32),
            jax.ShapeDtypeStruct((TOP_K, t), jnp.int32),
            jax.ShapeDtypeStruct((TOP_K, t), F32),
            jax.ShapeDtypeStruct((TOP_K, t), jnp.int32),
            jax.ShapeDtypeStruct((N_EXPERTS, 128), jnp.int32),
        ],
        scratch_shapes=[pltpu.VMEM((N_EXPERTS, 128), F32)],
        compiler_params=_params(("arbitrary",)),
        name="route",
    )(mixed, x, w_o, g_ffn, w_router_t, b_router)


MOE_BM = 256
DISPATCH_TM = 256


def _dispatch_body(dest_ref, padrow_ref, npad_ref, nused_ref, xp_ref, xs_hbm, zblk, sem, zsem):
    tm = xp_ref.shape[0]
    bm = zblk.shape[0]
    nb = xs_hbm.shape[0] // bm

    @pl.when(pl.program_id(0) == 0)
    def _():
        zblk[...] = jnp.zeros_like(zblk)
        npad = npad_ref[0]
        nused = nused_ref[0]

        def zero_row(j):
            return pltpu.make_async_copy(zblk.at[pl.ds(0, 1), :], xs_hbm.at[pl.ds(padrow_ref[j], 1), :], zsem)

        def zero_block(b):
            return pltpu.make_async_copy(zblk, xs_hbm.at[pl.ds(pl.multiple_of(b * bm, bm), bm), :], zsem)

        def start_row(j, c):
            zero_row(j).start()
            return c

        def wait_row(j, c):
            zero_row(j).wait()
            return c

        def start_block(b, c):
            zero_block(b).start()
            return c

        def wait_block(b, c):
            zero_block(b).wait()
            return c

        lax.fori_loop(0, npad, start_row, 0)
        lax.fori_loop(nused, nb, start_block, 0)
        lax.fori_loop(0, npad, wait_row, 0)
        lax.fori_loop(nused, nb, wait_block, 0)

    def row_copy(r, k):
        return pltpu.make_async_copy(xp_ref.at[pl.ds(r, 1), :],
                                     xs_hbm.at[pl.ds(dest_ref[0, 0, k * tm + r], 1), :], sem)

    def start(r, c):
        for k in range(TOP_K):
            row_copy(r, k).start(priority=k % 2)
        return c

    lax.fori_loop(0, tm, start, 0)
    for k in range(TOP_K):
        pltpu.make_async_copy(xp_ref, xs_hbm.at[pl.ds(0, tm), :], sem).wait()


def _dispatch(xp, dest_tiles, pad_rows, n_pad, n_used, n_rows):
    t = xp.shape[0]
    tm = dest_tiles.shape[2] // TOP_K
    return pl.pallas_call(
        _dispatch_body,
        grid=(t // tm,),
        in_specs=[
            pl.BlockSpec((1, 1, TOP_K * tm), lambda i: (i, 0, 0), memory_space=pltpu.SMEM),
            pl.BlockSpec(memory_space=pltpu.SMEM),
            pl.BlockSpec(memory_space=pltpu.SMEM),
            pl.BlockSpec(memory_space=pltpu.SMEM),
            pl.BlockSpec((tm, PACK_W), lambda i: (i, 0)),
        ],
        out_specs=pl.BlockSpec(memory_space=pltpu.HBM),
        out_shape=jax.ShapeDtypeStruct((n_rows, PACK_W), jnp.uint32),
        scratch_shapes=[
            pltpu.VMEM((MOE_BM, PACK_W), jnp.uint32),
            pltpu.SemaphoreType.DMA(()),
            pltpu.SemaphoreType.DMA(()),
        ],
        compiler_params=_params(("arbitrary",)),
        name="dispatch",
    )(dest_tiles, pad_rows, n_pad, n_used, xp)


FFN_TF = 1024
FFN_TN = 1024
N_SLOTS = 2
IN_SLOTS = 3
STREAM_PRIORITY = 1


def _stream_expert_blocks(first_ref, nblk_ref, nused_ref, src_hbm, dst_hbm, ibuf, obuf, isem, osem,
                          prepare, compute):
    j, e = pl.program_id(0), pl.program_id(1)
    n_in = ibuf.shape[0]
    ahead = n_in - 1
    bm_in, bm_out, width = ibuf.shape[1], obuf.shape[1], obuf.shape[2]
    half = bm_in // 2
    col = pl.multiple_of(j * width, width)
    nblk = nblk_ref[e]
    first = first_ref[e]

    def fetch(b, slot):
        src = src_hbm.at[pl.ds(pl.multiple_of(b * bm_in, bm_in), bm_in), :]
        return pltpu.make_async_copy(src, ibuf.at[slot], isem.at[slot])

    def fetch_start(b, slot):
        for p in range(2):
            src = src_hbm.at[pl.ds(pl.multiple_of(b * bm_in + p * half, half), half), :]
            pltpu.make_async_copy(src, ibuf.at[slot, pl.ds(p * half, half), :], isem.at[slot]).start(priority=p)

    def flush(b, slot):
        dst = dst_hbm.at[pl.ds(pl.multiple_of(b * bm_out, bm_out), bm_out), pl.ds(col, width)]
        return pltpu.make_async_copy(obuf.at[slot], dst, osem.at[slot])

    for d in range(ahead):
        @pl.when(nblk > d)
        def _():
            fetch_start(first + d, d)

    @pl.when(nblk > 0)
    def _():
        prepare()

    def step(i, carry):
        slot = lax.rem(i, n_in)
        fetch(first + i, slot).wait()

        @pl.when(i + ahead < nblk)
        def _():
            fetch_start(first + i + ahead, lax.rem(i + ahead, n_in))

        out = compute(ibuf[slot])
        oslot = lax.rem(i, N_SLOTS)

        @pl.when(i >= N_SLOTS)
        def _():
            flush(first + i - N_SLOTS, oslot).wait()

        obuf[oslot] = out
        flush(first + i, oslot).start(priority=STREAM_PRIORITY)
        return carry

    lax.fori_loop(0, nblk, step, 0)

    @pl.when(nblk >= 2)
    def _():
        flush(first + nblk - 2, lax.rem(nblk, N_SLOTS)).wait()

    @pl.when(nblk >= 1)
    def _():
        flush(first + nblk - 1, lax.rem(nblk - 1, N_SLOTS)).wait()

    @pl.when(e == pl.num_programs(1) - 1)
    def _():
        obuf[0] = jnp.zeros(obuf.shape[1:], obuf.dtype)

        def zero(b, carry):
            cp = flush(b, 0)
            cp.start()
            cp.wait()
            return carry

        lax.fori_loop(nused_ref[0], dst_hbm.shape[0] // bm_out, zero, 0)


def _ffn_up_body(first_ref, nblk_ref, nused_ref, xs_hbm, wg_ref, wu_ref, bg_ref, bu_ref, h_hbm,
                 wg_bf, wu_bf, ibuf, obuf, isem, osem):
    def prepare():
        wg_bf[...] = wg_ref[...].astype(BF16)
        wu_bf[...] = wu_ref[...].astype(BF16)

    def compute(packed):
        x = _unpack_rows(packed)
        gate = jnp.dot(x, wg_bf[...], preferred_element_type=F32) + bg_ref[...]
        up = jnp.dot(x, wu_bf[...], preferred_element_type=F32) + bu_ref[...]
        gate = jnp.minimum(gate, SWIGLU_LIMIT)
        up = jnp.clip(up, -SWIGLU_LIMIT, SWIGLU_LIMIT)
        act = ((up + 1.0) * gate * _sigmoid(SWIGLU_ALPHA * gate)).astype(BF16)
        return pltpu.bitcast(act, jnp.uint32)

    _stream_expert_blocks(first_ref, nblk_ref, nused_ref, xs_hbm, h_hbm, ibuf, obuf, isem, osem,
                          prepare, compute)


def _ffn_up(first_blk, n_blk, n_used, xs, w_gate_up, b_gate_up):
    n_rows = xs.shape[0]
    tf = FFN_TF
    nf = D_FF // tf
    return pl.pallas_call(
        _ffn_up_body,
        grid_spec=pltpu.PrefetchScalarGridSpec(
            num_scalar_prefetch=3,
            grid=(nf, N_EXPERTS),
            in_specs=[
                pl.BlockSpec(memory_space=pltpu.HBM),
                pl.BlockSpec((None, D_MODEL, tf), lambda j, e, *_: (e, 0, j)),
                pl.BlockSpec((None, D_MODEL, tf), lambda j, e, *_: (e, 0, nf + j)),
                pl.BlockSpec((None, 1, tf), lambda j, e, *_: (e, 0, j)),
                pl.BlockSpec((None, 1, tf), lambda j, e, *_: (e, 0, nf + j)),
            ],
            out_specs=pl.BlockSpec(memory_space=pltpu.HBM),
            scratch_shapes=[
                pltpu.VMEM((D_MODEL, tf), BF16),
                pltpu.VMEM((D_MODEL, tf), BF16),
                pltpu.VMEM((IN_SLOTS, MOE_BM, PACK_W), jnp.uint32),
                pltpu.VMEM((N_SLOTS, MOE_BM // 2, tf), jnp.uint32),
                pltpu.SemaphoreType.DMA((IN_SLOTS,)),
                pltpu.SemaphoreType.DMA((N_SLOTS,)),
            ],
        ),
        out_shape=jax.ShapeDtypeStruct((n_rows // 2, D_FF), jnp.uint32),
        compiler_params=_params(("arbitrary", "arbitrary")),
        name="ffn_up",
    )(first_blk, n_blk, n_used, xs, w_gate_up, w_gate_up, b_gate_up, b_gate_up)


def _ffn_down_body(first_ref, nblk_ref, nused_ref, h_hbm, wd_ref, bd_ref, y_hbm, wd_bf, ibuf, obuf, isem, osem):
    def prepare():
        wd_bf[...] = wd_ref[...].astype(BF16)

    def compute(paired):
        hid = pltpu.bitcast(paired, BF16)
        return jnp.dot(hid, wd_bf[...], preferred_element_type=F32) + bd_ref[...]

    _stream_expert_blocks(first_ref, nblk_ref, nused_ref, h_hbm, y_hbm, ibuf, obuf, isem, osem,
                          prepare, compute)


def _ffn_down(first_blk, n_blk, n_used, h, w_down, b_down):
    n_rows = 2 * h.shape[0]
    tn = FFN_TN
    return pl.pallas_call(
        _ffn_down_body,
        grid_spec=pltpu.PrefetchScalarGridSpec(
            num_scalar_prefetch=3,
            grid=(D_MODEL // tn, N_EXPERTS),
            in_specs=[
                pl.BlockSpec(memory_space=pltpu.HBM),
                pl.BlockSpec((None, D_FF, tn), lambda j, e, *_: (e, 0, j)),
                pl.BlockSpec((None, 1, tn), lambda j, e, *_: (e, 0, j)),
            ],
            out_specs=pl.BlockSpec(memory_space=pltpu.HBM),
            scratch_shapes=[
                pltpu.VMEM((D_FF, tn), BF16),
                pltpu.VMEM((IN_SLOTS, MOE_BM // 2, D_FF), jnp.uint32),
                pltpu.VMEM((N_SLOTS, MOE_BM, tn), F32),
                pltpu.SemaphoreType.DMA((IN_SLOTS,)),
                pltpu.SemaphoreType.DMA((N_SLOTS,)),
            ],
        ),
        out_shape=jax.ShapeDtypeStruct((n_rows, D_MODEL), F32),
        compiler_params=_params(("arbitrary", "arbitrary")),
        name="ffn_down",
    )(first_blk, n_blk, n_used, h, w_down, b_down)


COMBINE_TM = 128


def _combine_body(dest_ref, next_ref, h_ref, w_ref, g_ref, y_hbm, o_ref, gbuf, sem):
    i = pl.program_id(0)
    tm = h_ref.shape[0]

    def gather_tile(table_ref, slot):
        def start(r, c):
            for k in range(TOP_K):
                pltpu.make_async_copy(y_hbm.at[pl.ds(table_ref[0, 0, k * tm + r], 1), :],
                                      gbuf.at[slot, k, pl.ds(r, 1), :], sem.at[slot]).start(priority=k % 2)
            return c

        lax.fori_loop(0, tm, start, 0)

    slot = lax.rem(i, N_SLOTS)

    @pl.when(i == 0)
    def _():
        gather_tile(dest_ref, 0)

    @pl.when(i + 1 < pl.num_programs(0))
    def _():
        gather_tile(next_ref, 1 - slot)

    for k in range(TOP_K):
        pltpu.make_async_copy(y_hbm.at[pl.ds(0, tm), :], gbuf.at[slot, k], sem.at[slot]).wait()
    h = h_ref[...]
    for k in range(TOP_K):
        h = h + w_ref[:, k:k + 1] * gbuf[slot, k]
    o_ref[...] = h * lax.rsqrt(jnp.mean(h * h, axis=-1, keepdims=True) + EPS) * g_ref[...]


def _combine(dest_tiles, h1, w_cols, g_final, y):
    t = h1.shape[0]
    tm = dest_tiles.shape[2] // TOP_K
    d = D_MODEL
    last = t // tm - 1
    return pl.pallas_call(
        _combine_body,
        grid=(t // tm,),
        in_specs=[
            pl.BlockSpec((1, 1, TOP_K * tm), lambda i: (i, 0, 0), memory_space=pltpu.SMEM),
            pl.BlockSpec((1, 1, TOP_K * tm), lambda i: (jnp.minimum(i + 1, last), 0, 0), memory_space=pltpu.SMEM),
            pl.BlockSpec((tm, d), lambda i: (i, 0)),
            pl.BlockSpec((tm, TOP_K), lambda i: (i, 0)),
            pl.BlockSpec((1, d), lambda i: (0, 0)),
            pl.BlockSpec(memory_space=pltpu.HBM),
        ],
        out_specs=pl.BlockSpec((tm, d), lambda i: (i, 0)),
        out_shape=jax.ShapeDtypeStruct((t, d), F32),
        scratch_shapes=[pltpu.VMEM((N_SLOTS, TOP_K, tm, d), F32), pltpu.SemaphoreType.DMA((N_SLOTS,))],
        compiler_params=_params(("arbitrary",)),
        name="combine",
    )(dest_tiles, dest_tiles, h1, w_cols, g_final, y)


def _tile_major(a, tm):
    k, t = a.shape
    return a.reshape(k, t // tm, tm).transpose(1, 0, 2).reshape(t // tm, 1, k * tm)


def _routing_tables(top_e, rank, counts, t):
    bm = MOE_BM
    nb = (t * TOP_K) // bm + N_EXPERTS
    padded = (counts + bm - 1) // bm * bm
    pad_end = jnp.cumsum(padded)
    pad_start = pad_end - padded
    onehot = top_e[:, :, None] == jnp.arange(N_EXPERTS, dtype=jnp.int32)
    dest = rank + jnp.sum(jnp.where(onehot, pad_start, 0), axis=-1)
    first_blk = (pad_start // bm).astype(jnp.int32)
    n_blk = (padded // bm).astype(jnp.int32)
    n_used = (pad_end[-1] // bm).astype(jnp.int32).reshape(1)
    gap = padded - counts
    gap_end = jnp.cumsum(gap)
    j = jnp.arange(N_EXPERTS * bm, dtype=jnp.int32)
    ej = jnp.minimum(jnp.sum(j[:, None] >= gap_end[None, :], axis=1), N_EXPERTS - 1)
    pad_rows = (pad_start + counts)[ej] + j - (gap_end - gap)[ej]
    pad_rows = jnp.clip(pad_rows, 0, nb * bm - 1).astype(jnp.int32)
    n_pad = gap_end[-1].astype(jnp.int32).reshape(1)
    return dest.astype(jnp.int32), first_blk, n_blk, n_used, pad_rows, n_pad, nb * bm


def kernel(x, g_mix, w_in, ssm_conv_w, ssm_conv_b, ssm_dt_bias, ssm_a_log, ssm_d, ssm_norm_g, w_ssm_out,
           sc_conv_w, w_sc_out, b_gate, w_o, g_ffn, w_router, b_router, w_gate_up, b_gate_up, w_down,
           b_down, g_final):
    bsz, seq, d = x.shape
    t = bsz * seq
    assert bsz == 1 and d == D_MODEL and w_in.shape[0] == 1
    xt = x.reshape(t, d)
    w_dt = w_in[0, :, OFF_DT:OFF_SC]
    w_dt_hi = w_dt.astype(BF16)
    w_dt = jnp.concatenate([w_dt_hi, (w_dt - w_dt_hi.astype(F32)).astype(BF16)], axis=1)
    col = lambda a: a.reshape(-1, 1)
    row = lambda a: a.reshape(1, -1)

    u, dt_raw = _prenorm(xt, row(g_mix[0]), w_dt)
    proj = _inproj(u, w_in[0].T)
    y_norm = _ssd(proj, dt_raw.T, ssm_conv_w[0], row(ssm_conv_b[0]), col(ssm_dt_bias[0]), col(ssm_a_log[0]),
                  col(ssm_d[0]), row(ssm_norm_g[0]))
    mixed = _mix(y_norm, proj, row(b_gate[0]), sc_conv_w[0], w_ssm_out[0].astype(BF16), w_sc_out[0].astype(BF16))
    wr_t = w_router[0].T
    wr_hi = wr_t.astype(BF16)
    wr_lo = (wr_t - wr_hi.astype(F32)).astype(BF16)
    h1, xp, top_e, top_w, rank, counts = _route(mixed, xt, w_o[0].astype(BF16), row(g_ffn[0]),
                                                jnp.concatenate([wr_hi, wr_lo], axis=0), col(b_router[0]))
    dest, first_blk, n_blk, n_used, pad_rows, n_pad, n_rows = _routing_tables(top_e, rank, counts[:, 0], t)
    xs = _dispatch(xp, _tile_major(dest, min(DISPATCH_TM, t)), pad_rows, n_pad, n_used, n_rows)
    hid = _ffn_up(first_blk, n_blk, n_used, xs, w_gate_up[0], b_gate_up[0].reshape(N_EXPERTS, 1, 2 * D_FF))
    y = _ffn_down(first_blk, n_blk, n_used, hid, w_down[0], b_down[0].reshape(N_EXPERTS, 1, D_MODEL))
    out = _combine(_tile_major(dest, min(COMBINE_TM, t)), h1, top_w.T, row(g_final), y)
    return out.reshape(bsz, seq, d)
```

```python
import functools

import jax
import jax.numpy as jnp
from jax import lax
from jax.experimental import pallas as pl
from jax.experimental.pallas import tpu as pltpu

D_MODEL = 2048
SSM_D_INNER = 2 * D_MODEL
SSM_HEAD_DIM = 64
SSM_N_HEADS = SSM_D_INNER // SSM_HEAD_DIM
SSM_N_GROUPS = 8
SSM_HEADS_PER_GROUP = SSM_N_HEADS // SSM_N_GROUPS
SSM_D_STATE = 128
SSM_CONV = 4
SSM_GN = SSM_N_GROUPS * SSM_D_STATE
SSM_CONV_DIM = SSM_D_INNER + 2 * SSM_GN
SSM_GROUP_CH = SSM_D_INNER // SSM_N_GROUPS
SC_DIM = D_MODEL
SC_WIDTH = 3
N_EXPERTS = 32
TOP_K = 4
D_FF = D_MODEL
SWIGLU_LIMIT = 7.0
SWIGLU_ALPHA = 1.702
EPS = 1e-5

OFF_Z = 0
OFF_XBC = OFF_Z + SSM_D_INNER
OFF_DT = OFF_XBC + SSM_CONV_DIM
OFF_SC = OFF_DT + SSM_N_HEADS
OFF_GATE = OFF_SC + 3 * SC_DIM
D_IN_PROJ = OFF_GATE + 2 * D_MODEL

P_Z = 0
P_XBC = P_Z + SSM_D_INNER
P_SC = P_XBC + SSM_CONV_DIM
P_GATE = P_SC + 3 * SC_DIM
P_TOTAL = P_GATE + 2 * D_MODEL

SUBLANES = 8
VMEM_LIMIT = 56 * 1024 * 1024

F32 = jnp.float32
BF16 = jnp.bfloat16
HIGHEST = lax.Precision.HIGHEST
NT_DIMS = (((1,), (1,)), ((), ()))
TN_DIMS = (((0,), (0,)), ((), ()))


def _sigmoid(v):
    return 1.0 / (1.0 + jnp.exp(-v))


def _params(semantics):
    return pltpu.CompilerParams(dimension_semantics=semantics, vmem_limit_bytes=VMEM_LIMIT)


PRENORM_TM = 512
INPROJ_TM = 1024
INPROJ_TN = 1024
LANES = 128
DT_SHIFT = OFF_SC - OFF_DT
ALIGNED_TILES = OFF_DT // INPROJ_TN
CAST_ROWS = 256


def _prenorm_body(x_ref, g_ref, wdt_ref, u_ref, dt_ref):
    x = x_ref[...]
    u = x * lax.rsqrt(jnp.mean(x * x, axis=-1, keepdims=True) + EPS) * g_ref[...]
    u_hi = u.astype(BF16)
    u_ref[...] = u_hi
    u_lo = (u - u_hi.astype(F32)).astype(BF16)
    both = jnp.dot(u_hi, wdt_ref[...], preferred_element_type=F32)
    cross = jnp.dot(u_lo, wdt_ref[...], preferred_element_type=F32)
    nh = SSM_N_HEADS
    dt_ref[...] = both[:, :nh] + both[:, nh:] + cross[:, :nh]


def _prenorm(x, g, w_dt):
    t = x.shape[0]
    tm = min(PRENORM_TM, t)
    return pl.pallas_call(
        _prenorm_body,
        grid=(t // tm,),
        in_specs=[
            pl.BlockSpec((tm, D_MODEL), lambda i: (i, 0)),
            pl.BlockSpec((1, D_MODEL), lambda i: (0, 0)),
            pl.BlockSpec((D_MODEL, 2 * SSM_N_HEADS), lambda i: (0, 0)),
        ],
        out_specs=[
            pl.BlockSpec((tm, D_MODEL), lambda i: (i, 0)),
            pl.BlockSpec((tm, SSM_N_HEADS), lambda i: (i, 0)),
        ],
        out_shape=[
            jax.ShapeDtypeStruct((t, D_MODEL), BF16),
            jax.ShapeDtypeStruct((t, SSM_N_HEADS), F32),
        ],
        compiler_params=_params(("parallel",)),
        name="prenorm",
    )(x, g, w_dt)


def _inproj_body(u_ref, wt_ref, cw_ref, cb_ref, proj_ref, w_bf, cbuf):
    j, i = pl.program_id(0), pl.program_id(1)
    tm = u_ref.shape[0]
    tail = SUBLANES
    z_tiles = P_XBC // w_bf.shape[0]
    conv_tiles = P_SC // w_bf.shape[0]

    @pl.when(i == 0)
    def _():
        for r in range(0, w_bf.shape[0], CAST_ROWS):
            w_bf[r:r + CAST_ROWS, :] = wt_ref[r:r + CAST_ROWS, :].astype(BF16)
        cbuf[0:tail, :] = jnp.zeros((tail, cbuf.shape[1]), F32)

    def project():
        return lax.dot_general(u_ref[...], w_bf[...], NT_DIMS, preferred_element_type=F32)

    @pl.when(j < z_tiles)
    def _():
        acc = project()
        proj_ref[...] = (acc * _sigmoid(acc)).astype(BF16)

    @pl.when(jnp.logical_and(j >= z_tiles, j < conv_tiles))
    def _():
        cbuf[tail:tail + tm, :] = project()
        acc = cb_ref[...] + cw_ref[SSM_CONV - 1:SSM_CONV, :] * cbuf[tail:tail + tm, :]
        for k in range(SSM_CONV - 1):
            off = tail - (SSM_CONV - 1) + k
            acc = acc + cw_ref[k:k + 1, :] * cbuf[off:off + tm, :]
        cbuf[0:tail, :] = cbuf[tm:tm + tail, :]
        proj_ref[...] = (acc * _sigmoid(acc)).astype(BF16)

    @pl.when(j >= conv_tiles)
    def _():
        proj_ref[...] = project().astype(BF16)


def _inproj(u, w_in_t, conv_w, conv_b):
    t = u.shape[0]
    tm = min(INPROJ_TM, t)
    tn = INPROJ_TN
    assert OFF_DT % tn == 0 and DT_SHIFT % SUBLANES == 0 and P_XBC % tn == 0 and P_SC % tn == 0
    conv_col = lambda j: jnp.clip(j - P_XBC // tn, 0, SSM_CONV_DIM // tn - 1)
    first_row = lambda j: pl.multiple_of(j * tn + jnp.where(j >= ALIGNED_TILES, DT_SHIFT, 0), SUBLANES)
    return pl.pallas_call(
        _inproj_body,
        grid=(P_TOTAL // tn, t // tm),
        in_specs=[
            pl.BlockSpec((tm, D_MODEL), lambda j, i: (i, 0)),
            pl.BlockSpec((pl.Element(tn), pl.Element(D_MODEL)), lambda j, i: (first_row(j), 0)),
            pl.BlockSpec((SSM_CONV, tn), lambda j, i: (0, conv_col(j))),
            pl.BlockSpec((1, tn), lambda j, i: (0, conv_col(j))),
        ],
        out_specs=pl.BlockSpec((tm, tn), lambda j, i: (i, j)),
        out_shape=jax.ShapeDtypeStruct((t, P_TOTAL), BF16),
        scratch_shapes=[pltpu.VMEM((tn, D_MODEL), BF16), pltpu.VMEM((tm + SUBLANES, tn), F32)],
        compiler_params=_params(("arbitrary", "arbitrary")),
        name="inproj",
    )(u, w_in_t, conv_w, conv_b)


SSD_L = 256
HEAD_PAIR = 2 * SSM_HEAD_DIM


def _ssd_body(z_ref, x_ref, b_ref, c_ref, dt_ref, dtb_ref, alog_ref, d_ref, ng_ref, o_ref, s_ref):
    L = x_ref.shape[0]

    @pl.when(pl.program_id(1) == 0)
    def _():
        s_ref[...] = jnp.zeros_like(s_ref)

    xs = x_ref[...].astype(F32)
    bm = b_ref[...]
    cm = c_ref[...]

    dt_raw = dt_ref[...] + dtb_ref[...]
    dt = jnp.maximum(dt_raw, 0.0) + jnp.log(1.0 + jnp.exp(-jnp.abs(dt_raw)))
    da = dt * (-jnp.exp(alog_ref[...]))
    row = lax.broadcasted_iota(jnp.int32, (L, L), 0)
    col = lax.broadcasted_iota(jnp.int32, (L, L), 1)
    causal = row >= col
    incl = (row <= col).astype(BF16)
    da_hi = da.astype(BF16)
    rem = da - da_hi.astype(F32)
    da_mid = rem.astype(BF16)
    da_lo = (rem - da_mid.astype(F32)).astype(BF16)
    parts = jnp.dot(jnp.concatenate([da_hi, da_mid, da_lo], axis=0), incl, preferred_element_type=F32)
    hg = SSM_HEADS_PER_GROUP
    cs = parts[0:hg] + parts[hg:2 * hg] + parts[2 * hg:3 * hg]
    cs_end = cs[:, L - 1:L]
    flipped = jnp.concatenate([cs, dt, jnp.exp(cs_end - cs)], axis=0).T
    cs_t, dt_t, to_end_t = flipped[:, 0:hg], flipped[:, hg:2 * hg], flipped[:, 2 * hg:3 * hg]
    ecs_t = jnp.exp(cs_t)

    cb = lax.dot_general(cm, bm, NT_DIMS, preferred_element_type=F32)
    cb = jnp.where(causal, cb, 0.0)
    y_off = lax.dot_general(cm, s_ref[...].astype(BF16), NT_DIMS, preferred_element_type=F32)

    lane = lax.broadcasted_iota(jnp.int32, (L, HEAD_PAIR), 1)
    first = lane < SSM_HEAD_DIM
    ys = []
    xws = []
    for p in range(SSM_HEADS_PER_GROUP // 2):
        h0, h1 = 2 * p, 2 * p + 1
        sl = slice(p * HEAD_PAIR, (p + 1) * HEAD_PAIR)
        xp = xs[:, sl]
        pick = lambda a: jnp.where(first, a[:, h0:h0 + 1], a[:, h1:h1 + 1])
        xdt = xp * pick(dt_t)
        xdt_b = xdt.astype(BF16)
        yd = []
        for h in (h0, h1):
            seg = cs_t[:, h:h + 1] - cs[h:h + 1, :]
            m = cb * jnp.exp(jnp.minimum(seg, 0.0))
            yd.append(jnp.dot(m.astype(BF16), xdt_b, preferred_element_type=F32))
        y = jnp.where(first, yd[0], yd[1]) + y_off[:, sl] * pick(ecs_t)
        y = y + xp * jnp.where(first[0:1, :], d_ref[h0:h0 + 1, :], d_ref[h1:h1 + 1, :])
        ys.append(y)
        xws.append((xdt * pick(to_end_t)).astype(BF16))
    y = jnp.concatenate(ys, axis=1)
    xw = jnp.concatenate(xws, axis=1)

    upd = lax.dot_general(xw, bm, TN_DIMS, preferred_element_type=F32)
    chunk_decay = jnp.exp(cs_end)
    for h in range(SSM_HEADS_PER_GROUP):
        rows = slice(h * SSM_HEAD_DIM, (h + 1) * SSM_HEAD_DIM)
        s_ref[rows, :] = s_ref[rows, :] * chunk_decay[h:h + 1, :] + upd[rows, :]

    v = y * z_ref[...].astype(F32)
    v = v * lax.rsqrt(jnp.mean(v * v, axis=-1, keepdims=True) + EPS)
    o_ref[...] = (v * ng_ref[...]).astype(BF16)


def _ssd(proj, dt_rows, dt_bias, a_log, d_skip, norm_g):
    t = proj.shape[0]
    L = min(SSD_L, t)
    gc = SSM_GROUP_CH
    n = SSM_D_STATE
    zc, xc = P_Z // gc, P_XBC // gc
    bc, cc = (P_XBC + SSM_D_INNER) // n, (P_XBC + SSM_D_INNER + SSM_GN) // n
    hg = SSM_HEADS_PER_GROUP
    return pl.pallas_call(
        _ssd_body,
        grid=(SSM_N_GROUPS, t // L),
        in_specs=[
            pl.BlockSpec((L, gc), lambda g, i: (i, zc + g)),
            pl.BlockSpec((L, gc), lambda g, i: (i, xc + g)),
            pl.BlockSpec((L, n), lambda g, i: (i, bc + g)),
            pl.BlockSpec((L, n), lambda g, i: (i, cc + g)),
            pl.BlockSpec((hg, L), lambda g, i: (g, i)),
            pl.BlockSpec((hg, 1), lambda g, i: (g, 0)),
            pl.BlockSpec((hg, 1), lambda g, i: (g, 0)),
            pl.BlockSpec((hg, 1), lambda g, i: (g, 0)),
            pl.BlockSpec((1, gc), lambda g, i: (0, g)),
        ],
        out_specs=pl.BlockSpec((L, gc), lambda g, i: (i, g)),
        out_shape=jax.ShapeDtypeStruct((t, SSM_D_INNER), BF16),
        scratch_shapes=[pltpu.VMEM((gc, n), F32)],
        compiler_params=_params(("parallel", "arbitrary")),
        name="ssd",
    )(proj, proj, proj, proj, dt_rows, dt_bias, a_log, d_skip, norm_g)


MIX_TM = 256


def _mix_body(yn_ref, b_ref, c_ref, v_ref, cp_ref, vp_ref, g1_ref, g2_ref, bg1_ref, bg2_ref,
              wc_ref, wssm_ref, wsc_ref, o_ref, buf):
    tm = yn_ref.shape[0]
    tail = SUBLANES
    prev = cp_ref[...].astype(F32) * vp_ref[...].astype(F32)
    buf[0:tail, :] = jnp.where(pl.program_id(0) == 0, 0.0, prev)
    cv = c_ref[...].astype(F32) * v_ref[...].astype(F32)
    buf[tail:tail + tm, :] = cv
    conv = wc_ref[SC_WIDTH - 1:SC_WIDTH, :] * cv
    for k in range(SC_WIDTH - 1):
        off = tail - (SC_WIDTH - 1) + k
        conv = conv + wc_ref[k:k + 1, :] * buf[off:off + tm, :]
    sc_in = (b_ref[...].astype(F32) * conv).astype(BF16)
    y_sc = jnp.dot(sc_in, wsc_ref[...], preferred_element_type=F32)
    y_ssm = jnp.dot(yn_ref[...], wssm_ref[...], preferred_element_type=F32)
    g1 = _sigmoid(g1_ref[...].astype(F32) + bg1_ref[...])
    g2 = _sigmoid(g2_ref[...].astype(F32) + bg2_ref[...])
    o_ref[...] = (g1 * y_ssm + g2 * y_sc).astype(BF16)


def _resident(shape):
    return pl.BlockSpec(shape, lambda *_: (0,) * len(shape), pipeline_mode=pl.Buffered(1))


def _mix(y_norm, proj, b_gate, sc_conv_w, w_ssm_out, w_sc_out):
    t = y_norm.shape[0]
    tm = min(MIX_TM, t)
    d = D_MODEL
    sb, gb = P_SC // d, P_GATE // d
    prev_rows = lambda i: jnp.maximum(i * (tm // SUBLANES) - 1, 0)
    return pl.pallas_call(
        _mix_body,
        grid=(t // tm,),
        in_specs=[
            pl.BlockSpec((tm, SSM_D_INNER), lambda i: (i, 0)),
            pl.BlockSpec((tm, d), lambda i: (i, sb)),
            pl.BlockSpec((tm, d), lambda i: (i, sb + 1)),
            pl.BlockSpec((tm, d), lambda i: (i, sb + 2)),
            pl.BlockSpec((SUBLANES, d), lambda i: (prev_rows(i), sb + 1)),
            pl.BlockSpec((SUBLANES, d), lambda i: (prev_rows(i), sb + 2)),
            pl.BlockSpec((tm, d), lambda i: (i, gb)),
            pl.BlockSpec((tm, d), lambda i: (i, gb + 1)),
            pl.BlockSpec((1, d), lambda i: (0, 0)),
            pl.BlockSpec((1, d), lambda i: (0, 1)),
            pl.BlockSpec((SC_WIDTH, d), lambda i: (0, 0)),
            _resident((SSM_D_INNER, d)),
            _resident((d, d)),
        ],
        out_specs=pl.BlockSpec((tm, d), lambda i: (i, 0)),
        out_shape=jax.ShapeDtypeStruct((t, d), BF16),
        scratch_shapes=[pltpu.VMEM((tm + SUBLANES, d), F32)],
        compiler_params=_params(("parallel",)),
        name="mix",
    )(y_norm, proj, proj, proj, proj, proj, proj, proj, b_gate, b_gate, sc_conv_w, w_ssm_out, w_sc_out)


ROUTE_TM = 512
PACK_W = D_MODEL // 2


def _pack_rows(v):
    lo = lax.bitcast_convert_type(v[:, :PACK_W].astype(F32), jnp.uint32)
    hi = lax.bitcast_convert_type(v[:, PACK_W:].astype(F32), jnp.uint32)
    return hi | (lo >> 16)


def _unpack_rows(w):
    lo = lax.bitcast_convert_type(w << 16, F32).astype(BF16)
    hi = lax.bitcast_convert_type(w & jnp.uint32(0xFFFF0000), F32).astype(BF16)
    return jnp.concatenate([lo, hi], axis=1)


def _route_body(m_ref, x_ref, wo_ref, g_ref, wr_ref, br_ref,
                h_ref, xp_ref, e_ref, w_ref, r_ref, cnt_ref, carry):
    tm = m_ref.shape[0]

    @pl.when(pl.program_id(0) == 0)
    def _():
        carry[...] = jnp.zeros_like(carry)

    h = x_ref[...] + jnp.dot(m_ref[...], wo_ref[...], preferred_element_type=F32)
    h_ref[...] = h
    xn = h * lax.rsqrt(jnp.mean(h * h, axis=-1, keepdims=True) + EPS) * g_ref[...]
    xn_hi = xn.astype(BF16)
    xn_lo = (xn - xn_hi.astype(F32)).astype(BF16)
    xp_ref[...] = _pack_rows(xn_hi)

    ne = N_EXPERTS
    both = lax.dot_general(wr_ref[...], xn_hi, NT_DIMS, preferred_element_type=F32)
    cross = lax.dot_general(wr_ref[0:ne, :], xn_lo, NT_DIMS, preferred_element_type=F32)
    logits = both[0:ne] + both[ne:2 * ne] + cross + br_ref[...]
    eidx = lax.broadcasted_iota(jnp.int32, (N_EXPERTS, tm), 0)
    vals, hots = [], []
    for k in range(TOP_K):
        best = jnp.max(logits, axis=0, keepdims=True)
        arg = jnp.min(jnp.where(logits == best, eidx, N_EXPERTS), axis=0, keepdims=True)
        hot = eidx == arg
        e_ref[k:k + 1, :] = arg
        vals.append(best)
        hots.append(hot)
        logits = jnp.where(hot, -jnp.inf, logits)
    exps = [jnp.exp(v - vals[0]) for v in vals]
    denom = exps[0] + exps[1] + exps[2] + exps[3]
    for k in range(TOP_K):
        w_ref[k:k + 1, :] = exps[k] / denom

    cnt = (hots[0] | hots[1] | hots[2] | hots[3]).astype(F32)
    r_i = lax.broadcasted_iota(jnp.int32, (tm, tm), 0)
    c_i = lax.broadcasted_iota(jnp.int32, (tm, tm), 1)
    before = (r_i < c_i).astype(BF16)
    prior = carry[:, 0:1] + jnp.dot(cnt.astype(BF16), before, preferred_element_type=F32)
    for k in range(TOP_K):
        r_ref[k:k + 1, :] = jnp.sum(jnp.where(hots[k], prior, 0.0), axis=0, keepdims=True).astype(jnp.int32)
    total = carry[...] + jnp.sum(cnt, axis=1, keepdims=True)
    carry[...] = total
    cnt_ref[...] = total.astype(jnp.int32)


def _route(mixed, x, w_o, g_ffn, w_router_t, b_router):
    t = x.shape[0]
    tm = min(ROUTE_TM, t)
    d = D_MODEL
    return pl.pallas_call(
        _route_body,
        grid=(t // tm,),
        in_specs=[
            pl.BlockSpec((tm, d), lambda i: (i, 0)),
            pl.BlockSpec((tm, d), lambda i: (i, 0)),
            _resident((d, d)),
            pl.BlockSpec((1, d), lambda i: (0, 0)),
            pl.BlockSpec((2 * N_EXPERTS, d), lambda i: (0, 0)),
            pl.BlockSpec((N_EXPERTS, 1), lambda i: (0, 0)),
        ],
        out_specs=[
            pl.BlockSpec((tm, d), lambda i: (i, 0)),
            pl.BlockSpec((tm, PACK_W), lambda i: (i, 0)),
            pl.BlockSpec((TOP_K, tm), lambda i: (0, i)),
            pl.BlockSpec((TOP_K, tm), lambda i: (0, i)),
            pl.BlockSpec((TOP_K, tm), lambda i: (0, i)),
            pl.BlockSpec((N_EXPERTS, 128), lambda i: (0, 0)),
        ],
        out_shape=[
            jax.ShapeDtypeStruct((t, d), F32),
            jax.ShapeDtypeStruct((t, PACK_W), jnp.uint32),
            jax.ShapeDtypeStruct((TOP_K, t), jnp.int32),
            jax.ShapeDtypeStruct((TOP_K, t), F32),
            jax.ShapeDtypeStruct((TOP_K, t), jnp.int32),
            jax.ShapeDtypeStruct((N_EXPERTS, 128), jnp.int32),
        ],
        scratch_shapes=[pltpu.VMEM((N_EXPERTS, 128), F32)],
        compiler_params=_params(("arbitrary",)),
        name="route",
    )(mixed, x, w_o, g_ffn, w_router_t, b_router)


MOE_BM = 256
DISPATCH_TM = 256


def _dispatch_body(dest_ref, padrow_ref, npad_ref, nused_ref, xp_ref, xs_hbm, zblk, sem, zsem):
    tm = xp_ref.shape[0]
    bm = zblk.shape[0]
    nb = xs_hbm.shape[0] // bm

    @pl.when(pl.program_id(0) == 0)
    def _():
        zblk[...] = jnp.zeros_like(zblk)
        npad = npad_ref[0]
        nused = nused_ref[0]

        def zero_row(j):
            return pltpu.make_async_copy(zblk.at[pl.ds(0, 1), :], xs_hbm.at[pl.ds(padrow_ref[j], 1), :], zsem)

        def zero_block(b):
            return pltpu.make_async_copy(zblk, xs_hbm.at[pl.ds(pl.multiple_of(b * bm, bm), bm), :], zsem)

        def start_row(j, c):
            zero_row(j).start()
            return c

        def wait_row(j, c):
            zero_row(j).wait()
            return c

        def start_block(b, c):
            zero_block(b).start()
            return c

        def wait_block(b, c):
            zero_block(b).wait()
            return c

        lax.fori_loop(0, npad, start_row, 0)
        lax.fori_loop(nused, nb, start_block, 0)
        lax.fori_loop(0, npad, wait_row, 0)
        lax.fori_loop(nused, nb, wait_block, 0)

    def row_copy(r, k):
        return pltpu.make_async_copy(xp_ref.at[pl.ds(r, 1), :],
                                     xs_hbm.at[pl.ds(dest_ref[0, 0, k * tm + r], 1), :], sem)

    def start(r, c):
        for k in range(TOP_K):
            row_copy(r, k).start(priority=k % 2)
        return c

    lax.fori_loop(0, tm, start, 0)
    for k in range(TOP_K):
        pltpu.make_async_copy(xp_ref, xs_hbm.at[pl.ds(0, tm), :], sem).wait()


def _dispatch(xp, dest_tiles, pad_rows, n_pad, n_used, n_rows):
    t = xp.shape[0]
    tm = dest_tiles.shape[2] // TOP_K
    return pl.pallas_call(
        _dispatch_body,
        grid=(t // tm,),
        in_specs=[
            pl.BlockSpec((1, 1, TOP_K * tm), lambda i: (i, 0, 0), memory_space=pltpu.SMEM),
            pl.BlockSpec(memory_space=pltpu.SMEM),
            pl.BlockSpec(memory_space=pltpu.SMEM),
            pl.BlockSpec(memory_space=pltpu.SMEM),
            pl.BlockSpec((tm, PACK_W), lambda i: (i, 0)),
        ],
        out_specs=pl.BlockSpec(memory_space=pltpu.HBM),
        out_shape=jax.ShapeDtypeStruct((n_rows, PACK_W), jnp.uint32),
        scratch_shapes=[
            pltpu.VMEM((MOE_BM, PACK_W), jnp.uint32),
            pltpu.SemaphoreType.DMA(()),
            pltpu.SemaphoreType.DMA(()),
        ],
        compiler_params=_params(("arbitrary",)),
        name="dispatch",
    )(dest_tiles, pad_rows, n_pad, n_used, xp)


FFN_TF = 1024
FFN_TN = 2048
N_SLOTS = 2
IN_SLOTS = 3
STREAM_PRIORITY = 1


def _stream_expert_blocks(first_ref, nblk_ref, nused_ref, src_hbm, dst_hbm, ibuf, obuf, isem, osem,
                          prepare, compute):
    j, e = pl.program_id(0), pl.program_id(1)
    n_in = ibuf.shape[0]
    ahead = n_in - 1
    bm_in, bm_out, width = ibuf.shape[1], obuf.shape[1], obuf.shape[2]
    half = bm_in // 2
    col = pl.multiple_of(j * width, width)
    nblk = nblk_ref[e]
    first = first_ref[e]

    def fetch(b, slot):
        src = src_hbm.at[pl.ds(pl.multiple_of(b * bm_in, bm_in), bm_in), :]
        return pltpu.make_async_copy(src, ibuf.at[slot], isem.at[slot])

    def fetch_start(b, slot):
        for p in range(2):
            src = src_hbm.at[pl.ds(pl.multiple_of(b * bm_in + p * half, half), half), :]
            pltpu.make_async_copy(src, ibuf.at[slot, pl.ds(p * half, half), :], isem.at[slot]).start(priority=p)

    def flush(b, slot):
        dst = dst_hbm.at[pl.ds(pl.multiple_of(b * bm_out, bm_out), bm_out), pl.ds(col, width)]
        return pltpu.make_async_copy(obuf.at[slot], dst, osem.at[slot])

    for d in range(ahead):
        @pl.when(nblk > d)
        def _():
            fetch_start(first + d, d)

    @pl.when(nblk > 0)
    def _():
        prepare()

    def step(i, carry):
        slot = lax.rem(i, n_in)
        fetch(first + i, slot).wait()

        @pl.when(i + ahead < nblk)
        def _():
            fetch_start(first + i + ahead, lax.rem(i + ahead, n_in))

        out = compute(ibuf[slot])
        oslot = lax.rem(i, N_SLOTS)

        @pl.when(i >= N_SLOTS)
        def _():
            flush(first + i - N_SLOTS, oslot).wait()

        obuf[oslot] = out
        flush(first + i, oslot).start(priority=STREAM_PRIORITY)
        return carry

    lax.fori_loop(0, nblk, step, 0)

    @pl.when(nblk >= 2)
    def _():
        flush(first + nblk - 2, lax.rem(nblk, N_SLOTS)).wait()

    @pl.when(nblk >= 1)
    def _():
        flush(first + nblk - 1, lax.rem(nblk - 1, N_SLOTS)).wait()

    @pl.when(e == pl.num_programs(1) - 1)
    def _():
        obuf[0] = jnp.zeros(obuf.shape[1:], obuf.dtype)

        def zero(b, carry):
            cp = flush(b, 0)
            cp.start()
            cp.wait()
            return carry

        lax.fori_loop(nused_ref[0], dst_hbm.shape[0] // bm_out, zero, 0)


def _ffn_up_body(first_ref, nblk_ref, nused_ref, xs_hbm, wg_ref, wu_ref, bg_ref, bu_ref, h_hbm,
                 wg_bf, wu_bf, ibuf, obuf, isem, osem):
    def prepare():
        wg_bf[...] = wg_ref[...].astype(BF16)
        wu_bf[...] = wu_ref[...].astype(BF16)

    def compute(packed):
        x = _unpack_rows(packed)
        gate = jnp.dot(x, wg_bf[...], preferred_element_type=F32) + bg_ref[...]
        up = jnp.dot(x, wu_bf[...], preferred_element_type=F32) + bu_ref[...]
        gate = jnp.minimum(gate, SWIGLU_LIMIT)
        up = jnp.clip(up, -SWIGLU_LIMIT, SWIGLU_LIMIT)
        act = ((up + 1.0) * gate * _sigmoid(SWIGLU_ALPHA * gate)).astype(BF16)
        return pltpu.bitcast(act, jnp.uint32)

    _stream_expert_blocks(first_ref, nblk_ref, nused_ref, xs_hbm, h_hbm, ibuf, obuf, isem, osem,
                          prepare, compute)


def _ffn_up(first_blk, n_blk, n_used, xs, w_gate_up, b_gate_up):
    n_rows = xs.shape[0]
    tf = FFN_TF
    nf = D_FF // tf
    return pl.pallas_call(
        _ffn_up_body,
        grid_spec=pltpu.PrefetchScalarGridSpec(
            num_scalar_prefetch=3,
            grid=(nf, N_EXPERTS),
            in_specs=[
                pl.BlockSpec(memory_space=pltpu.HBM),
                pl.BlockSpec((None, D_MODEL, tf), lambda j, e, *_: (e, 0, j)),
                pl.BlockSpec((None, D_MODEL, tf), lambda j, e, *_: (e, 0, nf + j)),
                pl.BlockSpec((None, 1, tf), lambda j, e, *_: (e, 0, j)),
                pl.BlockSpec((None, 1, tf), lambda j, e, *_: (e, 0, nf + j)),
            ],
            out_specs=pl.BlockSpec(memory_space=pltpu.HBM),
            scratch_shapes=[
                pltpu.VMEM((D_MODEL, tf), BF16),
                pltpu.VMEM((D_MODEL, tf), BF16),
                pltpu.VMEM((IN_SLOTS, MOE_BM, PACK_W), jnp.uint32),
                pltpu.VMEM((N_SLOTS, MOE_BM // 2, tf), jnp.uint32),
                pltpu.SemaphoreType.DMA((IN_SLOTS,)),
                pltpu.SemaphoreType.DMA((N_SLOTS,)),
            ],
        ),
        out_shape=jax.ShapeDtypeStruct((n_rows // 2, D_FF), jnp.uint32),
        compiler_params=_params(("arbitrary", "arbitrary")),
        name="ffn_up",
    )(first_blk, n_blk, n_used, xs, w_gate_up, w_gate_up, b_gate_up, b_gate_up)


def _ffn_down_body(first_ref, nblk_ref, nused_ref, h_hbm, wd_ref, bd_ref, y_hbm, wd_bf, ibuf, obuf, isem, osem):
    def prepare():
        wd_bf[...] = wd_ref[...].astype(BF16)

    def compute(paired):
        hid = pltpu.bitcast(paired, BF16)
        return jnp.dot(hid, wd_bf[...], preferred_element_type=F32) + bd_ref[...]

    _stream_expert_blocks(first_ref, nblk_ref, nused_ref, h_hbm, y_hbm, ibuf, obuf, isem, osem,
                          prepare, compute)


def _ffn_down(first_blk, n_blk, n_used, h, w_down, b_down):
    n_rows = 2 * h.shape[0]
    tn = FFN_TN
    return pl.pallas_call(
        _ffn_down_body,
        grid_spec=pltpu.PrefetchScalarGridSpec(
            num_scalar_prefetch=3,
            grid=(D_MODEL // tn, N_EXPERTS),
            in_specs=[
                pl.BlockSpec(memory_space=pltpu.HBM),
                pl.BlockSpec((None, D_FF, tn), lambda j, e, *_: (e, 0, j)),
                pl.BlockSpec((None, 1, tn), lambda j, e, *_: (e, 0, j)),
            ],
            out_specs=pl.BlockSpec(memory_space=pltpu.HBM),
            scratch_shapes=[
                pltpu.VMEM((D_FF, tn), BF16),
                pltpu.VMEM((IN_SLOTS, MOE_BM // 2, D_FF), jnp.uint32),
                pltpu.VMEM((N_SLOTS, MOE_BM, tn), F32),
                pltpu.SemaphoreType.DMA((IN_SLOTS,)),
                pltpu.SemaphoreType.DMA((N_SLOTS,)),
            ],
        ),
        out_shape=jax.ShapeDtypeStruct((n_rows, D_MODEL), F32),
        compiler_params=_params(("arbitrary", "arbitrary")),
        name="ffn_down",
    )(first_blk, n_blk, n_used, h, w_down, b_down)


COMBINE_TM = 128


def _combine_body(dest_ref, next_ref, h_ref, w_ref, g_ref, y_hbm, o_ref, gbuf, sem):
    i = pl.program_id(0)
    tm = h_ref.shape[0]

    def gather_tile(table_ref, slot):
        def start(r, c):
            for k in range(TOP_K):
                pltpu.make_async_copy(y_hbm.at[pl.ds(table_ref[0, 0, k * tm + r], 1), :],
                                      gbuf.at[slot, k, pl.ds(r, 1), :], sem.at[slot]).start(priority=k % 2)
            return c

        lax.fori_loop(0, tm, start, 0)

    slot = lax.rem(i, N_SLOTS)

    @pl.when(i == 0)
    def _():
        gather_tile(dest_ref, 0)

    @pl.when(i + 1 < pl.num_programs(0))
    def _():
        gather_tile(next_ref, 1 - slot)

    for k in range(TOP_K):
        pltpu.make_async_copy(y_hbm.at[pl.ds(0, tm), :], gbuf.at[slot, k], sem.at[slot]).wait()
    h = h_ref[...]
    for k in range(TOP_K):
        h = h + w_ref[:, k:k + 1] * gbuf[slot, k]
    o_ref[...] = h * lax.rsqrt(jnp.mean(h * h, axis=-1, keepdims=True) + EPS) * g_ref[...]


def _combine(dest_tiles, h1, w_cols, g_final, y):
    t = h1.shape[0]
    tm = dest_tiles.shape[2] // TOP_K
    d = D_MODEL
    last = t // tm - 1
    return pl.pallas_call(
        _combine_body,
        grid=(t // tm,),
        in_specs=[
            pl.BlockSpec((1, 1, TOP_K * tm), lambda i: (i, 0, 0), memory_space=pltpu.SMEM),
            pl.BlockSpec((1, 1, TOP_K * tm), lambda i: (jnp.minimum(i + 1, last), 0, 0), memory_space=pltpu.SMEM),
            pl.BlockSpec((tm, d), lambda i: (i, 0)),
            pl.BlockSpec((tm, TOP_K), lambda i: (i, 0)),
            pl.BlockSpec((1, d), lambda i: (0, 0)),
            pl.BlockSpec(memory_space=pltpu.HBM),
        ],
        out_specs=pl.BlockSpec((tm, d), lambda i: (i, 0)),
        out_shape=jax.ShapeDtypeStruct((t, d), F32),
        scratch_shapes=[pltpu.VMEM((N_SLOTS, TOP_K, tm, d), F32), pltpu.SemaphoreType.DMA((N_SLOTS,))],
        compiler_params=_params(("arbitrary",)),
        name="combine",
    )(dest_tiles, dest_tiles, h1, w_cols, g_final, y)


def _tile_major(a, tm):
    k, t = a.shape
    return a.reshape(k, t // tm, tm).transpose(1, 0, 2).reshape(t // tm, 1, k * tm)


def _routing_tables(top_e, rank, counts, t):
    bm = MOE_BM
    nb = (t * TOP_K) // bm + N_EXPERTS
    padded = (counts + bm - 1) // bm * bm
    pad_end = jnp.cumsum(padded)
    pad_start = pad_end - padded
    onehot = top_e[:, :, None] == jnp.arange(N_EXPERTS, dtype=jnp.int32)
    dest = rank + jnp.sum(jnp.where(onehot, pad_start, 0), axis=-1)
    first_blk = (pad_start // bm).astype(jnp.int32)
    n_blk = (padded // bm).astype(jnp.int32)
    n_used = (pad_end[-1] // bm).astype(jnp.int32).reshape(1)
    gap = padded - counts
    gap_end = jnp.cumsum(gap)
    j = jnp.arange(N_EXPERTS * bm, dtype=jnp.int32)
    ej = jnp.minimum(jnp.sum(j[:, None] >= gap_end[None, :], axis=1), N_EXPERTS - 1)
    pad_rows = (pad_start + counts)[ej] + j - (gap_end - gap)[ej]
    pad_rows = jnp.clip(pad_rows, 0, nb * bm - 1).astype(jnp.int32)
    n_pad = gap_end[-1].astype(jnp.int32).reshape(1)
    return dest.astype(jnp.int32), first_blk, n_blk, n_used, pad_rows, n_pad, nb * bm


def kernel(x, g_mix, w_in, ssm_conv_w, ssm_conv_b, ssm_dt_bias, ssm_a_log, ssm_d, ssm_norm_g, w_ssm_out,
           sc_conv_w, w_sc_out, b_gate, w_o, g_ffn, w_router, b_router, w_gate_up, b_gate_up, w_down,
           b_down, g_final):
    bsz, seq, d = x.shape
    t = bsz * seq
    assert bsz == 1 and d == D_MODEL and w_in.shape[0] == 1
    xt = x.reshape(t, d)
    w_dt = w_in[0, :, OFF_DT:OFF_SC]
    w_dt_hi = w_dt.astype(BF16)
    w_dt = jnp.concatenate([w_dt_hi, (w_dt - w_dt_hi.astype(F32)).astype(BF16)], axis=1)
    col = lambda a: a.reshape(-1, 1)
    row = lambda a: a.reshape(1, -1)

    u, dt_raw = _prenorm(xt, row(g_mix[0]), w_dt)
    proj = _inproj(u, w_in[0].T, ssm_conv_w[0], row(ssm_conv_b[0]))
    y_norm = _ssd(proj, dt_raw.T, col(ssm_dt_bias[0]), col(ssm_a_log[0]), col(ssm_d[0]), row(ssm_norm_g[0]))
    mixed = _mix(y_norm, proj, row(b_gate[0]), sc_conv_w[0], w_ssm_out[0].astype(BF16), w_sc_out[0].astype(BF16))
    wr_t = w_router[0].T
    wr_hi = wr_t.astype(BF16)
    wr_lo = (wr_t - wr_hi.astype(F32)).astype(BF16)
    h1, xp, top_e, top_w, rank, counts = _route(mixed, xt, w_o[0].astype(BF16), row(g_ffn[0]),
                                                jnp.concatenate([wr_hi, wr_lo], axis=0), col(b_router[0]))
    dest, first_blk, n_blk, n_used, pad_rows, n_pad, n_rows = _routing_tables(top_e, rank, counts[:, 0], t)
    xs = _dispatch(xp, _tile_major(dest, min(DISPATCH_TM, t)), pad_rows, n_pad, n_used, n_rows)
    hid = _ffn_up(first_blk, n_blk, n_used, xs, w_gate_up[0], b_gate_up[0].reshape(N_EXPERTS, 1, 2 * D_FF))
    y = _ffn_down(first_blk, n_blk, n_used, hid, w_down[0], b_down[0].reshape(N_EXPERTS, 1, D_MODEL))
    out = _combine(_tile_major(dest, min(COMBINE_TM, t)), h1, top_w.T, row(g_final), y)
    return out.reshape(bsz, seq, d)
```

```python
import functools

import jax
import jax.numpy as jnp
from jax import lax
from jax.experimental import pallas as pl
from jax.experimental.pallas import tpu as pltpu

D_MODEL = 2048
SSM_D_INNER = 2 * D_MODEL
SSM_HEAD_DIM = 64
SSM_N_HEADS = SSM_D_INNER // SSM_HEAD_DIM
SSM_N_GROUPS = 8
SSM_HEADS_PER_GROUP = SSM_N_HEADS // SSM_N_GROUPS
SSM_D_STATE = 128
SSM_CONV = 4
SSM_GN = SSM_N_GROUPS * SSM_D_STATE
SSM_CONV_DIM = SSM_D_INNER + 2 * SSM_GN
SSM_GROUP_CH = SSM_D_INNER // SSM_N_GROUPS
SC_DIM = D_MODEL
SC_WIDTH = 3
N_EXPERTS = 32
TOP_K = 4
D_FF = D_MODEL
SWIGLU_LIMIT = 7.0
SWIGLU_ALPHA = 1.702
EPS = 1e-5

OFF_Z = 0
OFF_XBC = OFF_Z + SSM_D_INNER
OFF_DT = OFF_XBC + SSM_CONV_DIM
OFF_SC = OFF_DT + SSM_N_HEADS
OFF_GATE = OFF_SC + 3 * SC_DIM
D_IN_PROJ = OFF_GATE + 2 * D_MODEL

P_Z = 0
P_XBC = P_Z + SSM_D_INNER
P_SC = P_XBC + SSM_CONV_DIM
P_GATE = P_SC + 3 * SC_DIM
P_TOTAL = P_GATE + 2 * D_MODEL

SUBLANES = 8
VMEM_LIMIT = 56 * 1024 * 1024

F32 = jnp.float32
BF16 = jnp.bfloat16
HIGHEST = lax.Precision.HIGHEST
NT_DIMS = (((1,), (1,)), ((), ()))
TN_DIMS = (((0,), (0,)), ((), ()))


def _sigmoid(v):
    return 1.0 / (1.0 + jnp.exp(-v))


def _params(semantics):
    return pltpu.CompilerParams(dimension_semantics=semantics, vmem_limit_bytes=VMEM_LIMIT)


PRENORM_TM = 512
INPROJ_TM = 1024
INPROJ_TN = 1024
LANES = 128
DT_SHIFT = OFF_SC - OFF_DT
ALIGNED_TILES = OFF_DT // INPROJ_TN
CAST_ROWS = 256
EPILOGUE_ROWS = 256


def _prenorm_body(x_ref, g_ref, wdt_ref, u_ref, dt_ref):
    x = x_ref[...]
    u = x * lax.rsqrt(jnp.mean(x * x, axis=-1, keepdims=True) + EPS) * g_ref[...]
    u_hi = u.astype(BF16)
    u_ref[...] = u_hi
    u_lo = (u - u_hi.astype(F32)).astype(BF16)
    both = jnp.dot(u_hi, wdt_ref[...], preferred_element_type=F32)
    cross = jnp.dot(u_lo, wdt_ref[...], preferred_element_type=F32)
    nh = SSM_N_HEADS
    dt_ref[...] = both[:, :nh] + both[:, nh:] + cross[:, :nh]


def _prenorm(x, g, w_dt):
    t = x.shape[0]
    tm = min(PRENORM_TM, t)
    return pl.pallas_call(
        _prenorm_body,
        grid=(t // tm,),
        in_specs=[
            pl.BlockSpec((tm, D_MODEL), lambda i: (i, 0)),
            pl.BlockSpec((1, D_MODEL), lambda i: (0, 0)),
            pl.BlockSpec((D_MODEL, 2 * SSM_N_HEADS), lambda i: (0, 0)),
        ],
        out_specs=[
            pl.BlockSpec((tm, D_MODEL), lambda i: (i, 0)),
            pl.BlockSpec((tm, SSM_N_HEADS), lambda i: (i, 0)),
        ],
        out_shape=[
            jax.ShapeDtypeStruct((t, D_MODEL), BF16),
            jax.ShapeDtypeStruct((t, SSM_N_HEADS), F32),
        ],
        compiler_params=_params(("parallel",)),
        name="prenorm",
    )(x, g, w_dt)


def _inproj_body(u_ref, wt_ref, proj_ref, w_bf):
    j, i = pl.program_id(0), pl.program_id(1)
    tm = u_ref.shape[0]
    z_tiles = P_XBC // w_bf.shape[0]

    @pl.when(i == 0)
    def _():
        for r in range(0, w_bf.shape[0], CAST_ROWS):
            w_bf[r:r + CAST_ROWS, :] = wt_ref[r:r + CAST_ROWS, :].astype(BF16)

    def project(r0, rows):
        return lax.dot_general(u_ref[r0:r0 + rows, :], w_bf[...], NT_DIMS, preferred_element_type=F32)

    chunk = min(EPILOGUE_ROWS, tm)

    @pl.when(j < z_tiles)
    def _():
        for r0 in range(0, tm, chunk):
            acc = project(r0, chunk)
            proj_ref[r0:r0 + chunk, :] = (acc * _sigmoid(acc)).astype(BF16)

    @pl.when(j >= z_tiles)
    def _():
        proj_ref[...] = project(0, tm).astype(BF16)


def _inproj(u, w_in_t):
    t = u.shape[0]
    tm = min(INPROJ_TM, t)
    tn = INPROJ_TN
    assert OFF_DT % tn == 0 and DT_SHIFT % SUBLANES == 0 and P_XBC % tn == 0
    first_row = lambda j: pl.multiple_of(j * tn + jnp.where(j >= ALIGNED_TILES, DT_SHIFT, 0), SUBLANES)
    return pl.pallas_call(
        _inproj_body,
        grid=(P_TOTAL // tn, t // tm),
        in_specs=[
            pl.BlockSpec((tm, D_MODEL), lambda j, i: (i, 0)),
            pl.BlockSpec((pl.Element(tn), pl.Element(D_MODEL)), lambda j, i: (first_row(j), 0)),
        ],
        out_specs=pl.BlockSpec((tm, tn), lambda j, i: (i, j)),
        out_shape=jax.ShapeDtypeStruct((t, P_TOTAL), BF16),
        scratch_shapes=[pltpu.VMEM((tn, D_MODEL), BF16)],
        compiler_params=_params(("arbitrary", "arbitrary")),
        name="inproj",
    )(u, w_in_t)


SSD_L = 256
HEAD_PAIR = 2 * SSM_HEAD_DIM


def _ssd_body(z_ref, x_ref, b_ref, c_ref, dt_ref, wx_ref, wb_ref, wc_ref, bx_ref, bb_ref, bc_ref,
              dtb_ref, alog_ref, d_ref, ng_ref, o_ref, s_ref, xbuf, bbuf, cbuf):
    L = x_ref.shape[0]
    tail = SUBLANES

    @pl.when(pl.program_id(1) == 0)
    def _():
        s_ref[...] = jnp.zeros_like(s_ref)
        xbuf[0:tail, :] = jnp.zeros((tail, xbuf.shape[1]), F32)
        bbuf[0:tail, :] = jnp.zeros((tail, bbuf.shape[1]), F32)
        cbuf[0:tail, :] = jnp.zeros((tail, cbuf.shape[1]), F32)

    def conv_silu(buf, in_ref, w_ref, bias_ref):
        buf[tail:tail + L, :] = in_ref[...].astype(F32)
        acc = bias_ref[...] + w_ref[SSM_CONV - 1:SSM_CONV, :] * buf[tail:tail + L, :]
        for k in range(SSM_CONV - 1):
            off = tail - (SSM_CONV - 1) + k
            acc = acc + w_ref[k:k + 1, :] * buf[off:off + L, :]
        buf[0:tail, :] = buf[L:L + tail, :]
        return acc * _sigmoid(acc)

    xs = conv_silu(xbuf, x_ref, wx_ref, bx_ref)
    bm = conv_silu(bbuf, b_ref, wb_ref, bb_ref).astype(BF16)
    cm = conv_silu(cbuf, c_ref, wc_ref, bc_ref).astype(BF16)

    dt_raw = dt_ref[...] + dtb_ref[...]
    dt = jnp.maximum(dt_raw, 0.0) + jnp.log(1.0 + jnp.exp(-jnp.abs(dt_raw)))
    da = dt * (-jnp.exp(alog_ref[...]))
    row = lax.broadcasted_iota(jnp.int32, (L, L), 0)
    col = lax.broadcasted_iota(jnp.int32, (L, L), 1)
    causal = row >= col
    incl = (row <= col).astype(BF16)
    da_hi = da.astype(BF16)
    rem = da - da_hi.astype(F32)
    da_mid = rem.astype(BF16)
    da_lo = (rem - da_mid.astype(F32)).astype(BF16)
    parts = jnp.dot(jnp.concatenate([da_hi, da_mid, da_lo], axis=0), incl, preferred_element_type=F32)
    hg = SSM_HEADS_PER_GROUP
    cs = parts[0:hg] + parts[hg:2 * hg] + parts[2 * hg:3 * hg]
    cs_end = cs[:, L - 1:L]
    flipped = jnp.concatenate([cs, dt, jnp.exp(cs_end - cs)], axis=0).T
    cs_t, dt_t, to_end_t = flipped[:, 0:hg], flipped[:, hg:2 * hg], flipped[:, 2 * hg:3 * hg]
    ecs_t = jnp.exp(cs_t)

    cb = lax.dot_general(cm, bm, NT_DIMS, preferred_element_type=F32)
    cb = jnp.where(causal, cb, 0.0)
    y_off = lax.dot_general(cm, s_ref[...].astype(BF16), NT_DIMS, preferred_element_type=F32)

    lane = lax.broadcasted_iota(jnp.int32, (L, HEAD_PAIR), 1)
    first = lane < SSM_HEAD_DIM
    ys = []
    xws = []
    for p in range(SSM_HEADS_PER_GROUP // 2):
        h0, h1 = 2 * p, 2 * p + 1
        sl = slice(p * HEAD_PAIR, (p + 1) * HEAD_PAIR)
        xp = xs[:, sl]
        pick = lambda a: jnp.where(first, a[:, h0:h0 + 1], a[:, h1:h1 + 1])
        xdt = xp * pick(dt_t)
        xdt_b = xdt.astype(BF16)
        yd = []
        for h in (h0, h1):
            seg = cs_t[:, h:h + 1] - cs[h:h + 1, :]
            m = cb * jnp.exp(jnp.minimum(seg, 0.0))
            yd.append(jnp.dot(m.astype(BF16), xdt_b, preferred_element_type=F32))
        y = jnp.where(first, yd[0], yd[1]) + y_off[:, sl] * pick(ecs_t)
        y = y + xp * jnp.where(first[0:1, :], d_ref[h0:h0 + 1, :], d_ref[h1:h1 + 1, :])
        ys.append(y)
        xws.append((xdt * pick(to_end_t)).astype(BF16))
    y = jnp.concatenate(ys, axis=1)
    xw = jnp.concatenate(xws, axis=1)

    upd = lax.dot_general(xw, bm, TN_DIMS, preferred_element_type=F32)
    chunk_decay = jnp.exp(cs_end)
    for h in range(SSM_HEADS_PER_GROUP):
        rows = slice(h * SSM_HEAD_DIM, (h + 1) * SSM_HEAD_DIM)
        s_ref[rows, :] = s_ref[rows, :] * chunk_decay[h:h + 1, :] + upd[rows, :]

    v = y * z_ref[...].astype(F32)
    v = v * lax.rsqrt(jnp.mean(v * v, axis=-1, keepdims=True) + EPS)
    o_ref[...] = (v * ng_ref[...]).astype(BF16)


def _ssd(proj, dt_rows, conv_w, conv_b, dt_bias, a_log, d_skip, norm_g):
    t = proj.shape[0]
    L = min(SSD_L, t)
    gc = SSM_GROUP_CH
    n = SSM_D_STATE
    zc, xc = P_Z // gc, P_XBC // gc
    bc, cc = (P_XBC + SSM_D_INNER) // n, (P_XBC + SSM_D_INNER + SSM_GN) // n
    wbc, wcc = SSM_D_INNER // n, (SSM_D_INNER + SSM_GN) // n
    hg = SSM_HEADS_PER_GROUP
    return pl.pallas_call(
        _ssd_body,
        grid=(SSM_N_GROUPS, t // L),
        in_specs=[
            pl.BlockSpec((L, gc), lambda g, i: (i, zc + g)),
            pl.BlockSpec((L, gc), lambda g, i: (i, xc + g)),
            pl.BlockSpec((L, n), lambda g, i: (i, bc + g)),
            pl.BlockSpec((L, n), lambda g, i: (i, cc + g)),
            pl.BlockSpec((hg, L), lambda g, i: (g, i)),
            pl.BlockSpec((SSM_CONV, gc), lambda g, i: (0, g)),
            pl.BlockSpec((SSM_CONV, n), lambda g, i: (0, wbc + g)),
            pl.BlockSpec((SSM_CONV, n), lambda g, i: (0, wcc + g)),
            pl.BlockSpec((1, gc), lambda g, i: (0, g)),
            pl.BlockSpec((1, n), lambda g, i: (0, wbc + g)),
            pl.BlockSpec((1, n), lambda g, i: (0, wcc + g)),
            pl.BlockSpec((hg, 1), lambda g, i: (g, 0)),
            pl.BlockSpec((hg, 1), lambda g, i: (g, 0)),
            pl.BlockSpec((hg, 1), lambda g, i: (g, 0)),
            pl.BlockSpec((1, gc), lambda g, i: (0, g)),
        ],
        out_specs=pl.BlockSpec((L, gc), lambda g, i: (i, g)),
        out_shape=jax.ShapeDtypeStruct((t, SSM_D_INNER), BF16),
        scratch_shapes=[
            pltpu.VMEM((gc, n), F32),
            pltpu.VMEM((L + SUBLANES, gc), F32),
            pltpu.VMEM((L + SUBLANES, n), F32),
            pltpu.VMEM((L + SUBLANES, n), F32),
        ],
        compiler_params=_params(("parallel", "arbitrary")),
        name="ssd",
    )(proj, proj, proj, proj, dt_rows, conv_w, conv_w, conv_w, conv_b, conv_b, conv_b,
      dt_bias, a_log, d_skip, norm_g)


MIX_TM = 256


def _mix_body(yn_ref, b_ref, c_ref, v_ref, cp_ref, vp_ref, g1_ref, g2_ref, bg1_ref, bg2_ref,
              wc_ref, wssm_ref, wsc_ref, o_ref, buf):
    tm = yn_ref.shape[0]
    tail = SUBLANES
    prev = cp_ref[...].astype(F32) * vp_ref[...].astype(F32)
    buf[0:tail, :] = jnp.where(pl.program_id(0) == 0, 0.0, prev)
    cv = c_ref[...].astype(F32) * v_ref[...].astype(F32)
    buf[tail:tail + tm, :] = cv
    conv = wc_ref[SC_WIDTH - 1:SC_WIDTH, :] * cv
    for k in range(SC_WIDTH - 1):
        off = tail - (SC_WIDTH - 1) + k
        conv = conv + wc_ref[k:k + 1, :] * buf[off:off + tm, :]
    sc_in = (b_ref[...].astype(F32) * conv).astype(BF16)
    y_sc = jnp.dot(sc_in, wsc_ref[...], preferred_element_type=F32)
    y_ssm = jnp.dot(yn_ref[...], wssm_ref[...], preferred_element_type=F32)
    g1 = _sigmoid(g1_ref[...].astype(F32) + bg1_ref[...])
    g2 = _sigmoid(g2_ref[...].astype(F32) + bg2_ref[...])
    o_ref[...] = (g1 * y_ssm + g2 * y_sc).astype(BF16)


def _resident(shape):
    return pl.BlockSpec(shape, lambda *_: (0,) * len(shape), pipeline_mode=pl.Buffered(1))


def _mix(y_norm, proj, b_gate, sc_conv_w, w_ssm_out, w_sc_out):
    t = y_norm.shape[0]
    tm = min(MIX_TM, t)
    d = D_MODEL
    sb, gb = P_SC // d, P_GATE // d
    prev_rows = lambda i: jnp.maximum(i * (tm // SUBLANES) - 1, 0)
    return pl.pallas_call(
        _mix_body,
        grid=(t // tm,),
        in_specs=[
            pl.BlockSpec((tm, SSM_D_INNER), lambda i: (i, 0)),
            pl.BlockSpec((tm, d), lambda i: (i, sb)),
            pl.BlockSpec((tm, d), lambda i: (i, sb + 1)),
            pl.BlockSpec((tm, d), lambda i: (i, sb + 2)),
            pl.BlockSpec((SUBLANES, d), lambda i: (prev_rows(i), sb + 1)),
            pl.BlockSpec((SUBLANES, d), lambda i: (prev_rows(i), sb + 2)),
            pl.BlockSpec((tm, d), lambda i: (i, gb)),
            pl.BlockSpec((tm, d), lambda i: (i, gb + 1)),
            pl.BlockSpec((1, d), lambda i: (0, 0)),
            pl.BlockSpec((1, d), lambda i: (0, 1)),
            pl.BlockSpec((SC_WIDTH, d), lambda i: (0, 0)),
            _resident((SSM_D_INNER, d)),
            _resident((d, d)),
        ],
        out_specs=pl.BlockSpec((tm, d), lambda i: (i, 0)),
        out_shape=jax.ShapeDtypeStruct((t, d), BF16),
        scratch_shapes=[pltpu.VMEM((tm + SUBLANES, d), F32)],
        compiler_params=_params(("parallel",)),
        name="mix",
    )(y_norm, proj, proj, proj, proj, proj, proj, proj, b_gate, b_gate, sc_conv_w, w_ssm_out, w_sc_out)


ROUTE_TM = 512
PACK_W = D_MODEL // 2


def _pack_rows(v):
    lo = lax.bitcast_convert_type(v[:, :PACK_W].astype(F32), jnp.uint32)
    hi = lax.bitcast_convert_type(v[:, PACK_W:].astype(F32), jnp.uint32)
    return hi | (lo >> 16)


def _unpack_rows(w):
    lo = lax.bitcast_convert_type(w << 16, F32).astype(BF16)
    hi = lax.bitcast_convert_type(w & jnp.uint32(0xFFFF0000), F32).astype(BF16)
    return jnp.concatenate([lo, hi], axis=1)


def _route_body(m_ref, x_ref, wo_ref, g_ref, wr_ref, br_ref,
                h_ref, xp_ref, e_ref, w_ref, r_ref, cnt_ref, carry):
    tm = m_ref.shape[0]

    @pl.when(pl.program_id(0) == 0)
    def _():
        carry[...] = jnp.zeros_like(carry)

    h = x_ref[...] + jnp.dot(m_ref[...], wo_ref[...], preferred_element_type=F32)
    h_ref[...] = h
    xn = h * lax.rsqrt(jnp.mean(h * h, axis=-1, keepdims=True) + EPS) * g_ref[...]
    xn_hi = xn.astype(BF16)
    xn_lo = (xn - xn_hi.astype(F32)).astype(BF16)
    xp_ref[...] = _pack_rows(xn_hi)

    ne = N_EXPERTS
    both = lax.dot_general(wr_ref[...], xn_hi, NT_DIMS, preferred_element_type=F32)
    cross = lax.dot_general(wr_ref[0:ne, :], xn_lo, NT_DIMS, preferred_element_type=F32)
    logits = both[0:ne] + both[ne:2 * ne] + cross + br_ref[...]
    eidx = lax.broadcasted_iota(jnp.int32, (N_EXPERTS, tm), 0)
    vals, hots = [], []
    for k in range(TOP_K):
        best = jnp.max(logits, axis=0, keepdims=True)
        arg = jnp.min(jnp.where(logits == best, eidx, N_EXPERTS), axis=0, keepdims=True)
        hot = eidx == arg
        e_ref[k:k + 1, :] = arg
        vals.append(best)
        hots.append(hot)
        logits = jnp.where(hot, -jnp.inf, logits)
    exps = [jnp.exp(v - vals[0]) for v in vals]
    denom = exps[0] + exps[1] + exps[2] + exps[3]
    for k in range(TOP_K):
        w_ref[k:k + 1, :] = exps[k] / denom

    cnt = (hots[0] | hots[1] | hots[2] | hots[3]).astype(F32)
    r_i = lax.broadcasted_iota(jnp.int32, (tm, tm), 0)
    c_i = lax.broadcasted_iota(jnp.int32, (tm, tm), 1)
    before = (r_i < c_i).astype(BF16)
    prior = carry[:, 0:1] + jnp.dot(cnt.astype(BF16), before, preferred_element_type=F32)
    for k in range(TOP_K):
        r_ref[k:k + 1, :] = jnp.sum(jnp.where(hots[k], prior, 0.0), axis=0, keepdims=True).astype(jnp.int32)
    total = carry[...] + jnp.sum(cnt, axis=1, keepdims=True)
    carry[...] = total
    cnt_ref[...] = total.astype(jnp.int32)


def _route(mixed, x, w_o, g_ffn, w_router_t, b_router):
    t = x.shape[0]
    tm = min(ROUTE_TM, t)
    d = D_MODEL
    return pl.pallas_call(
        _route_body,
        grid=(t // tm,),
        in_specs=[
            pl.BlockSpec((tm, d), lambda i: (i, 0)),
            pl.BlockSpec((tm, d), lambda i: (i, 0)),
            _resident((d, d)),
            pl.BlockSpec((1, d), lambda i: (0, 0)),
            pl.BlockSpec((2 * N_EXPERTS, d), lambda i: (0, 0)),
            pl.BlockSpec((N_EXPERTS, 1), lambda i: (0, 0)),
        ],
        out_specs=[
            pl.BlockSpec((tm, d), lambda i: (i, 0)),
            pl.BlockSpec((tm, PACK_W), lambda i: (i, 0)),
            pl.BlockSpec((TOP_K, tm), lambda i: (0, i)),
            pl.BlockSpec((TOP_K, tm), lambda i: (0, i)),
            pl.BlockSpec((TOP_K, tm), lambda i: (0, i)),
            pl.BlockSpec((N_EXPERTS, 128), lambda i: (0, 0)),
        ],
        out_shape=[
            jax.ShapeDtypeStruct((t, d), F32),
            jax.ShapeDtypeStruct((t, PACK_W), jnp.uint32),
            jax.ShapeDtypeStruct((TOP_K, t), jnp.int32),
            jax.ShapeDtypeStruct((TOP_K, t), F32),
            jax.ShapeDtypeStruct((TOP_K, t), jnp.int32),
            jax.ShapeDtypeStruct((N_EXPERTS, 128), jnp.int32),
        ],
        scratch_shapes=[pltpu.VMEM((N_EXPERTS, 128), F32)],
        compiler_params=_params(("arbitrary",)),
        name="route",
    )(mixed, x, w_o, g_ffn, w_router_t, b_router)


MOE_BM = 256
DISPATCH_TM = 256


def _dispatch_body(dest_ref, padrow_ref, npad_ref, nused_ref, xp_ref, xs_hbm, zblk, sem, zsem):
    tm = xp_ref.shape[0]
    bm = zblk.shape[0]
    nb = xs_hbm.shape[0] // bm

    @pl.when(pl.program_id(0) == 0)
    def _():
        zblk[...] = jnp.zeros_like(zblk)
        npad = npad_ref[0]
        nused = nused_ref[0]

        def zero_row(j):
            return pltpu.make_async_copy(zblk.at[pl.ds(0, 1), :], xs_hbm.at[pl.ds(padrow_ref[j], 1), :], zsem)

        def zero_block(b):
            return pltpu.make_async_copy(zblk, xs_hbm.at[pl.ds(pl.multiple_of(b * bm, bm), bm), :], zsem)

        def start_row(j, c):
            zero_row(j).start()
            return c

        def wait_row(j, c):
            zero_row(j).wait()
            return c

        def start_block(b, c):
            zero_block(b).start()
            return c

        def wait_block(b, c):
            zero_block(b).wait()
            return c

        lax.fori_loop(0, npad, start_row, 0)
        lax.fori_loop(nused, nb, start_block, 0)
        lax.fori_loop(0, npad, wait_row, 0)
        lax.fori_loop(nused, nb, wait_block, 0)

    def row_copy(r, k):
        return pltpu.make_async_copy(xp_ref.at[pl.ds(r, 1), :],
                                     xs_hbm.at[pl.ds(dest_ref[0, 0, k * tm + r], 1), :], sem)

    def start(r, c):
        for k in range(TOP_K):
            row_copy(r, k).start(priority=k % 2)
        return c

    lax.fori_loop(0, tm, start, 0)
    for k in range(TOP_K):
        pltpu.make_async_copy(xp_ref, xs_hbm.at[pl.ds(0, tm), :], sem).wait()


def _dispatch(xp, dest_tiles, pad_rows, n_pad, n_used, n_rows):
    t = xp.shape[0]
    tm = dest_tiles.shape[2] // TOP_K
    return pl.pallas_call(
        _dispatch_body,
        grid=(t // tm,),
        in_specs=[
            pl.BlockSpec((1, 1, TOP_K * tm), lambda i: (i, 0, 0), memory_space=pltpu.SMEM),
            pl.BlockSpec(memory_space=pltpu.SMEM),
            pl.BlockSpec(memory_space=pltpu.SMEM),
            pl.BlockSpec(memory_space=pltpu.SMEM),
            pl.BlockSpec((tm, PACK_W), lambda i: (i, 0)),
        ],
        out_specs=pl.BlockSpec(memory_space=pltpu.HBM),
        out_shape=jax.ShapeDtypeStruct((n_rows, PACK_W), jnp.uint32),
        scratch_shapes=[
            pltpu.VMEM((MOE_BM, PACK_W), jnp.uint32),
            pltpu.SemaphoreType.DMA(()),
            pltpu.SemaphoreType.DMA(()),
        ],
        compiler_params=_params(("arbitrary",)),
        name="dispatch",
    )(dest_tiles, pad_rows, n_pad, n_used, xp)


FFN_TF = 1024
FFN_TN = 2048
FFN_CHUNK = 512
N_SLOTS = 2
IN_SLOTS = 3
STREAM_PRIORITY = 1


def _stream_expert_blocks(first_ref, nblk_ref, nused_ref, src_hbm, dst_hbm, ibuf, obuf, isem, osem,
                          prepare, compute):
    j, e = pl.program_id(0), pl.program_id(1)
    n_in = ibuf.shape[0]
    ahead = n_in - 1
    bm_in = ibuf.shape[1]
    n_chunks, bm_out, cw = obuf.shape[1:]
    width = n_chunks * cw
    half = bm_in // 2
    col = pl.multiple_of(j * width, width)
    nblk = nblk_ref[e]
    first = first_ref[e]

    def fetch(b, slot):
        src = src_hbm.at[pl.ds(pl.multiple_of(b * bm_in, bm_in), bm_in), :]
        return pltpu.make_async_copy(src, ibuf.at[slot], isem.at[slot])

    def fetch_start(b, slot):
        for p in range(2):
            src = src_hbm.at[pl.ds(pl.multiple_of(b * bm_in + p * half, half), half), :]
            pltpu.make_async_copy(src, ibuf.at[slot, pl.ds(p * half, half), :], isem.at[slot]).start(priority=p)

    def flush(b, slot):
        out_rows = pl.ds(pl.multiple_of(b * bm_out, bm_out), bm_out)
        return [pltpu.make_async_copy(obuf.at[slot, n], dst_hbm.at[out_rows, pl.ds(col + n * cw, cw)], osem.at[slot])
                for n in range(n_chunks)]

    def flush_start(b, slot):
        for cp in flush(b, slot):
            cp.start(priority=STREAM_PRIORITY)

    def flush_wait(b, slot):
        for cp in flush(b, slot):
            cp.wait()

    for d in range(ahead):
        @pl.when(nblk > d)
        def _():
            fetch_start(first + d, d)

    @pl.when(nblk > 0)
    def _():
        prepare()

    def step(i, carry):
        slot = lax.rem(i, n_in)
        oslot = lax.rem(i, N_SLOTS)
        fetch(first + i, slot).wait()

        @pl.when(i + ahead < nblk)
        def _():
            fetch_start(first + i + ahead, lax.rem(i + ahead, n_in))

        @pl.when(i >= N_SLOTS)
        def _():
            flush_wait(first + i - N_SLOTS, oslot)

        compute(ibuf[slot], obuf.at[oslot])
        flush_start(first + i, oslot)
        return carry

    lax.fori_loop(0, nblk, step, 0)

    @pl.when(nblk >= 2)
    def _():
        flush_wait(first + nblk - 2, lax.rem(nblk, N_SLOTS))

    @pl.when(nblk >= 1)
    def _():
        flush_wait(first + nblk - 1, lax.rem(nblk - 1, N_SLOTS))

    @pl.when(e == pl.num_programs(1) - 1)
    def _():
        obuf[0] = jnp.zeros(obuf.shape[1:], obuf.dtype)

        def zero(b, carry):
            flush_start(b, 0)
            flush_wait(b, 0)
            return carry

        lax.fori_loop(nused_ref[0], dst_hbm.shape[0] // bm_out, zero, 0)


def _ffn_up_body(first_ref, nblk_ref, nused_ref, xs_hbm, wg_ref, wu_ref, bg_ref, bu_ref, h_hbm,
                 wg_bf, wu_bf, ibuf, obuf, isem, osem):
    def prepare():
        wg_bf[...] = wg_ref[...].astype(BF16)
        wu_bf[...] = wu_ref[...].astype(BF16)

    def compute(packed, out_ref):
        x = _unpack_rows(packed)
        gate = jnp.dot(x, wg_bf[...], preferred_element_type=F32) + bg_ref[...]
        up = jnp.dot(x, wu_bf[...], preferred_element_type=F32) + bu_ref[...]
        gate = jnp.minimum(gate, SWIGLU_LIMIT)
        up = jnp.clip(up, -SWIGLU_LIMIT, SWIGLU_LIMIT)
        act = ((up + 1.0) * gate * _sigmoid(SWIGLU_ALPHA * gate)).astype(BF16)
        out_ref[0] = pltpu.bitcast(act, jnp.uint32)

    _stream_expert_blocks(first_ref, nblk_ref, nused_ref, xs_hbm, h_hbm, ibuf, obuf, isem, osem,
                          prepare, compute)


def _ffn_up(first_blk, n_blk, n_used, xs, w_gate_up, b_gate_up):
    n_rows = xs.shape[0]
    tf = FFN_TF
    nf = D_FF // tf
    return pl.pallas_call(
        _ffn_up_body,
        grid_spec=pltpu.PrefetchScalarGridSpec(
            num_scalar_prefetch=3,
            grid=(nf, N_EXPERTS),
            in_specs=[
                pl.BlockSpec(memory_space=pltpu.HBM),
                pl.BlockSpec((None, D_MODEL, tf), lambda j, e, *_: (e, 0, j)),
                pl.BlockSpec((None, D_MODEL, tf), lambda j, e, *_: (e, 0, nf + j)),
                pl.BlockSpec((None, 1, tf), lambda j, e, *_: (e, 0, j)),
                pl.BlockSpec((None, 1, tf), lambda j, e, *_: (e, 0, nf + j)),
            ],
            out_specs=pl.BlockSpec(memory_space=pltpu.HBM),
            scratch_shapes=[
                pltpu.VMEM((D_MODEL, tf), BF16),
                pltpu.VMEM((D_MODEL, tf), BF16),
                pltpu.VMEM((IN_SLOTS, MOE_BM, PACK_W), jnp.uint32),
                pltpu.VMEM((N_SLOTS, 1, MOE_BM // 2, tf), jnp.uint32),
                pltpu.SemaphoreType.DMA((IN_SLOTS,)),
                pltpu.SemaphoreType.DMA((N_SLOTS,)),
            ],
        ),
        out_shape=jax.ShapeDtypeStruct((n_rows // 2, D_FF), jnp.uint32),
        compiler_params=_params(("arbitrary", "arbitrary")),
        name="ffn_up",
    )(first_blk, n_blk, n_used, xs, w_gate_up, w_gate_up, b_gate_up, b_gate_up)


def _ffn_down_body(first_ref, nblk_ref, nused_ref, h_hbm, wd_ref, bd_ref, y_hbm, wd_bf, ibuf, obuf, isem, osem):
    n_chunks, _, cw = wd_bf.shape

    def prepare():
        for n in range(n_chunks):
            wd_bf[n] = wd_ref[:, n * cw:(n + 1) * cw].astype(BF16)

    def compute(paired, out_ref):
        hid = pltpu.bitcast(paired, BF16)

        def chunk(n, carry):
            out_ref[n] = jnp.dot(hid, wd_bf[n], preferred_element_type=F32) + bd_ref[n]
            return carry

        lax.fori_loop(0, n_chunks, chunk, 0)

    _stream_expert_blocks(first_ref, nblk_ref, nused_ref, h_hbm, y_hbm, ibuf, obuf, isem, osem,
                          prepare, compute)


def _ffn_down(first_blk, n_blk, n_used, h, w_down, b_down):
    n_rows = 2 * h.shape[0]
    tn = FFN_TN
    cw = FFN_CHUNK
    nc = tn // cw
    return pl.pallas_call(
        _ffn_down_body,
        grid_spec=pltpu.PrefetchScalarGridSpec(
            num_scalar_prefetch=3,
            grid=(D_MODEL // tn, N_EXPERTS),
            in_specs=[
                pl.BlockSpec(memory_space=pltpu.HBM),
                pl.BlockSpec((None, D_FF, tn), lambda j, e, *_: (e, 0, j)),
                pl.BlockSpec((None, nc, 1, cw), lambda j, e, *_: (e, j, 0, 0)),
            ],
            out_specs=pl.BlockSpec(memory_space=pltpu.HBM),
            scratch_shapes=[
                pltpu.VMEM((nc, D_FF, cw), BF16),
                pltpu.VMEM((IN_SLOTS, MOE_BM // 2, D_FF), jnp.uint32),
                pltpu.VMEM((N_SLOTS, nc, MOE_BM, cw), F32),
                pltpu.SemaphoreType.DMA((IN_SLOTS,)),
                pltpu.SemaphoreType.DMA((N_SLOTS,)),
            ],
        ),
        out_shape=jax.ShapeDtypeStruct((n_rows, D_MODEL), F32),
        compiler_params=_params(("arbitrary", "arbitrary")),
        name="ffn_down",
    )(first_blk, n_blk, n_used, h, w_down, b_down)


COMBINE_TM = 128


def _combine_body(dest_ref, next_ref, h_ref, w_ref, g_ref, y_hbm, o_ref, gbuf, sem):
    i = pl.program_id(0)
    tm = h_ref.shape[0]

    def gather_tile(table_ref, slot):
        def start(r, c):
            for k in range(TOP_K):
                pltpu.make_async_copy(y_hbm.at[pl.ds(table_ref[0, 0, k * tm + r], 1), :],
                                      gbuf.at[slot, k, pl.ds(r, 1), :], sem.at[slot]).start(priority=k % 2)
            return c

        lax.fori_loop(0, tm, start, 0)

    slot = lax.rem(i, N_SLOTS)

    @pl.when(i == 0)
    def _():
        gather_tile(dest_ref, 0)

    @pl.when(i + 1 < pl.num_programs(0))
    def _():
        gather_tile(next_ref, 1 - slot)

    for k in range(TOP_K):
        pltpu.make_async_copy(y_hbm.at[pl.ds(0, tm), :], gbuf.at[slot, k], sem.at[slot]).wait()
    h = h_ref[...]
    for k in range(TOP_K):
        h = h + w_ref[:, k:k + 1] * gbuf[slot, k]
    o_ref[...] = h * lax.rsqrt(jnp.mean(h * h, axis=-1, keepdims=True) + EPS) * g_ref[...]


def _combine(dest_tiles, h1, w_cols, g_final, y):
    t = h1.shape[0]
    tm = dest_tiles.shape[2] // TOP_K
    d = D_MODEL
    last = t // tm - 1
    return pl.pallas_call(
        _combine_body,
        grid=(t // tm,),
        in_specs=[
            pl.BlockSpec((1, 1, TOP_K * tm), lambda i: (i, 0, 0), memory_space=pltpu.SMEM),
            pl.BlockSpec((1, 1, TOP_K * tm), lambda i: (jnp.minimum(i + 1, last), 0, 0), memory_space=pltpu.SMEM),
            pl.BlockSpec((tm, d), lambda i: (i, 0)),
            pl.BlockSpec((tm, TOP_K), lambda i: (i, 0)),
            pl.BlockSpec((1, d), lambda i: (0, 0)),
            pl.BlockSpec(memory_space=pltpu.HBM),
        ],
        out_specs=pl.BlockSpec((tm, d), lambda i: (i, 0)),
        out_shape=jax.ShapeDtypeStruct((t, d), F32),
        scratch_shapes=[pltpu.VMEM((N_SLOTS, TOP_K, tm, d), F32), pltpu.SemaphoreType.DMA((N_SLOTS,))],
        compiler_params=_params(("arbitrary",)),
        name="combine",
    )(dest_tiles, dest_tiles, h1, w_cols, g_final, y)


def _tile_major(a, tm):
    k, t = a.shape
    return a.reshape(k, t // tm, tm).transpose(1, 0, 2).reshape(t // tm, 1, k * tm)


def _routing_tables(top_e, rank, counts, t):
    bm = MOE_BM
    nb = (t * TOP_K) // bm + N_EXPERTS
    padded = (counts + bm - 1) // bm * bm
    pad_end = jnp.cumsum(padded)
    pad_start = pad_end - padded
    onehot = top_e[:, :, None] == jnp.arange(N_EXPERTS, dtype=jnp.int32)
    dest = rank + jnp.sum(jnp.where(onehot, pad_start, 0), axis=-1)
    first_blk = (pad_start // bm).astype(jnp.int32)
    n_blk = (padded // bm).astype(jnp.int32)
    n_used = (pad_end[-1] // bm).astype(jnp.int32).reshape(1)
    gap = padded - counts
    gap_end = jnp.cumsum(gap)
    j = jnp.arange(N_EXPERTS * bm, dtype=jnp.int32)
    ej = jnp.minimum(jnp.sum(j[:, None] >= gap_end[None, :], axis=1), N_EXPERTS - 1)
    pad_rows = (pad_start + counts)[ej] + j - (gap_end - gap)[ej]
    pad_rows = jnp.clip(pad_rows, 0, nb * bm - 1).astype(jnp.int32)
    n_pad = gap_end[-1].astype(jnp.int32).reshape(1)
    return dest.astype(jnp.int32), first_blk, n_blk, n_used, pad_rows, n_pad, nb * bm


def kernel(x, g_mix, w_in, ssm_conv_w, ssm_conv_b, ssm_dt_bias, ssm_a_log, ssm_d, ssm_norm_g, w_ssm_out,
           sc_conv_w, w_sc_out, b_gate, w_o, g_ffn, w_router, b_router, w_gate_up, b_gate_up, w_down,
           b_down, g_final):
    bsz, seq, d = x.shape
    t = bsz * seq
    assert bsz == 1 and d == D_MODEL and w_in.shape[0] == 1
    xt = x.reshape(t, d)
    w_dt = w_in[0, :, OFF_DT:OFF_SC]
    w_dt_hi = w_dt.astype(BF16)
    w_dt = jnp.concatenate([w_dt_hi, (w_dt - w_dt_hi.astype(F32)).astype(BF16)], axis=1)
    col = lambda a: a.reshape(-1, 1)
    row = lambda a: a.reshape(1, -1)

    u, dt_raw = _prenorm(xt, row(g_mix[0]), w_dt)
    proj = _inproj(u, w_in[0].T)
    y_norm = _ssd(proj, dt_raw.T, ssm_conv_w[0], row(ssm_conv_b[0]), col(ssm_dt_bias[0]), col(ssm_a_log[0]),
                  col(ssm_d[0]), row(ssm_norm_g[0]))
    mixed = _mix(y_norm, proj, row(b_gate[0]), sc_conv_w[0], w_ssm_out[0].astype(BF16), w_sc_out[0].astype(BF16))
    wr_t = w_router[0].T
    wr_hi = wr_t.astype(BF16)
    wr_lo = (wr_t - wr_hi.astype(F32)).astype(BF16)
    h1, xp, top_e, top_w, rank, counts = _route(mixed, xt, w_o[0].astype(BF16), row(g_ffn[0]),
                                                jnp.concatenate([wr_hi, wr_lo], axis=0), col(b_router[0]))
    dest, first_blk, n_blk, n_used, pad_rows, n_pad, n_rows = _routing_tables(top_e, rank, counts[:, 0], t)
    xs = _dispatch(xp, _tile_major(dest, min(DISPATCH_TM, t)), pad_rows, n_pad, n_used, n_rows)
    hid = _ffn_up(first_blk, n_blk, n_used, xs, w_gate_up[0], b_gate_up[0].reshape(N_EXPERTS, 1, 2 * D_FF))
    y = _ffn_down(first_blk, n_blk, n_used, hid, w_down[0],
                  b_down[0].reshape(N_EXPERTS, D_MODEL // FFN_CHUNK, 1, FFN_CHUNK))
    out = _combine(_tile_major(dest, min(COMBINE_TM, t)), h1, top_w.T, row(g_final), y)
    return out.reshape(bsz, seq, d)
```

```python
import functools

import jax
import jax.numpy as jnp
from jax import lax
from jax.experimental import pallas as pl
from jax.experimental.pallas import tpu as pltpu

D_MODEL = 2048
SSM_D_INNER = 2 * D_MODEL
SSM_HEAD_DIM = 64
SSM_N_HEADS = SSM_D_INNER // SSM_HEAD_DIM
SSM_N_GROUPS = 8
SSM_HEADS_PER_GROUP = SSM_N_HEADS // SSM_N_GROUPS
SSM_D_STATE = 128
SSM_CONV = 4
SSM_GN = SSM_N_GROUPS * SSM_D_STATE
SSM_CONV_DIM = SSM_D_INNER + 2 * SSM_GN
SSM_GROUP_CH = SSM_D_INNER // SSM_N_GROUPS
SC_DIM = D_MODEL
SC_WIDTH = 3
N_EXPERTS = 32
TOP_K = 4
D_FF = D_MODEL
SWIGLU_LIMIT = 7.0
SWIGLU_ALPHA = 1.702
EPS = 1e-5

OFF_Z = 0
OFF_XBC = OFF_Z + SSM_D_INNER
OFF_DT = OFF_XBC + SSM_CONV_DIM
OFF_SC = OFF_DT + SSM_N_HEADS
OFF_GATE = OFF_SC + 3 * SC_DIM
D_IN_PROJ = OFF_GATE + 2 * D_MODEL

P_Z = 0
P_XBC = P_Z + SSM_D_INNER
P_SC = P_XBC + SSM_CONV_DIM
P_GATE = P_SC + 3 * SC_DIM
P_TOTAL = P_GATE + 2 * D_MODEL

SUBLANES = 8
VMEM_LIMIT = 56 * 1024 * 1024

F32 = jnp.float32
BF16 = jnp.bfloat16
HIGHEST = lax.Precision.HIGHEST
NT_DIMS = (((1,), (1,)), ((), ()))
TN_DIMS = (((0,), (0,)), ((), ()))


def _sigmoid(v):
    return 1.0 / (1.0 + jnp.exp(-v))


def _params(semantics):
    return pltpu.CompilerParams(dimension_semantics=semantics, vmem_limit_bytes=VMEM_LIMIT)


PRENORM_TM = 512
INPROJ_TM = 1024
INPROJ_TN = 1024
LANES = 128
DT_SHIFT = OFF_SC - OFF_DT
ALIGNED_TILES = OFF_DT // INPROJ_TN
CAST_ROWS = 256
EPILOGUE_ROWS = 256


def _prenorm_body(x_ref, g_ref, wdt_ref, u_ref, dt_ref):
    x = x_ref[...]
    u = x * lax.rsqrt(jnp.mean(x * x, axis=-1, keepdims=True) + EPS) * g_ref[...]
    u_hi = u.astype(BF16)
    u_ref[...] = u_hi
    u_lo = (u - u_hi.astype(F32)).astype(BF16)
    both = jnp.dot(u_hi, wdt_ref[...], preferred_element_type=F32)
    cross = jnp.dot(u_lo, wdt_ref[...], preferred_element_type=F32)
    nh = SSM_N_HEADS
    dt_ref[...] = both[:, :nh] + both[:, nh:] + cross[:, :nh]


def _prenorm(x, g, w_dt):
    t = x.shape[0]
    tm = min(PRENORM_TM, t)
    return pl.pallas_call(
        _prenorm_body,
        grid=(t // tm,),
        in_specs=[
            pl.BlockSpec((tm, D_MODEL), lambda i: (i, 0)),
            pl.BlockSpec((1, D_MODEL), lambda i: (0, 0)),
            pl.BlockSpec((D_MODEL, 2 * SSM_N_HEADS), lambda i: (0, 0)),
        ],
        out_specs=[
            pl.BlockSpec((tm, D_MODEL), lambda i: (i, 0)),
            pl.BlockSpec((tm, SSM_N_HEADS), lambda i: (i, 0)),
        ],
        out_shape=[
            jax.ShapeDtypeStruct((t, D_MODEL), BF16),
            jax.ShapeDtypeStruct((t, SSM_N_HEADS), F32),
        ],
        compiler_params=_params(("parallel",)),
        name="prenorm",
    )(x, g, w_dt)


def _inproj_body(u_ref, wt_ref, proj_ref, w_bf):
    j, i = pl.program_id(0), pl.program_id(1)
    tm = u_ref.shape[0]
    z_tiles = P_XBC // w_bf.shape[0]

    @pl.when(i == 0)
    def _():
        for r in range(0, w_bf.shape[0], CAST_ROWS):
            w_bf[r:r + CAST_ROWS, :] = wt_ref[r:r + CAST_ROWS, :].astype(BF16)

    def project(r0, rows):
        return lax.dot_general(u_ref[r0:r0 + rows, :], w_bf[...], NT_DIMS, preferred_element_type=F32)

    chunk = min(EPILOGUE_ROWS, tm)

    @pl.when(j < z_tiles)
    def _():
        for r0 in range(0, tm, chunk):
            acc = project(r0, chunk)
            proj_ref[r0:r0 + chunk, :] = (acc * _sigmoid(acc)).astype(BF16)

    @pl.when(j >= z_tiles)
    def _():
        proj_ref[...] = project(0, tm).astype(BF16)


def _inproj(u, w_in_t):
    t = u.shape[0]
    tm = min(INPROJ_TM, t)
    tn = INPROJ_TN
    assert OFF_DT % tn == 0 and DT_SHIFT % SUBLANES == 0 and P_XBC % tn == 0
    first_row = lambda j: pl.multiple_of(j * tn + jnp.where(j >= ALIGNED_TILES, DT_SHIFT, 0), SUBLANES)
    return pl.pallas_call(
        _inproj_body,
        grid=(P_TOTAL // tn, t // tm),
        in_specs=[
            pl.BlockSpec((tm, D_MODEL), lambda j, i: (i, 0)),
            pl.BlockSpec((pl.Element(tn), pl.Element(D_MODEL)), lambda j, i: (first_row(j), 0)),
        ],
        out_specs=pl.BlockSpec((tm, tn), lambda j, i: (i, j)),
        out_shape=jax.ShapeDtypeStruct((t, P_TOTAL), BF16),
        scratch_shapes=[pltpu.VMEM((tn, D_MODEL), BF16)],
        compiler_params=_params(("arbitrary", "arbitrary")),
        name="inproj",
    )(u, w_in_t)


SSD_L = 256
HEAD_PAIR = 2 * SSM_HEAD_DIM


def _ssd_body(z_ref, x_ref, b_ref, c_ref, dt_ref, widen_ref, wx_ref, wb_ref, wc_ref, bx_ref, bb_ref, bc_ref,
              dtb_ref, alog_ref, d_ref, ng_ref, o_ref, s_ref, xbuf, bbuf, cbuf):
    L = x_ref.shape[0]
    tail = SUBLANES

    @pl.when(pl.program_id(1) == 0)
    def _():
        s_ref[...] = jnp.zeros_like(s_ref)
        xbuf[0:tail, :] = jnp.zeros((tail, xbuf.shape[1]), F32)
        bbuf[0:tail, :] = jnp.zeros((tail, bbuf.shape[1]), F32)
        cbuf[0:tail, :] = jnp.zeros((tail, cbuf.shape[1]), F32)

    def conv_silu(buf, in_ref, w_ref, bias_ref):
        buf[tail:tail + L, :] = in_ref[...].astype(F32)
        acc = bias_ref[...] + w_ref[SSM_CONV - 1:SSM_CONV, :] * buf[tail:tail + L, :]
        for k in range(SSM_CONV - 1):
            off = tail - (SSM_CONV - 1) + k
            acc = acc + w_ref[k:k + 1, :] * buf[off:off + L, :]
        buf[0:tail, :] = buf[L:L + tail, :]
        return acc * _sigmoid(acc)

    xs = conv_silu(xbuf, x_ref, wx_ref, bx_ref)
    bm = conv_silu(bbuf, b_ref, wb_ref, bb_ref).astype(BF16)
    cm = conv_silu(cbuf, c_ref, wc_ref, bc_ref).astype(BF16)

    dt_raw = dt_ref[...] + dtb_ref[...]
    dt = jnp.maximum(dt_raw, 0.0) + jnp.log(1.0 + jnp.exp(-jnp.abs(dt_raw)))
    da = dt * (-jnp.exp(alog_ref[...]))
    row = lax.broadcasted_iota(jnp.int32, (L, L), 0)
    col = lax.broadcasted_iota(jnp.int32, (L, L), 1)
    causal = row >= col
    incl = (row <= col).astype(BF16)
    da_hi = da.astype(BF16)
    rem = da - da_hi.astype(F32)
    da_mid = rem.astype(BF16)
    da_lo = (rem - da_mid.astype(F32)).astype(BF16)
    parts = jnp.dot(jnp.concatenate([da_hi, da_mid, da_lo], axis=0), incl, preferred_element_type=F32)
    hg = SSM_HEADS_PER_GROUP
    cs = parts[0:hg] + parts[hg:2 * hg] + parts[2 * hg:3 * hg]
    cs_end = cs[:, L - 1:L]
    def pieces(a):
        hi = a.astype(BF16).astype(F32)
        mid = (a - hi).astype(BF16).astype(F32)
        lo = (a - hi - mid).astype(BF16).astype(F32)
        return hi, mid, lo

    split = [pieces(a) for a in (dt, jnp.exp(cs_end - cs), jnp.exp(cs))]
    stacked = jnp.concatenate([split[q][s] for s in range(3) for q in range(3)] + [cs], axis=0)
    flipped = stacked.T
    cs_t = flipped[:, 9 * hg:10 * hg]
    wide = jnp.dot(flipped.astype(BF16), widen_ref[...], preferred_element_type=F32)
    gc = xs.shape[1]
    dt_x, to_end_x, ecs_x = wide[:, 0:gc], wide[:, gc:2 * gc], wide[:, 2 * gc:3 * gc]

    cb = lax.dot_general(cm, bm, NT_DIMS, preferred_element_type=F32)
    cb = jnp.where(causal, cb, 0.0)
    y_off = jnp.dot(cm, s_ref[...].astype(BF16), preferred_element_type=F32)

    xdt = xs * dt_x
    xdt_b = xdt.astype(BF16)
    lane = lax.broadcasted_iota(jnp.int32, (L, HEAD_PAIR), 1)
    first = lane < SSM_HEAD_DIM
    ys = []
    for p in range(SSM_HEADS_PER_GROUP // 2):
        sl = slice(p * HEAD_PAIR, (p + 1) * HEAD_PAIR)
        yd = []
        for h in (2 * p, 2 * p + 1):
            seg = cs_t[:, h:h + 1] - cs[h:h + 1, :]
            m = cb * jnp.exp(jnp.minimum(seg, 0.0))
            yd.append(jnp.dot(m.astype(BF16), xdt_b[:, sl], preferred_element_type=F32))
        ys.append(jnp.where(first, yd[0], yd[1]))
    y = jnp.concatenate(ys, axis=1) + y_off * ecs_x + xs * d_ref[...]

    xw = (xdt * to_end_x).astype(BF16)
    upd = lax.dot_general(bm, xw, TN_DIMS, preferred_element_type=F32)
    s_ref[...] = s_ref[...] * ecs_x[L - 1:L, :] + upd

    v = y * z_ref[...].astype(F32)
    v = v * lax.rsqrt(jnp.mean(v * v, axis=-1, keepdims=True) + EPS)
    o_ref[...] = (v * ng_ref[...]).astype(BF16)


def _ssd(proj, dt_rows, conv_w, conv_b, dt_bias, a_log, d_skip, norm_g):
    t = proj.shape[0]
    L = min(SSD_L, t)
    gc = SSM_GROUP_CH
    n = SSM_D_STATE
    zc, xc = P_Z // gc, P_XBC // gc
    bc, cc = (P_XBC + SSM_D_INNER) // n, (P_XBC + SSM_D_INNER + SSM_GN) // n
    wbc, wcc = SSM_D_INNER // n, (SSM_D_INNER + SSM_GN) // n
    hg = SSM_HEADS_PER_GROUP
    n_factors, n_pieces = 3, 3
    r = jnp.arange((n_factors * n_pieces + 1) * hg)[:, None]
    c = jnp.arange(n_factors * gc)[None, :]
    widen = ((r < n_factors * n_pieces * hg)
             & (r % (n_factors * hg) == (c // gc) * hg + (c % gc) // SSM_HEAD_DIM)).astype(BF16)
    d_wide = jnp.repeat(d_skip.reshape(-1), SSM_HEAD_DIM).reshape(1, -1)
    return pl.pallas_call(
        _ssd_body,
        grid=(SSM_N_GROUPS, t // L),
        in_specs=[
            pl.BlockSpec((L, gc), lambda g, i: (i, zc + g)),
            pl.BlockSpec((L, gc), lambda g, i: (i, xc + g)),
            pl.BlockSpec((L, n), lambda g, i: (i, bc + g)),
            pl.BlockSpec((L, n), lambda g, i: (i, cc + g)),
            pl.BlockSpec((hg, L), lambda g, i: (g, i)),
            pl.BlockSpec(widen.shape, lambda g, i: (0, 0)),
            pl.BlockSpec((SSM_CONV, gc), lambda g, i: (0, g)),
            pl.BlockSpec((SSM_CONV, n), lambda g, i: (0, wbc + g)),
            pl.BlockSpec((SSM_CONV, n), lambda g, i: (0, wcc + g)),
            pl.BlockSpec((1, gc), lambda g, i: (0, g)),
            pl.BlockSpec((1, n), lambda g, i: (0, wbc + g)),
            pl.BlockSpec((1, n), lambda g, i: (0, wcc + g)),
            pl.BlockSpec((hg, 1), lambda g, i: (g, 0)),
            pl.BlockSpec((hg, 1), lambda g, i: (g, 0)),
            pl.BlockSpec((1, gc), lambda g, i: (0, g)),
            pl.BlockSpec((1, gc), lambda g, i: (0, g)),
        ],
        out_specs=pl.BlockSpec((L, gc), lambda g, i: (i, g)),
        out_shape=jax.ShapeDtypeStruct((t, SSM_D_INNER), BF16),
        scratch_shapes=[
            pltpu.VMEM((n, gc), F32),
            pltpu.VMEM((L + SUBLANES, gc), F32),
            pltpu.VMEM((L + SUBLANES, n), F32),
            pltpu.VMEM((L + SUBLANES, n), F32),
        ],
        compiler_params=_params(("parallel", "arbitrary")),
        name="ssd",
    )(proj, proj, proj, proj, dt_rows, widen, conv_w, conv_w, conv_w, conv_b, conv_b, conv_b,
      dt_bias, a_log, d_wide, norm_g)


MIX_TM = 256


def _mix_body(yn_ref, b_ref, c_ref, v_ref, cp_ref, vp_ref, g1_ref, g2_ref, bg1_ref, bg2_ref,
              wc_ref, wssm_ref, wsc_ref, o_ref, buf):
    tm = yn_ref.shape[0]
    tail = SUBLANES
    prev = cp_ref[...].astype(F32) * vp_ref[...].astype(F32)
    buf[0:tail, :] = jnp.where(pl.program_id(0) == 0, 0.0, prev)
    cv = c_ref[...].astype(F32) * v_ref[...].astype(F32)
    buf[tail:tail + tm, :] = cv
    conv = wc_ref[SC_WIDTH - 1:SC_WIDTH, :] * cv
    for k in range(SC_WIDTH - 1):
        off = tail - (SC_WIDTH - 1) + k
        conv = conv + wc_ref[k:k + 1, :] * buf[off:off + tm, :]
    sc_in = (b_ref[...].astype(F32) * conv).astype(BF16)
    y_sc = jnp.dot(sc_in, wsc_ref[...], preferred_element_type=F32)
    y_ssm = jnp.dot(yn_ref[...], wssm_ref[...], preferred_element_type=F32)
    g1 = _sigmoid(g1_ref[...].astype(F32) + bg1_ref[...])
    g2 = _sigmoid(g2_ref[...].astype(F32) + bg2_ref[...])
    o_ref[...] = (g1 * y_ssm + g2 * y_sc).astype(BF16)


def _resident(shape):
    return pl.BlockSpec(shape, lambda *_: (0,) * len(shape), pipeline_mode=pl.Buffered(1))


def _mix(y_norm, proj, b_gate, sc_conv_w, w_ssm_out, w_sc_out):
    t = y_norm.shape[0]
    tm = min(MIX_TM, t)
    d = D_MODEL
    sb, gb = P_SC // d, P_GATE // d
    prev_rows = lambda i: jnp.maximum(i * (tm // SUBLANES) - 1, 0)
    return pl.pallas_call(
        _mix_body,
        grid=(t // tm,),
        in_specs=[
            pl.BlockSpec((tm, SSM_D_INNER), lambda i: (i, 0)),
            pl.BlockSpec((tm, d), lambda i: (i, sb)),
            pl.BlockSpec((tm, d), lambda i: (i, sb + 1)),
            pl.BlockSpec((tm, d), lambda i: (i, sb + 2)),
            pl.BlockSpec((SUBLANES, d), lambda i: (prev_rows(i), sb + 1)),
            pl.BlockSpec((SUBLANES, d), lambda i: (prev_rows(i), sb + 2)),
            pl.BlockSpec((tm, d), lambda i: (i, gb)),
            pl.BlockSpec((tm, d), lambda i: (i, gb + 1)),
            pl.BlockSpec((1, d), lambda i: (0, 0)),
            pl.BlockSpec((1, d), lambda i: (0, 1)),
            pl.BlockSpec((SC_WIDTH, d), lambda i: (0, 0)),
            _resident((SSM_D_INNER, d)),
            _resident((d, d)),
        ],
        out_specs=pl.BlockSpec((tm, d), lambda i: (i, 0)),
        out_shape=jax.ShapeDtypeStruct((t, d), BF16),
        scratch_shapes=[pltpu.VMEM((tm + SUBLANES, d), F32)],
        compiler_params=_params(("parallel",)),
        name="mix",
    )(y_norm, proj, proj, proj, proj, proj, proj, proj, b_gate, b_gate, sc_conv_w, w_ssm_out, w_sc_out)


ROUTE_TM = 512
PACK_W = D_MODEL // 2


def _pack_rows(v):
    lo = lax.bitcast_convert_type(v[:, :PACK_W].astype(F32), jnp.uint32)
    hi = lax.bitcast_convert_type(v[:, PACK_W:].astype(F32), jnp.uint32)
    return hi | (lo >> 16)


def _unpack_rows(w):
    lo = lax.bitcast_convert_type(w << 16, F32).astype(BF16)
    hi = lax.bitcast_convert_type(w & jnp.uint32(0xFFFF0000), F32).astype(BF16)
    return jnp.concatenate([lo, hi], axis=1)


def _route_body(m_ref, x_ref, wo_ref, g_ref, wr_ref, br_ref,
                h_ref, xp_ref, e_ref, w_ref, r_ref, cnt_ref, carry):
    tm = m_ref.shape[0]

    @pl.when(pl.program_id(0) == 0)
    def _():
        carry[...] = jnp.zeros_like(carry)

    h = x_ref[...] + jnp.dot(m_ref[...], wo_ref[...], preferred_element_type=F32)
    h_ref[...] = h
    xn = h * lax.rsqrt(jnp.mean(h * h, axis=-1, keepdims=True) + EPS) * g_ref[...]
    xn_hi = xn.astype(BF16)
    xn_lo = (xn - xn_hi.astype(F32)).astype(BF16)
    xp_ref[...] = _pack_rows(xn_hi)

    ne = N_EXPERTS
    both = lax.dot_general(wr_ref[...], xn_hi, NT_DIMS, preferred_element_type=F32)
    cross = lax.dot_general(wr_ref[0:ne, :], xn_lo, NT_DIMS, preferred_element_type=F32)
    logits = both[0:ne] + both[ne:2 * ne] + cross + br_ref[...]
    eidx = lax.broadcasted_iota(jnp.int32, (N_EXPERTS, tm), 0)
    vals, hots = [], []
    for k in range(TOP_K):
        best = jnp.max(logits, axis=0, keepdims=True)
        arg = jnp.min(jnp.where(logits == best, eidx, N_EXPERTS), axis=0, keepdims=True)
        hot = eidx == arg
        e_ref[k:k + 1, :] = arg
        vals.append(best)
        hots.append(hot)
        logits = jnp.where(hot, -jnp.inf, logits)
    exps = [jnp.exp(v - vals[0]) for v in vals]
    denom = exps[0] + exps[1] + exps[2] + exps[3]
    for k in range(TOP_K):
        w_ref[k:k + 1, :] = exps[k] / denom

    cnt = (hots[0] | hots[1] | hots[2] | hots[3]).astype(F32)
    r_i = lax.broadcasted_iota(jnp.int32, (tm, tm), 0)
    c_i = lax.broadcasted_iota(jnp.int32, (tm, tm), 1)
    before = (r_i < c_i).astype(BF16)
    prior = carry[:, 0:1] + jnp.dot(cnt.astype(BF16), before, preferred_element_type=F32)
    for k in range(TOP_K):
        r_ref[k:k + 1, :] = jnp.sum(jnp.where(hots[k], prior, 0.0), axis=0, keepdims=True).astype(jnp.int32)
    total = carry[...] + jnp.sum(cnt, axis=1, keepdims=True)
    carry[...] = total
    cnt_ref[...] = total.astype(jnp.int32)


def _route(mixed, x, w_o, g_ffn, w_router_t, b_router):
    t = x.shape[0]
    tm = min(ROUTE_TM, t)
    d = D_MODEL
    return pl.pallas_call(
        _route_body,
        grid=(t // tm,),
        in_specs=[
            pl.BlockSpec((tm, d), lambda i: (i, 0)),
            pl.BlockSpec((tm, d), lambda i: (i, 0)),
            _resident((d, d)),
            pl.BlockSpec((1, d), lambda i: (0, 0)),
            pl.BlockSpec((2 * N_EXPERTS, d), lambda i: (0, 0)),
            pl.BlockSpec((N_EXPERTS, 1), lambda i: (0, 0)),
        ],
        out_specs=[
            pl.BlockSpec((tm, d), lambda i: (i, 0)),
            pl.BlockSpec((tm, PACK_W), lambda i: (i, 0)),
            pl.BlockSpec((TOP_K, tm), lambda i: (0, i)),
            pl.BlockSpec((TOP_K, tm), lambda i: (0, i)),
            pl.BlockSpec((TOP_K, tm), lambda i: (0, i)),
            pl.BlockSpec((N_EXPERTS, 128), lambda i: (0, 0)),
        ],
        out_shape=[
            jax.ShapeDtypeStruct((t, d), F32),
            jax.ShapeDtypeStruct((t, PACK_W), jnp.uint32),
            jax.ShapeDtypeStruct((TOP_K, t), jnp.int32),
            jax.ShapeDtypeStruct((TOP_K, t), F32),
            jax.ShapeDtypeStruct((TOP_K, t), jnp.int32),
            jax.ShapeDtypeStruct((N_EXPERTS, 128), jnp.int32),
        ],
        scratch_shapes=[pltpu.VMEM((N_EXPERTS, 128), F32)],
        compiler_params=_params(("arbitrary",)),
        name="route",
    )(mixed, x, w_o, g_ffn, w_router_t, b_router)


MOE_BM = 256
DISPATCH_TM = 256


def _dispatch_body(dest_ref, padrow_ref, npad_ref, nused_ref, xp_ref, xs_hbm, zblk, sem, zsem):
    tm = xp_ref.shape[0]
    bm = zblk.shape[0]
    nb = xs_hbm.shape[0] // bm

    @pl.when(pl.program_id(0) == 0)
    def _():
        zblk[...] = jnp.zeros_like(zblk)
        npad = npad_ref[0]
        nused = nused_ref[0]

        def zero_row(j):
            return pltpu.make_async_copy(zblk.at[pl.ds(0, 1), :], xs_hbm.at[pl.ds(padrow_ref[j], 1), :], zsem)

        def zero_block(b):
            return pltpu.make_async_copy(zblk, xs_hbm.at[pl.ds(pl.multiple_of(b * bm, bm), bm), :], zsem)

        def start_row(j, c):
            zero_row(j).start()
            return c

        def wait_row(j, c):
            zero_row(j).wait()
            return c

        def start_block(b, c):
            zero_block(b).start()
            return c

        def wait_block(b, c):
            zero_block(b).wait()
            return c

        lax.fori_loop(0, npad, start_row, 0)
        lax.fori_loop(nused, nb, start_block, 0)
        lax.fori_loop(0, npad, wait_row, 0)
        lax.fori_loop(nused, nb, wait_block, 0)

    def row_copy(r, k):
        return pltpu.make_async_copy(xp_ref.at[pl.ds(r, 1), :],
                                     xs_hbm.at[pl.ds(dest_ref[0, 0, k * tm + r], 1), :], sem)

    def start(r, c):
        for k in range(TOP_K):
            row_copy(r, k).start(priority=k % 2)
        return c

    lax.fori_loop(0, tm, start, 0)
    for k in range(TOP_K):
        pltpu.make_async_copy(xp_ref, xs_hbm.at[pl.ds(0, tm), :], sem).wait()


def _dispatch(xp, dest_tiles, pad_rows, n_pad, n_used, n_rows):
    t = xp.shape[0]
    tm = dest_tiles.shape[2] // TOP_K
    return pl.pallas_call(
        _dispatch_body,
        grid=(t // tm,),
        in_specs=[
            pl.BlockSpec((1, 1, TOP_K * tm), lambda i: (i, 0, 0), memory_space=pltpu.SMEM),
            pl.BlockSpec(memory_space=pltpu.SMEM),
            pl.BlockSpec(memory_space=pltpu.SMEM),
            pl.BlockSpec(memory_space=pltpu.SMEM),
            pl.BlockSpec((tm, PACK_W), lambda i: (i, 0)),
        ],
        out_specs=pl.BlockSpec(memory_space=pltpu.HBM),
        out_shape=jax.ShapeDtypeStruct((n_rows, PACK_W), jnp.uint32),
        scratch_shapes=[
            pltpu.VMEM((MOE_BM, PACK_W), jnp.uint32),
            pltpu.SemaphoreType.DMA(()),
            pltpu.SemaphoreType.DMA(()),
        ],
        compiler_params=_params(("arbitrary",)),
        name="dispatch",
    )(dest_tiles, pad_rows, n_pad, n_used, xp)


FFN_TF = 1024
FFN_TN = 2048
FFN_CHUNK = 2048
N_SLOTS = 2
IN_SLOTS = 3
STREAM_PRIORITY = 1


def _stream_expert_blocks(first_ref, nblk_ref, nused_ref, src_hbm, dst_hbm, ibuf, obuf, isem, osem,
                          prepare, compute):
    j, e = pl.program_id(0), pl.program_id(1)
    n_in = ibuf.shape[0]
    ahead = n_in - 1
    bm_in = ibuf.shape[1]
    n_chunks, bm_out, cw = obuf.shape[1:]
    width = n_chunks * cw
    half = bm_in // 2
    col = pl.multiple_of(j * width, width)
    nblk = nblk_ref[e]
    first = first_ref[e]

    def fetch(b, slot):
        src = src_hbm.at[pl.ds(pl.multiple_of(b * bm_in, bm_in), bm_in), :]
        return pltpu.make_async_copy(src, ibuf.at[slot], isem.at[slot])

    def fetch_start(b, slot):
        for p in range(2):
            src = src_hbm.at[pl.ds(pl.multiple_of(b * bm_in + p * half, half), half), :]
            pltpu.make_async_copy(src, ibuf.at[slot, pl.ds(p * half, half), :], isem.at[slot]).start(priority=p)

    def flush(b, slot):
        out_rows = pl.ds(pl.multiple_of(b * bm_out, bm_out), bm_out)
        return [pltpu.make_async_copy(obuf.at[slot, n], dst_hbm.at[out_rows, pl.ds(col + n * cw, cw)], osem.at[slot])
                for n in range(n_chunks)]

    def flush_start(b, slot):
        for cp in flush(b, slot):
            cp.start(priority=STREAM_PRIORITY)

    def flush_wait(b, slot):
        for cp in flush(b, slot):
            cp.wait()

    for d in range(ahead):
        @pl.when(nblk > d)
        def _():
            fetch_start(first + d, d)

    @pl.when(nblk > 0)
    def _():
        prepare()

    def step(i, carry):
        slot = lax.rem(i, n_in)
        oslot = lax.rem(i, N_SLOTS)
        fetch(first + i, slot).wait()

        @pl.when(i + ahead < nblk)
        def _():
            fetch_start(first + i + ahead, lax.rem(i + ahead, n_in))

        @pl.when(i >= N_SLOTS)
        def _():
            flush_wait(first + i - N_SLOTS, oslot)

        compute(ibuf[slot], obuf.at[oslot])
        flush_start(first + i, oslot)
        return carry

    lax.fori_loop(0, nblk, step, 0)

    @pl.when(nblk >= 2)
    def _():
        flush_wait(first + nblk - 2, lax.rem(nblk, N_SLOTS))

    @pl.when(nblk >= 1)
    def _():
        flush_wait(first + nblk - 1, lax.rem(nblk - 1, N_SLOTS))

    @pl.when(e == pl.num_programs(1) - 1)
    def _():
        obuf[0] = jnp.zeros(obuf.shape[1:], obuf.dtype)

        def zero(b, carry):
            flush_start(b, 0)
            flush_wait(b, 0)
            return carry

        lax.fori_loop(nused_ref[0], dst_hbm.shape[0] // bm_out, zero, 0)


def _ffn_up_body(first_ref, nblk_ref, nused_ref, xs_hbm, wg_ref, wu_ref, bg_ref, bu_ref, h_hbm,
                 wg_bf, wu_bf, ibuf, obuf, isem, osem):
    def prepare():
        wg_bf[...] = wg_ref[...].astype(BF16)
        wu_bf[...] = wu_ref[...].astype(BF16)

    def compute(packed, out_ref):
        x = _unpack_rows(packed)
        gate = jnp.dot(x, wg_bf[...], preferred_element_type=F32) + bg_ref[...]
        up = jnp.dot(x, wu_bf[...], preferred_element_type=F32) + bu_ref[...]
        gate = jnp.minimum(gate, SWIGLU_LIMIT)
        up = jnp.clip(up, -SWIGLU_LIMIT, SWIGLU_LIMIT)
        act = ((up + 1.0) * gate * _sigmoid(SWIGLU_ALPHA * gate)).astype(BF16)
        out_ref[0] = pltpu.bitcast(act, jnp.uint32)

    _stream_expert_blocks(first_ref, nblk_ref, nused_ref, xs_hbm, h_hbm, ibuf, obuf, isem, osem,
                          prepare, compute)


def _ffn_up(first_blk, n_blk, n_used, xs, w_gate_up, b_gate_up):
    n_rows = xs.shape[0]
    tf = FFN_TF
    nf = D_FF // tf
    return pl.pallas_call(
        _ffn_up_body,
        grid_spec=pltpu.PrefetchScalarGridSpec(
            num_scalar_prefetch=3,
            grid=(nf, N_EXPERTS),
            in_specs=[
                pl.BlockSpec(memory_space=pltpu.HBM),
                pl.BlockSpec((None, D_MODEL, tf), lambda j, e, *_: (e, 0, j)),
                pl.BlockSpec((None, D_MODEL, tf), lambda j, e, *_: (e, 0, nf + j)),
                pl.BlockSpec((None, 1, tf), lambda j, e, *_: (e, 0, j)),
                pl.BlockSpec((None, 1, tf), lambda j, e, *_: (e, 0, nf + j)),
            ],
            out_specs=pl.BlockSpec(memory_space=pltpu.HBM),
            scratch_shapes=[
                pltpu.VMEM((D_MODEL, tf), BF16),
                pltpu.VMEM((D_MODEL, tf), BF16),
                pltpu.VMEM((IN_SLOTS, MOE_BM, PACK_W), jnp.uint32),
                pltpu.VMEM((N_SLOTS, 1, MOE_BM // 2, tf), jnp.uint32),
                pltpu.SemaphoreType.DMA((IN_SLOTS,)),
                pltpu.SemaphoreType.DMA((N_SLOTS,)),
            ],
        ),
        out_shape=jax.ShapeDtypeStruct((n_rows // 2, D_FF), jnp.uint32),
        compiler_params=_params(("arbitrary", "arbitrary")),
        name="ffn_up",
    )(first_blk, n_blk, n_used, xs, w_gate_up, w_gate_up, b_gate_up, b_gate_up)


def _ffn_down_body(first_ref, nblk_ref, nused_ref, h_hbm, wd_ref, bd_ref, y_hbm, wd_bf, ibuf, obuf, isem, osem):
    n_chunks, _, cw = wd_bf.shape

    def prepare():
        for n in range(n_chunks):
            wd_bf[n] = wd_ref[:, n * cw:(n + 1) * cw].astype(BF16)

    def compute(paired, out_ref):
        hid = pltpu.bitcast(paired, BF16)

        def chunk(n, carry):
            out_ref[n] = jnp.dot(hid, wd_bf[n], preferred_element_type=F32) + bd_ref[n]
            return carry

        lax.fori_loop(0, n_chunks, chunk, 0)

    _stream_expert_blocks(first_ref, nblk_ref, nused_ref, h_hbm, y_hbm, ibuf, obuf, isem, osem,
                          prepare, compute)


def _ffn_down(first_blk, n_blk, n_used, h, w_down, b_down):
    n_rows = 2 * h.shape[0]
    tn = FFN_TN
    cw = FFN_CHUNK
    nc = tn // cw
    return pl.pallas_call(
        _ffn_down_body,
        grid_spec=pltpu.PrefetchScalarGridSpec(
            num_scalar_prefetch=3,
            grid=(D_MODEL // tn, N_EXPERTS),
            in_specs=[
                pl.BlockSpec(memory_space=pltpu.HBM),
                pl.BlockSpec((None, D_FF, tn), lambda j, e, *_: (e, 0, j)),
                pl.BlockSpec((None, nc, 1, cw), lambda j, e, *_: (e, j, 0, 0)),
            ],
            out_specs=pl.BlockSpec(memory_space=pltpu.HBM),
            scratch_shapes=[
                pltpu.VMEM((nc, D_FF, cw), BF16),
                pltpu.VMEM((IN_SLOTS, MOE_BM // 2, D_FF), jnp.uint32),
                pltpu.VMEM((N_SLOTS, nc, MOE_BM, cw), F32),
                pltpu.SemaphoreType.DMA((IN_SLOTS,)),
                pltpu.SemaphoreType.DMA((N_SLOTS,)),
            ],
        ),
        out_shape=jax.ShapeDtypeStruct((n_rows, D_MODEL), F32),
        compiler_params=_params(("arbitrary", "arbitrary")),
        name="ffn_down",
    )(first_blk, n_blk, n_used, h, w_down, b_down)


COMBINE_TM = 128


def _combine_body(dest_ref, next_ref, h_ref, w_ref, g_ref, y_hbm, o_ref, gbuf, sem):
    i = pl.program_id(0)
    tm = h_ref.shape[0]

    def gather_tile(table_ref, slot):
        def start(r, c):
            for k in range(TOP_K):
                pltpu.make_async_copy(y_hbm.at[pl.ds(table_ref[0, 0, k * tm + r], 1), :],
                                      gbuf.at[slot, k, pl.ds(r, 1), :], sem.at[slot]).start(priority=k % 2)
            return c

        lax.fori_loop(0, tm, start, 0)

    slot = lax.rem(i, N_SLOTS)

    @pl.when(i == 0)
    def _():
        gather_tile(dest_ref, 0)

    @pl.when(i + 1 < pl.num_programs(0))
    def _():
        gather_tile(next_ref, 1 - slot)

    for k in range(TOP_K):
        pltpu.make_async_copy(y_hbm.at[pl.ds(0, tm), :], gbuf.at[slot, k], sem.at[slot]).wait()
    h = h_ref[...]
    for k in range(TOP_K):
        h = h + w_ref[:, k:k + 1] * gbuf[slot, k]
    o_ref[...] = h * lax.rsqrt(jnp.mean(h * h, axis=-1, keepdims=True) + EPS) * g_ref[...]


def _combine(dest_tiles, h1, w_cols, g_final, y):
    t = h1.shape[0]
    tm = dest_tiles.shape[2] // TOP_K
    d = D_MODEL
    last = t // tm - 1
    return pl.pallas_call(
        _combine_body,
        grid=(t // tm,),
        in_specs=[
            pl.BlockSpec((1, 1, TOP_K * tm), lambda i: (i, 0, 0), memory_space=pltpu.SMEM),
            pl.BlockSpec((1, 1, TOP_K * tm), lambda i: (jnp.minimum(i + 1, last), 0, 0), memory_space=pltpu.SMEM),
            pl.BlockSpec((tm, d), lambda i: (i, 0)),
            pl.BlockSpec((tm, TOP_K), lambda i: (i, 0)),
            pl.BlockSpec((1, d), lambda i: (0, 0)),
            pl.BlockSpec(memory_space=pltpu.HBM),
        ],
        out_specs=pl.BlockSpec((tm, d), lambda i: (i, 0)),
        out_shape=jax.ShapeDtypeStruct((t, d), F32),
        scratch_shapes=[pltpu.VMEM((N_SLOTS, TOP_K, tm, d), F32), pltpu.SemaphoreType.DMA((N_SLOTS,))],
        compiler_params=_params(("arbitrary",)),
        name="combine",
    )(dest_tiles, dest_tiles, h1, w_cols, g_final, y)


def _tile_major(a, tm):
    k, t = a.shape
    return a.reshape(k, t // tm, tm).transpose(1, 0, 2).reshape(t // tm, 1, k * tm)


def _routing_tables(top_e, rank, counts, t):
    bm = MOE_BM
    nb = (t * TOP_K) // bm + N_EXPERTS
    padded = (counts + bm - 1) // bm * bm
    pad_end = jnp.cumsum(padded)
    pad_start = pad_end - padded
    onehot = top_e[:, :, None] == jnp.arange(N_EXPERTS, dtype=jnp.int32)
    dest = rank + jnp.sum(jnp.where(onehot, pad_start, 0), axis=-1)
    first_blk = (pad_start // bm).astype(jnp.int32)
    n_blk = (padded // bm).astype(jnp.int32)
    n_used = (pad_end[-1] // bm).astype(jnp.int32).reshape(1)
    gap = padded - counts
    gap_end = jnp.cumsum(gap)
    j = jnp.arange(N_EXPERTS * bm, dtype=jnp.int32)
    ej = jnp.minimum(jnp.sum(j[:, None] >= gap_end[None, :], axis=1), N_EXPERTS - 1)
    pad_rows = (pad_start + counts)[ej] + j - (gap_end - gap)[ej]
    pad_rows = jnp.clip(pad_rows, 0, nb * bm - 1).astype(jnp.int32)
    n_pad = gap_end[-1].astype(jnp.int32).reshape(1)
    return dest.astype(jnp.int32), first_blk, n_blk, n_used, pad_rows, n_pad, nb * bm


def kernel(x, g_mix, w_in, ssm_conv_w, ssm_conv_b, ssm_dt_bias, ssm_a_log, ssm_d, ssm_norm_g, w_ssm_out,
           sc_conv_w, w_sc_out, b_gate, w_o, g_ffn, w_router, b_router, w_gate_up, b_gate_up, w_down,
           b_down, g_final):
    bsz, seq, d = x.shape
    t = bsz * seq
    assert bsz == 1 and d == D_MODEL and w_in.shape[0] == 1
    xt = x.reshape(t, d)
    w_dt = w_in[0, :, OFF_DT:OFF_SC]
    w_dt_hi = w_dt.astype(BF16)
    w_dt = jnp.concatenate([w_dt_hi, (w_dt - w_dt_hi.astype(F32)).astype(BF16)], axis=1)
    col = lambda a: a.reshape(-1, 1)
    row = lambda a: a.reshape(1, -1)

    u, dt_raw = _prenorm(xt, row(g_mix[0]), w_dt)
    proj = _inproj(u, w_in[0].T)
    y_norm = _ssd(proj, dt_raw.T, ssm_conv_w[0], row(ssm_conv_b[0]), col(ssm_dt_bias[0]), col(ssm_a_log[0]),
                  col(ssm_d[0]), row(ssm_norm_g[0]))
    mixed = _mix(y_norm, proj, row(b_gate[0]), sc_conv_w[0], w_ssm_out[0].astype(BF16), w_sc_out[0].astype(BF16))
    wr_t = w_router[0].T
    wr_hi = wr_t.astype(BF16)
    wr_lo = (wr_t - wr_hi.astype(F32)).astype(BF16)
    h1, xp, top_e, top_w, rank, counts = _route(mixed, xt, w_o[0].astype(BF16), row(g_ffn[0]),
                                                jnp.concatenate([wr_hi, wr_lo], axis=0), col(b_router[0]))
    dest, first_blk, n_blk, n_used, pad_rows, n_pad, n_rows = _routing_tables(top_e, rank, counts[:, 0], t)
    xs = _dispatch(xp, _tile_major(dest, min(DISPATCH_TM, t)), pad_rows, n_pad, n_used, n_rows)
    hid = _ffn_up(first_blk, n_blk, n_used, xs, w_gate_up[0], b_gate_up[0].reshape(N_EXPERTS, 1, 2 * D_FF))
    y = _ffn_down(first_blk, n_blk, n_used, hid, w_down[0],
                  b_down[0].reshape(N_EXPERTS, D_MODEL // FFN_CHUNK, 1, FFN_CHUNK))
    out = _combine(_tile_major(dest, min(COMBINE_TM, t)), h1, top_w.T, row(g_final), y)
    return out.reshape(bsz, seq, d)
```

```python
import functools

import jax
import jax.numpy as jnp
from jax import lax
from jax.experimental import pallas as pl
from jax.experimental.pallas import tpu as pltpu

D_MODEL = 2048
SSM_D_INNER = 2 * D_MODEL
SSM_HEAD_DIM = 64
SSM_N_HEADS = SSM_D_INNER // SSM_HEAD_DIM
SSM_N_GROUPS = 8
SSM_HEADS_PER_GROUP = SSM_N_HEADS // SSM_N_GROUPS
SSM_D_STATE = 128
SSM_CONV = 4
SSM_GN = SSM_N_GROUPS * SSM_D_STATE
SSM_CONV_DIM = SSM_D_INNER + 2 * SSM_GN
SSM_GROUP_CH = SSM_D_INNER // SSM_N_GROUPS
SC_DIM = D_MODEL
SC_WIDTH = 3
N_EXPERTS = 32
TOP_K = 4
D_FF = D_MODEL
SWIGLU_LIMIT = 7.0
SWIGLU_ALPHA = 1.702
EPS = 1e-5

OFF_Z = 0
OFF_XBC = OFF_Z + SSM_D_INNER
OFF_DT = OFF_XBC + SSM_CONV_DIM
OFF_SC = OFF_DT + SSM_N_HEADS
OFF_GATE = OFF_SC + 3 * SC_DIM
D_IN_PROJ = OFF_GATE + 2 * D_MODEL

P_Z = 0
P_XBC = P_Z + SSM_D_INNER
P_SC = P_XBC + SSM_CONV_DIM
P_GATE = P_SC + 3 * SC_DIM
P_TOTAL = P_GATE + 2 * D_MODEL

SUBLANES = 8
VMEM_LIMIT = 56 * 1024 * 1024

F32 = jnp.float32
BF16 = jnp.bfloat16
HIGHEST = lax.Precision.HIGHEST
NT_DIMS = (((1,), (1,)), ((), ()))
TN_DIMS = (((0,), (0,)), ((), ()))


def _sigmoid(v):
    return 1.0 / (1.0 + jnp.exp(-v))


def _params(semantics):
    return pltpu.CompilerParams(dimension_semantics=semantics, vmem_limit_bytes=VMEM_LIMIT)


PRENORM_TM = 512
INPROJ_TM = 1024
INPROJ_TN = 1024
LANES = 128
DT_SHIFT = OFF_SC - OFF_DT
ALIGNED_TILES = OFF_DT // INPROJ_TN
CAST_ROWS = 256
EPILOGUE_ROWS = 256


def _prenorm_body(x_ref, g_ref, wdt_ref, u_ref, dt_ref):
    x = x_ref[...]
    u = x * lax.rsqrt(jnp.mean(x * x, axis=-1, keepdims=True) + EPS) * g_ref[...]
    u_hi = u.astype(BF16)
    u_ref[...] = u_hi
    u_lo = (u - u_hi.astype(F32)).astype(BF16)
    both = jnp.dot(u_hi, wdt_ref[...], preferred_element_type=F32)
    cross = jnp.dot(u_lo, wdt_ref[...], preferred_element_type=F32)
    nh = SSM_N_HEADS
    dt_ref[...] = both[:, :nh] + both[:, nh:] + cross[:, :nh]


def _prenorm(x, g, w_dt):
    t = x.shape[0]
    tm = min(PRENORM_TM, t)
    return pl.pallas_call(
        _prenorm_body,
        grid=(t // tm,),
        in_specs=[
            pl.BlockSpec((tm, D_MODEL), lambda i: (i, 0)),
            pl.BlockSpec((1, D_MODEL), lambda i: (0, 0)),
            pl.BlockSpec((D_MODEL, 2 * SSM_N_HEADS), lambda i: (0, 0)),
        ],
        out_specs=[
            pl.BlockSpec((tm, D_MODEL), lambda i: (i, 0)),
            pl.BlockSpec((tm, SSM_N_HEADS), lambda i: (i, 0)),
        ],
        out_shape=[
            jax.ShapeDtypeStruct((t, D_MODEL), BF16),
            jax.ShapeDtypeStruct((t, SSM_N_HEADS), F32),
        ],
        compiler_params=_params(("parallel",)),
        name="prenorm",
    )(x, g, w_dt)


def _inproj_body(u_ref, wt_ref, proj_ref, w_bf):
    j, i = pl.program_id(0), pl.program_id(1)
    tm = u_ref.shape[0]
    z_tiles = P_XBC // w_bf.shape[0]

    @pl.when(i == 0)
    def _():
        for r in range(0, w_bf.shape[0], CAST_ROWS):
            w_bf[r:r + CAST_ROWS, :] = wt_ref[r:r + CAST_ROWS, :].astype(BF16)

    def project(r0, rows):
        return lax.dot_general(u_ref[r0:r0 + rows, :], w_bf[...], NT_DIMS, preferred_element_type=F32)

    chunk = min(EPILOGUE_ROWS, tm)

    @pl.when(j < z_tiles)
    def _():
        for r0 in range(0, tm, chunk):
            acc = project(r0, chunk)
            proj_ref[r0:r0 + chunk, :] = (acc * _sigmoid(acc)).astype(BF16)

    @pl.when(j >= z_tiles)
    def _():
        proj_ref[...] = project(0, tm).astype(BF16)


def _inproj(u, w_in_t):
    t = u.shape[0]
    tm = min(INPROJ_TM, t)
    tn = INPROJ_TN
    assert OFF_DT % tn == 0 and DT_SHIFT % SUBLANES == 0 and P_XBC % tn == 0
    first_row = lambda j: pl.multiple_of(j * tn + jnp.where(j >= ALIGNED_TILES, DT_SHIFT, 0), SUBLANES)
    return pl.pallas_call(
        _inproj_body,
        grid=(P_TOTAL // tn, t // tm),
        in_specs=[
            pl.BlockSpec((tm, D_MODEL), lambda j, i: (i, 0)),
            pl.BlockSpec((pl.Element(tn), pl.Element(D_MODEL)), lambda j, i: (first_row(j), 0)),
        ],
        out_specs=pl.BlockSpec((tm, tn), lambda j, i: (i, j)),
        out_shape=jax.ShapeDtypeStruct((t, P_TOTAL), BF16),
        scratch_shapes=[pltpu.VMEM((tn, D_MODEL), BF16)],
        compiler_params=_params(("arbitrary", "arbitrary")),
        name="inproj",
    )(u, w_in_t)


SSD_L = 256
HEAD_PAIR = 2 * SSM_HEAD_DIM


def _ssd_body(z_ref, x_ref, b_ref, c_ref, dt_ref, widen_ref, wx_ref, wb_ref, wc_ref, bx_ref, bb_ref, bc_ref,
              dtb_ref, alog_ref, d_ref, ng_ref, o_ref, s_ref, xbuf, bbuf, cbuf):
    L = x_ref.shape[0]
    tail = SUBLANES

    @pl.when(pl.program_id(1) == 0)
    def _():
        s_ref[...] = jnp.zeros_like(s_ref)
        xbuf[0:tail, :] = jnp.zeros((tail, xbuf.shape[1]), F32)
        bbuf[0:tail, :] = jnp.zeros((tail, bbuf.shape[1]), F32)
        cbuf[0:tail, :] = jnp.zeros((tail, cbuf.shape[1]), F32)

    def conv_silu(buf, in_ref, w_ref, bias_ref):
        buf[tail:tail + L, :] = in_ref[...].astype(F32)
        acc = bias_ref[...] + w_ref[SSM_CONV - 1:SSM_CONV, :] * buf[tail:tail + L, :]
        for k in range(SSM_CONV - 1):
            off = tail - (SSM_CONV - 1) + k
            acc = acc + w_ref[k:k + 1, :] * buf[off:off + L, :]
        buf[0:tail, :] = buf[L:L + tail, :]
        return acc * _sigmoid(acc)

    xs = conv_silu(xbuf, x_ref, wx_ref, bx_ref)
    bm = conv_silu(bbuf, b_ref, wb_ref, bb_ref).astype(BF16)
    cm = conv_silu(cbuf, c_ref, wc_ref, bc_ref).astype(BF16)

    dt_raw = dt_ref[...] + dtb_ref[...]
    dt = jnp.maximum(dt_raw, 0.0) + jnp.log(1.0 + jnp.exp(-jnp.abs(dt_raw)))
    da = dt * (-jnp.exp(alog_ref[...]))
    row = lax.broadcasted_iota(jnp.int32, (L, L), 0)
    col = lax.broadcasted_iota(jnp.int32, (L, L), 1)
    causal = row >= col
    incl = (row <= col).astype(BF16)
    da_hi = da.astype(BF16)
    rem = da - da_hi.astype(F32)
    da_mid = rem.astype(BF16)
    da_lo = (rem - da_mid.astype(F32)).astype(BF16)
    parts = jnp.dot(jnp.concatenate([da_hi, da_mid, da_lo], axis=0), incl, preferred_element_type=F32)
    hg = SSM_HEADS_PER_GROUP
    cs = parts[0:hg] + parts[hg:2 * hg] + parts[2 * hg:3 * hg]
    cs_end = cs[:, L - 1:L]
    def pieces(a):
        hi = a.astype(BF16).astype(F32)
        mid = (a - hi).astype(BF16).astype(F32)
        lo = (a - hi - mid).astype(BF16).astype(F32)
        return hi, mid, lo

    split = [pieces(a) for a in (dt, jnp.exp(cs_end - cs), jnp.exp(cs))]
    stacked = jnp.concatenate([split[q][s] for s in range(3) for q in range(3)] + [cs], axis=0)
    flipped = stacked.T
    cs_t = flipped[:, 9 * hg:10 * hg]
    wide = jnp.dot(flipped.astype(BF16), widen_ref[...], preferred_element_type=F32)
    gc = xs.shape[1]
    dt_x, to_end_x, ecs_x = wide[:, 0:gc], wide[:, gc:2 * gc], wide[:, 2 * gc:3 * gc]

    cb = lax.dot_general(cm, bm, NT_DIMS, preferred_element_type=F32)
    cb = jnp.where(causal, cb, 0.0)
    y_off = jnp.dot(cm, s_ref[...].astype(BF16), preferred_element_type=F32)

    xdt = xs * dt_x
    xdt_b = xdt.astype(BF16)
    lane = lax.broadcasted_iota(jnp.int32, (L, HEAD_PAIR), 1)
    first = lane < SSM_HEAD_DIM
    ys = []
    for p in range(SSM_HEADS_PER_GROUP // 2):
        sl = slice(p * HEAD_PAIR, (p + 1) * HEAD_PAIR)
        yd = []
        for h in (2 * p, 2 * p + 1):
            seg = cs_t[:, h:h + 1] - cs[h:h + 1, :]
            m = cb * jnp.exp(jnp.minimum(seg, 0.0))
            yd.append(jnp.dot(m.astype(BF16), xdt_b[:, sl], preferred_element_type=F32))
        ys.append(jnp.where(first, yd[0], yd[1]))
    y = jnp.concatenate(ys, axis=1) + y_off * ecs_x + xs * d_ref[...]

    xw = (xdt * to_end_x).astype(BF16)
    upd = lax.dot_general(bm, xw, TN_DIMS, preferred_element_type=F32)
    s_ref[...] = s_ref[...] * ecs_x[L - 1:L, :] + upd

    v = y * z_ref[...].astype(F32)
    v = v * lax.rsqrt(jnp.mean(v * v, axis=-1, keepdims=True) + EPS)
    o_ref[...] = (v * ng_ref[...]).astype(BF16)


def _ssd(proj, dt_rows, conv_w, conv_b, dt_bias, a_log, d_skip, norm_g):
    t = proj.shape[0]
    L = min(SSD_L, t)
    gc = SSM_GROUP_CH
    n = SSM_D_STATE
    zc, xc = P_Z // gc, P_XBC // gc
    bc, cc = (P_XBC + SSM_D_INNER) // n, (P_XBC + SSM_D_INNER + SSM_GN) // n
    wbc, wcc = SSM_D_INNER // n, (SSM_D_INNER + SSM_GN) // n
    hg = SSM_HEADS_PER_GROUP
    n_factors, n_pieces = 3, 3
    r = jnp.arange((n_factors * n_pieces + 1) * hg)[:, None]
    c = jnp.arange(n_factors * gc)[None, :]
    widen = ((r < n_factors * n_pieces * hg)
             & (r % (n_factors * hg) == (c // gc) * hg + (c % gc) // SSM_HEAD_DIM)).astype(BF16)
    d_wide = jnp.repeat(d_skip.reshape(-1), SSM_HEAD_DIM).reshape(1, -1)
    return pl.pallas_call(
        _ssd_body,
        grid=(SSM_N_GROUPS, t // L),
        in_specs=[
            pl.BlockSpec((L, gc), lambda g, i: (i, zc + g)),
            pl.BlockSpec((L, gc), lambda g, i: (i, xc + g)),
            pl.BlockSpec((L, n), lambda g, i: (i, bc + g)),
            pl.BlockSpec((L, n), lambda g, i: (i, cc + g)),
            pl.BlockSpec((hg, L), lambda g, i: (g, i)),
            pl.BlockSpec(widen.shape, lambda g, i: (0, 0)),
            pl.BlockSpec((SSM_CONV, gc), lambda g, i: (0, g)),
            pl.BlockSpec((SSM_CONV, n), lambda g, i: (0, wbc + g)),
            pl.BlockSpec((SSM_CONV, n), lambda g, i: (0, wcc + g)),
            pl.BlockSpec((1, gc), lambda g, i: (0, g)),
            pl.BlockSpec((1, n), lambda g, i: (0, wbc + g)),
            pl.BlockSpec((1, n), lambda g, i: (0, wcc + g)),
            pl.BlockSpec((hg, 1), lambda g, i: (g, 0)),
            pl.BlockSpec((hg, 1), lambda g, i: (g, 0)),
            pl.BlockSpec((1, gc), lambda g, i: (0, g)),
            pl.BlockSpec((1, gc), lambda g, i: (0, g)),
        ],
        out_specs=pl.BlockSpec((L, gc), lambda g, i: (i, g)),
        out_shape=jax.ShapeDtypeStruct((t, SSM_D_INNER), BF16),
        scratch_shapes=[
            pltpu.VMEM((n, gc), F32),
            pltpu.VMEM((L + SUBLANES, gc), F32),
            pltpu.VMEM((L + SUBLANES, n), F32),
            pltpu.VMEM((L + SUBLANES, n), F32),
        ],
        compiler_params=_params(("parallel", "arbitrary")),
        name="ssd",
    )(proj, proj, proj, proj, dt_rows, widen, conv_w, conv_w, conv_w, conv_b, conv_b, conv_b,
      dt_bias, a_log, d_wide, norm_g)


MIX_TM = 256


def _mix_body(yn_ref, b_ref, c_ref, v_ref, cp_ref, vp_ref, g1_ref, g2_ref, bg1_ref, bg2_ref,
              wc_ref, wssm_ref, wsc_ref, o_ref, buf):
    tm = yn_ref.shape[0]
    tail = SUBLANES
    prev = cp_ref[...].astype(F32) * vp_ref[...].astype(F32)
    buf[0:tail, :] = jnp.where(pl.program_id(0) == 0, 0.0, prev)
    cv = c_ref[...].astype(F32) * v_ref[...].astype(F32)
    buf[tail:tail + tm, :] = cv
    conv = wc_ref[SC_WIDTH - 1:SC_WIDTH, :] * cv
    for k in range(SC_WIDTH - 1):
        off = tail - (SC_WIDTH - 1) + k
        conv = conv + wc_ref[k:k + 1, :] * buf[off:off + tm, :]
    sc_in = (b_ref[...].astype(F32) * conv).astype(BF16)
    y_sc = jnp.dot(sc_in, wsc_ref[...], preferred_element_type=F32)
    y_ssm = jnp.dot(yn_ref[...], wssm_ref[...], preferred_element_type=F32)
    g1 = _sigmoid(g1_ref[...].astype(F32) + bg1_ref[...])
    g2 = _sigmoid(g2_ref[...].astype(F32) + bg2_ref[...])
    o_ref[...] = (g1 * y_ssm + g2 * y_sc).astype(BF16)


def _resident(shape):
    return pl.BlockSpec(shape, lambda *_: (0,) * len(shape), pipeline_mode=pl.Buffered(1))


def _mix(y_norm, proj, b_gate, sc_conv_w, w_ssm_out, w_sc_out):
    t = y_norm.shape[0]
    tm = min(MIX_TM, t)
    d = D_MODEL
    sb, gb = P_SC // d, P_GATE // d
    prev_rows = lambda i: jnp.maximum(i * (tm // SUBLANES) - 1, 0)
    return pl.pallas_call(
        _mix_body,
        grid=(t // tm,),
        in_specs=[
            pl.BlockSpec((tm, SSM_D_INNER), lambda i: (i, 0)),
            pl.BlockSpec((tm, d), lambda i: (i, sb)),
            pl.BlockSpec((tm, d), lambda i: (i, sb + 1)),
            pl.BlockSpec((tm, d), lambda i: (i, sb + 2)),
            pl.BlockSpec((SUBLANES, d), lambda i: (prev_rows(i), sb + 1)),
            pl.BlockSpec((SUBLANES, d), lambda i: (prev_rows(i), sb + 2)),
            pl.BlockSpec((tm, d), lambda i: (i, gb)),
            pl.BlockSpec((tm, d), lambda i: (i, gb + 1)),
            pl.BlockSpec((1, d), lambda i: (0, 0)),
            pl.BlockSpec((1, d), lambda i: (0, 1)),
            pl.BlockSpec((SC_WIDTH, d), lambda i: (0, 0)),
            _resident((SSM_D_INNER, d)),
            _resident((d, d)),
        ],
        out_specs=pl.BlockSpec((tm, d), lambda i: (i, 0)),
        out_shape=jax.ShapeDtypeStruct((t, d), BF16),
        scratch_shapes=[pltpu.VMEM((tm + SUBLANES, d), F32)],
        compiler_params=_params(("parallel",)),
        name="mix",
    )(y_norm, proj, proj, proj, proj, proj, proj, proj, b_gate, b_gate, sc_conv_w, w_ssm_out, w_sc_out)


ROUTE_TM = 512
PACK_W = D_MODEL // 2


def _pack_rows(v):
    lo = lax.bitcast_convert_type(v[:, :PACK_W].astype(F32), jnp.uint32)
    hi = lax.bitcast_convert_type(v[:, PACK_W:].astype(F32), jnp.uint32)
    return hi | (lo >> 16)


def _unpack_rows(w):
    lo = lax.bitcast_convert_type(w << 16, F32).astype(BF16)
    hi = lax.bitcast_convert_type(w & jnp.uint32(0xFFFF0000), F32).astype(BF16)
    return jnp.concatenate([lo, hi], axis=1)


def _route_body(m_ref, x_ref, wo_ref, g_ref, wr_ref, br_ref,
                h_ref, xp_ref, e_ref, w_ref, r_ref, cnt_ref, carry):
    tm = m_ref.shape[0]

    @pl.when(pl.program_id(0) == 0)
    def _():
        carry[...] = jnp.zeros_like(carry)

    h = x_ref[...] + jnp.dot(m_ref[...], wo_ref[...], preferred_element_type=F32)
    h_ref[...] = h
    xn = h * lax.rsqrt(jnp.mean(h * h, axis=-1, keepdims=True) + EPS) * g_ref[...]
    xn_hi = xn.astype(BF16)
    xn_lo = (xn - xn_hi.astype(F32)).astype(BF16)
    xp_ref[...] = _pack_rows(xn_hi)

    ne = N_EXPERTS
    both = lax.dot_general(wr_ref[...], xn_hi, NT_DIMS, preferred_element_type=F32)
    cross = lax.dot_general(wr_ref[0:ne, :], xn_lo, NT_DIMS, preferred_element_type=F32)
    logits = both[0:ne] + both[ne:2 * ne] + cross + br_ref[...]
    eidx = lax.broadcasted_iota(jnp.int32, (N_EXPERTS, tm), 0)
    vals, hots = [], []
    for k in range(TOP_K):
        best = jnp.max(logits, axis=0, keepdims=True)
        arg = jnp.min(jnp.where(logits == best, eidx, N_EXPERTS), axis=0, keepdims=True)
        hot = eidx == arg
        e_ref[k:k + 1, :] = arg
        vals.append(best)
        hots.append(hot)
        logits = jnp.where(hot, -jnp.inf, logits)
    exps = [jnp.exp(v - vals[0]) for v in vals]
    denom = exps[0] + exps[1] + exps[2] + exps[3]
    for k in range(TOP_K):
        w_ref[k:k + 1, :] = exps[k] / denom

    cnt = (hots[0] | hots[1] | hots[2] | hots[3]).astype(F32)
    r_i = lax.broadcasted_iota(jnp.int32, (tm, tm), 0)
    c_i = lax.broadcasted_iota(jnp.int32, (tm, tm), 1)
    before = (r_i < c_i).astype(BF16)
    prior = carry[:, 0:1] + jnp.dot(cnt.astype(BF16), before, preferred_element_type=F32)
    for k in range(TOP_K):
        r_ref[k:k + 1, :] = jnp.sum(jnp.where(hots[k], prior, 0.0), axis=0, keepdims=True).astype(jnp.int32)
    total = carry[...] + jnp.sum(cnt, axis=1, keepdims=True)
    carry[...] = total
    cnt_ref[...] = total.astype(jnp.int32)


def _route(mixed, x, w_o, g_ffn, w_router_t, b_router):
    t = x.shape[0]
    tm = min(ROUTE_TM, t)
    d = D_MODEL
    return pl.pallas_call(
        _route_body,
        grid=(t // tm,),
        in_specs=[
            pl.BlockSpec((tm, d), lambda i: (i, 0)),
            pl.BlockSpec((tm, d), lambda i: (i, 0)),
            _resident((d, d)),
            pl.BlockSpec((1, d), lambda i: (0, 0)),
            pl.BlockSpec((2 * N_EXPERTS, d), lambda i: (0, 0)),
            pl.BlockSpec((N_EXPERTS, 1), lambda i: (0, 0)),
        ],
        out_specs=[
            pl.BlockSpec((tm, d), lambda i: (i, 0)),
            pl.BlockSpec((tm, PACK_W), lambda i: (i, 0)),
            pl.BlockSpec((TOP_K, tm), lambda i: (0, i)),
            pl.BlockSpec((TOP_K, tm), lambda i: (0, i)),
            pl.BlockSpec((TOP_K, tm), lambda i: (0, i)),
            pl.BlockSpec((N_EXPERTS, 128), lambda i: (0, 0)),
        ],
        out_shape=[
            jax.ShapeDtypeStruct((t, d), F32),
            jax.ShapeDtypeStruct((t, PACK_W), jnp.uint32),
            jax.ShapeDtypeStruct((TOP_K, t), jnp.int32),
            jax.ShapeDtypeStruct((TOP_K, t), F32),
            jax.ShapeDtypeStruct((TOP_K, t), jnp.int32),
            jax.ShapeDtypeStruct((N_EXPERTS, 128), jnp.int32),
        ],
        scratch_shapes=[pltpu.VMEM((N_EXPERTS, 128), F32)],
        compiler_params=_params(("arbitrary",)),
        name="route",
    )(mixed, x, w_o, g_ffn, w_router_t, b_router)


MOE_BM = 512
DISPATCH_TM = 256


def _dispatch_body(dest_ref, padrow_ref, npad_ref, nused_ref, xp_ref, xs_hbm, zblk, sem, zsem):
    tm = xp_ref.shape[0]
    bm = zblk.shape[0]
    nb = xs_hbm.shape[0] // bm

    @pl.when(pl.program_id(0) == 0)
    def _():
        zblk[...] = jnp.zeros_like(zblk)
        npad = npad_ref[0]
        nused = nused_ref[0]

        def zero_row(j):
            return pltpu.make_async_copy(zblk.at[pl.ds(0, 1), :], xs_hbm.at[pl.ds(padrow_ref[j], 1), :], zsem)

        def zero_block(b):
            return pltpu.make_async_copy(zblk, xs_hbm.at[pl.ds(pl.multiple_of(b * bm, bm), bm), :], zsem)

        def start_row(j, c):
            zero_row(j).start()
            return c

        def wait_row(j, c):
            zero_row(j).wait()
            return c

        def start_block(b, c):
            zero_block(b).start()
            return c

        def wait_block(b, c):
            zero_block(b).wait()
            return c

        lax.fori_loop(0, npad, start_row, 0)
        lax.fori_loop(nused, nb, start_block, 0)
        lax.fori_loop(0, npad, wait_row, 0)
        lax.fori_loop(nused, nb, wait_block, 0)

    def row_copy(r, k):
        return pltpu.make_async_copy(xp_ref.at[pl.ds(r, 1), :],
                                     xs_hbm.at[pl.ds(dest_ref[0, 0, k * tm + r], 1), :], sem)

    def start(r, c):
        for k in range(TOP_K):
            row_copy(r, k).start(priority=k % 2)
        return c

    lax.fori_loop(0, tm, start, 0)
    for k in range(TOP_K):
        pltpu.make_async_copy(xp_ref, xs_hbm.at[pl.ds(0, tm), :], sem).wait()


def _dispatch(xp, dest_tiles, pad_rows, n_pad, n_used, n_rows):
    t = xp.shape[0]
    tm = dest_tiles.shape[2] // TOP_K
    return pl.pallas_call(
        _dispatch_body,
        grid=(t // tm,),
        in_specs=[
            pl.BlockSpec((1, 1, TOP_K * tm), lambda i: (i, 0, 0), memory_space=pltpu.SMEM),
            pl.BlockSpec(memory_space=pltpu.SMEM),
            pl.BlockSpec(memory_space=pltpu.SMEM),
            pl.BlockSpec(memory_space=pltpu.SMEM),
            pl.BlockSpec((tm, PACK_W), lambda i: (i, 0)),
        ],
        out_specs=pl.BlockSpec(memory_space=pltpu.HBM),
        out_shape=jax.ShapeDtypeStruct((n_rows, PACK_W), jnp.uint32),
        scratch_shapes=[
            pltpu.VMEM((MOE_BM, PACK_W), jnp.uint32),
            pltpu.SemaphoreType.DMA(()),
            pltpu.SemaphoreType.DMA(()),
        ],
        compiler_params=_params(("arbitrary",)),
        name="dispatch",
    )(dest_tiles, pad_rows, n_pad, n_used, xp)


FFN_TF = 1024
FFN_TN = 2048
FFN_CHUNK = 2048
N_SLOTS = 2
IN_SLOTS = 2
STREAM_PRIORITY = 1


def _stream_expert_blocks(first_ref, nblk_ref, nused_ref, src_hbm, dst_hbm, ibuf, obuf, isem, osem,
                          prepare, compute):
    j, e = pl.program_id(0), pl.program_id(1)
    n_in = ibuf.shape[0]
    ahead = n_in - 1
    bm_in = ibuf.shape[1]
    n_chunks, bm_out, cw = obuf.shape[1:]
    width = n_chunks * cw
    half = bm_in // 2
    col = pl.multiple_of(j * width, width)
    nblk = nblk_ref[e]
    first = first_ref[e]

    def fetch(b, slot):
        src = src_hbm.at[pl.ds(pl.multiple_of(b * bm_in, bm_in), bm_in), :]
        return pltpu.make_async_copy(src, ibuf.at[slot], isem.at[slot])

    def fetch_start(b, slot):
        for p in range(2):
            src = src_hbm.at[pl.ds(pl.multiple_of(b * bm_in + p * half, half), half), :]
            pltpu.make_async_copy(src, ibuf.at[slot, pl.ds(p * half, half), :], isem.at[slot]).start(priority=p)

    def flush(b, slot):
        out_rows = pl.ds(pl.multiple_of(b * bm_out, bm_out), bm_out)
        return [pltpu.make_async_copy(obuf.at[slot, n], dst_hbm.at[out_rows, pl.ds(col + n * cw, cw)], osem.at[slot])
                for n in range(n_chunks)]

    def flush_start(b, slot):
        for cp in flush(b, slot):
            cp.start(priority=STREAM_PRIORITY)

    def flush_wait(b, slot):
        for cp in flush(b, slot):
            cp.wait()

    for d in range(ahead):
        @pl.when(nblk > d)
        def _():
            fetch_start(first + d, d)

    @pl.when(nblk > 0)
    def _():
        prepare()

    def step(i, carry):
        slot = lax.rem(i, n_in)
        oslot = lax.rem(i, N_SLOTS)
        fetch(first + i, slot).wait()

        @pl.when(i + ahead < nblk)
        def _():
            fetch_start(first + i + ahead, lax.rem(i + ahead, n_in))

        @pl.when(i >= N_SLOTS)
        def _():
            flush_wait(first + i - N_SLOTS, oslot)

        compute(ibuf[slot], obuf.at[oslot])
        flush_start(first + i, oslot)
        return carry

    lax.fori_loop(0, nblk, step, 0)

    @pl.when(nblk >= 2)
    def _():
        flush_wait(first + nblk - 2, lax.rem(nblk, N_SLOTS))

    @pl.when(nblk >= 1)
    def _():
        flush_wait(first + nblk - 1, lax.rem(nblk - 1, N_SLOTS))

    @pl.when(e == pl.num_programs(1) - 1)
    def _():
        obuf[0] = jnp.zeros(obuf.shape[1:], obuf.dtype)

        def zero(b, carry):
            flush_start(b, 0)
            flush_wait(b, 0)
            return carry

        lax.fori_loop(nused_ref[0], dst_hbm.shape[0] // bm_out, zero, 0)


def _ffn_up_body(first_ref, nblk_ref, nused_ref, xs_hbm, wg_ref, wu_ref, bg_ref, bu_ref, h_hbm,
                 wg_bf, wu_bf, ibuf, obuf, isem, osem):
    def prepare():
        wg_bf[...] = wg_ref[...].astype(BF16)
        wu_bf[...] = wu_ref[...].astype(BF16)

    def compute(packed, out_ref):
        x = _unpack_rows(packed)
        gate = jnp.dot(x, wg_bf[...], preferred_element_type=F32) + bg_ref[...]
        up = jnp.dot(x, wu_bf[...], preferred_element_type=F32) + bu_ref[...]
        gate = jnp.minimum(gate, SWIGLU_LIMIT)
        up = jnp.clip(up, -SWIGLU_LIMIT, SWIGLU_LIMIT)
        act = ((up + 1.0) * gate * _sigmoid(SWIGLU_ALPHA * gate)).astype(BF16)
        out_ref[0] = pltpu.bitcast(act, jnp.uint32)

    _stream_expert_blocks(first_ref, nblk_ref, nused_ref, xs_hbm, h_hbm, ibuf, obuf, isem, osem,
                          prepare, compute)


def _ffn_up(first_blk, n_blk, n_used, xs, w_gate_up, b_gate_up):
    n_rows = xs.shape[0]
    tf = FFN_TF
    nf = D_FF // tf
    return pl.pallas_call(
        _ffn_up_body,
        grid_spec=pltpu.PrefetchScalarGridSpec(
            num_scalar_prefetch=3,
            grid=(nf, N_EXPERTS),
            in_specs=[
                pl.BlockSpec(memory_space=pltpu.HBM),
                pl.BlockSpec((None, D_MODEL, tf), lambda j, e, *_: (e, 0, j)),
                pl.BlockSpec((None, D_MODEL, tf), lambda j, e, *_: (e, 0, nf + j)),
                pl.BlockSpec((None, 1, tf), lambda j, e, *_: (e, 0, j)),
                pl.BlockSpec((None, 1, tf), lambda j, e, *_: (e, 0, nf + j)),
            ],
            out_specs=pl.BlockSpec(memory_space=pltpu.HBM),
            scratch_shapes=[
                pltpu.VMEM((D_MODEL, tf), BF16),
                pltpu.VMEM((D_MODEL, tf), BF16),
                pltpu.VMEM((IN_SLOTS, MOE_BM, PACK_W), jnp.uint32),
                pltpu.VMEM((N_SLOTS, 1, MOE_BM // 2, tf), jnp.uint32),
                pltpu.SemaphoreType.DMA((IN_SLOTS,)),
                pltpu.SemaphoreType.DMA((N_SLOTS,)),
            ],
        ),
        out_shape=jax.ShapeDtypeStruct((n_rows // 2, D_FF), jnp.uint32),
        compiler_params=_params(("arbitrary", "arbitrary")),
        name="ffn_up",
    )(first_blk, n_blk, n_used, xs, w_gate_up, w_gate_up, b_gate_up, b_gate_up)


def _ffn_down_body(first_ref, nblk_ref, nused_ref, h_hbm, wd_ref, bd_ref, y_hbm, wd_bf, ibuf, obuf, isem, osem):
    n_chunks, _, cw = wd_bf.shape

    def prepare():
        for n in range(n_chunks):
            wd_bf[n] = wd_ref[:, n * cw:(n + 1) * cw].astype(BF16)

    def compute(paired, out_ref):
        hid = pltpu.bitcast(paired, BF16)

        def chunk(n, carry):
            out_ref[n] = jnp.dot(hid, wd_bf[n], preferred_element_type=F32) + bd_ref[n]
            return carry

        lax.fori_loop(0, n_chunks, chunk, 0)

    _stream_expert_blocks(first_ref, nblk_ref, nused_ref, h_hbm, y_hbm, ibuf, obuf, isem, osem,
                          prepare, compute)


def _ffn_down(first_blk, n_blk, n_used, h, w_down, b_down):
    n_rows = 2 * h.shape[0]
    tn = FFN_TN
    cw = FFN_CHUNK
    nc = tn // cw
    return pl.pallas_call(
        _ffn_down_body,
        grid_spec=pltpu.PrefetchScalarGridSpec(
            num_scalar_prefetch=3,
            grid=(D_MODEL // tn, N_EXPERTS),
            in_specs=[
                pl.BlockSpec(memory_space=pltpu.HBM),
                pl.BlockSpec((None, D_FF, tn), lambda j, e, *_: (e, 0, j)),
                pl.BlockSpec((None, nc, 1, cw), lambda j, e, *_: (e, j, 0, 0)),
            ],
            out_specs=pl.BlockSpec(memory_space=pltpu.HBM),
            scratch_shapes=[
                pltpu.VMEM((nc, D_FF, cw), BF16),
                pltpu.VMEM((IN_SLOTS, MOE_BM // 2, D_FF), jnp.uint32),
                pltpu.VMEM((N_SLOTS, nc, MOE_BM, cw), F32),
                pltpu.SemaphoreType.DMA((IN_SLOTS,)),
                pltpu.SemaphoreType.DMA((N_SLOTS,)),
            ],
        ),
        out_shape=jax.ShapeDtypeStruct((n_rows, D_MODEL), F32),
        compiler_params=_params(("arbitrary", "arbitrary")),
        name="ffn_down",
    )(first_blk, n_blk, n_used, h, w_down, b_down)


COMBINE_TM = 128


def _combine_body(dest_ref, next_ref, h_ref, w_ref, g_ref, y_hbm, o_ref, gbuf, sem):
    i = pl.program_id(0)
    tm = h_ref.shape[0]

    def gather_tile(table_ref, slot):
        def start(r, c):
            for k in range(TOP_K):
                pltpu.make_async_copy(y_hbm.at[pl.ds(table_ref[0, 0, k * tm + r], 1), :],
                                      gbuf.at[slot, k, pl.ds(r, 1), :], sem.at[slot]).start(priority=k % 2)
            return c

        lax.fori_loop(0, tm, start, 0)

    slot = lax.rem(i, N_SLOTS)

    @pl.when(i == 0)
    def _():
        gather_tile(dest_ref, 0)

    @pl.when(i + 1 < pl.num_programs(0))
    def _():
        gather_tile(next_ref, 1 - slot)

    for k in range(TOP_K):
        pltpu.make_async_copy(y_hbm.at[pl.ds(0, tm), :], gbuf.at[slot, k], sem.at[slot]).wait()
    h = h_ref[...]
    for k in range(TOP_K):
        h = h + w_ref[:, k:k + 1] * gbuf[slot, k]
    o_ref[...] = h * lax.rsqrt(jnp.mean(h * h, axis=-1, keepdims=True) + EPS) * g_ref[...]


def _combine(dest_tiles, h1, w_cols, g_final, y):
    t = h1.shape[0]
    tm = dest_tiles.shape[2] // TOP_K
    d = D_MODEL
    last = t // tm - 1
    return pl.pallas_call(
        _combine_body,
        grid=(t // tm,),
        in_specs=[
            pl.BlockSpec((1, 1, TOP_K * tm), lambda i: (i, 0, 0), memory_space=pltpu.SMEM),
            pl.BlockSpec((1, 1, TOP_K * tm), lambda i: (jnp.minimum(i + 1, last), 0, 0), memory_space=pltpu.SMEM),
            pl.BlockSpec((tm, d), lambda i: (i, 0)),
            pl.BlockSpec((tm, TOP_K), lambda i: (i, 0)),
            pl.BlockSpec((1, d), lambda i: (0, 0)),
            pl.BlockSpec(memory_space=pltpu.HBM),
        ],
        out_specs=pl.BlockSpec((tm, d), lambda i: (i, 0)),
        out_shape=jax.ShapeDtypeStruct((t, d), F32),
        scratch_shapes=[pltpu.VMEM((N_SLOTS, TOP_K, tm, d), F32), pltpu.SemaphoreType.DMA((N_SLOTS,))],
        compiler_params=_params(("arbitrary",)),
        name="combine",
    )(dest_tiles, dest_tiles, h1, w_cols, g_final, y)


def _tile_major(a, tm):
    k, t = a.shape
    return a.reshape(k, t // tm, tm).transpose(1, 0, 2).reshape(t // tm, 1, k * tm)


def _routing_tables(top_e, rank, counts, t):
    bm = MOE_BM
    nb = (t * TOP_K) // bm + N_EXPERTS
    padded = (counts + bm - 1) // bm * bm
    pad_end = jnp.cumsum(padded)
    pad_start = pad_end - padded
    onehot = top_e[:, :, None] == jnp.arange(N_EXPERTS, dtype=jnp.int32)
    dest = rank + jnp.sum(jnp.where(onehot, pad_start, 0), axis=-1)
    first_blk = (pad_start // bm).astype(jnp.int32)
    n_blk = (padded // bm).astype(jnp.int32)
    n_used = (pad_end[-1] // bm).astype(jnp.int32).reshape(1)
    gap = padded - counts
    gap_end = jnp.cumsum(gap)
    j = jnp.arange(N_EXPERTS * bm, dtype=jnp.int32)
    ej = jnp.minimum(jnp.sum(j[:, None] >= gap_end[None, :], axis=1), N_EXPERTS - 1)
    pad_rows = (pad_start + counts)[ej] + j - (gap_end - gap)[ej]
    pad_rows = jnp.clip(pad_rows, 0, nb * bm - 1).astype(jnp.int32)
    n_pad = gap_end[-1].astype(jnp.int32).reshape(1)
    return dest.astype(jnp.int32), first_blk, n_blk, n_used, pad_rows, n_pad, nb * bm


def kernel(x, g_mix, w_in, ssm_conv_w, ssm_conv_b, ssm_dt_bias, ssm_a_log, ssm_d, ssm_norm_g, w_ssm_out,
           sc_conv_w, w_sc_out, b_gate, w_o, g_ffn, w_router, b_router, w_gate_up, b_gate_up, w_down,
           b_down, g_final):
    bsz, seq, d = x.shape
    t = bsz * seq
    assert bsz == 1 and d == D_MODEL and w_in.shape[0] == 1
    xt = x.reshape(t, d)
    w_dt = w_in[0, :, OFF_DT:OFF_SC]
    w_dt_hi = w_dt.astype(BF16)
    w_dt = jnp.concatenate([w_dt_hi, (w_dt - w_dt_hi.astype(F32)).astype(BF16)], axis=1)
    col = lambda a: a.reshape(-1, 1)
    row = lambda a: a.reshape(1, -1)

    u, dt_raw = _prenorm(xt, row(g_mix[0]), w_dt)
    proj = _inproj(u, w_in[0].T)
    y_norm = _ssd(proj, dt_raw.T, ssm_conv_w[0], row(ssm_conv_b[0]), col(ssm_dt_bias[0]), col(ssm_a_log[0]),
                  col(ssm_d[0]), row(ssm_norm_g[0]))
    mixed = _mix(y_norm, proj, row(b_gate[0]), sc_conv_w[0], w_ssm_out[0].astype(BF16), w_sc_out[0].astype(BF16))
    wr_t = w_router[0].T
    wr_hi = wr_t.astype(BF16)
    wr_lo = (wr_t - wr_hi.astype(F32)).astype(BF16)
    h1, xp, top_e, top_w, rank, counts = _route(mixed, xt, w_o[0].astype(BF16), row(g_ffn[0]),
                                                jnp.concatenate([wr_hi, wr_lo], axis=0), col(b_router[0]))
    dest, first_blk, n_blk, n_used, pad_rows, n_pad, n_rows = _routing_tables(top_e, rank, counts[:, 0], t)
    xs = _dispatch(xp, _tile_major(dest, min(DISPATCH_TM, t)), pad_rows, n_pad, n_used, n_rows)
    hid = _ffn_up(first_blk, n_blk, n_used, xs, w_gate_up[0], b_gate_up[0].reshape(N_EXPERTS, 1, 2 * D_FF))
    y = _ffn_down(first_blk, n_blk, n_used, hid, w_down[0],
                  b_down[0].reshape(N_EXPERTS, D_MODEL // FFN_CHUNK, 1, FFN_CHUNK))
    out = _combine(_tile_major(dest, min(COMBINE_TM, t)), h1, top_w.T, row(g_final), y)
    return out.reshape(bsz, seq, d)
```

```python
import functools

import jax
import jax.numpy as jnp
from jax import lax
from jax.experimental import pallas as pl
from jax.experimental.pallas import tpu as pltpu

D_MODEL = 2048
SSM_D_INNER = 2 * D_MODEL
SSM_HEAD_DIM = 64
SSM_N_HEADS = SSM_D_INNER // SSM_HEAD_DIM
SSM_N_GROUPS = 8
SSM_HEADS_PER_GROUP = SSM_N_HEADS // SSM_N_GROUPS
SSM_D_STATE = 128
SSM_CONV = 4
SSM_GN = SSM_N_GROUPS * SSM_D_STATE
SSM_CONV_DIM = SSM_D_INNER + 2 * SSM_GN
SSM_GROUP_CH = SSM_D_INNER // SSM_N_GROUPS
SC_DIM = D_MODEL
SC_WIDTH = 3
N_EXPERTS = 32
TOP_K = 4
D_FF = D_MODEL
SWIGLU_LIMIT = 7.0
SWIGLU_ALPHA = 1.702
EPS = 1e-5

OFF_Z = 0
OFF_XBC = OFF_Z + SSM_D_INNER
OFF_DT = OFF_XBC + SSM_CONV_DIM
OFF_SC = OFF_DT + SSM_N_HEADS
OFF_GATE = OFF_SC + 3 * SC_DIM
D_IN_PROJ = OFF_GATE + 2 * D_MODEL

P_Z = 0
P_XBC = P_Z + SSM_D_INNER
P_SC = P_XBC + SSM_CONV_DIM
P_GATE = P_SC + 3 * SC_DIM
P_TOTAL = P_GATE + 2 * D_MODEL

SUBLANES = 8
VMEM_LIMIT = 56 * 1024 * 1024

F32 = jnp.float32
BF16 = jnp.bfloat16
HIGHEST = lax.Precision.HIGHEST
NT_DIMS = (((1,), (1,)), ((), ()))
TN_DIMS = (((0,), (0,)), ((), ()))


def _sigmoid(v):
    return 0.5 * jnp.tanh(0.5 * v) + 0.5


def _params(semantics):
    return pltpu.CompilerParams(dimension_semantics=semantics, vmem_limit_bytes=VMEM_LIMIT)


PRENORM_TM = 512
INPROJ_TM = 1024
INPROJ_TN = 1024
LANES = 128
DT_SHIFT = OFF_SC - OFF_DT
ALIGNED_TILES = OFF_DT // INPROJ_TN
CAST_ROWS = 256
EPILOGUE_ROWS = 256


def _prenorm_body(x_ref, g_ref, wdt_ref, u_ref, dt_ref):
    x = x_ref[...]
    u = x * lax.rsqrt(jnp.mean(x * x, axis=-1, keepdims=True) + EPS) * g_ref[...]
    u_hi = u.astype(BF16)
    u_ref[...] = u_hi
    u_lo = (u - u_hi.astype(F32)).astype(BF16)
    both = jnp.dot(u_hi, wdt_ref[...], preferred_element_type=F32)
    cross = jnp.dot(u_lo, wdt_ref[...], preferred_element_type=F32)
    nh = SSM_N_HEADS
    dt_ref[...] = both[:, :nh] + both[:, nh:] + cross[:, :nh]


def _prenorm(x, g, w_dt):
    t = x.shape[0]
    tm = min(PRENORM_TM, t)
    return pl.pallas_call(
        _prenorm_body,
        grid=(t // tm,),
        in_specs=[
            pl.BlockSpec((tm, D_MODEL), lambda i: (i, 0)),
            pl.BlockSpec((1, D_MODEL), lambda i: (0, 0)),
            pl.BlockSpec((D_MODEL, 2 * SSM_N_HEADS), lambda i: (0, 0)),
        ],
        out_specs=[
            pl.BlockSpec((tm, D_MODEL), lambda i: (i, 0)),
            pl.BlockSpec((tm, SSM_N_HEADS), lambda i: (i, 0)),
        ],
        out_shape=[
            jax.ShapeDtypeStruct((t, D_MODEL), BF16),
            jax.ShapeDtypeStruct((t, SSM_N_HEADS), F32),
        ],
        compiler_params=_params(("parallel",)),
        name="prenorm",
    )(x, g, w_dt)


def _inproj_body(u_ref, wt_ref, proj_ref, w_bf):
    j, i = pl.program_id(0), pl.program_id(1)
    tm = u_ref.shape[0]
    z_tiles = P_XBC // w_bf.shape[0]

    @pl.when(i == 0)
    def _():
        for r in range(0, w_bf.shape[0], CAST_ROWS):
            w_bf[r:r + CAST_ROWS, :] = wt_ref[r:r + CAST_ROWS, :].astype(BF16)

    def project(r0, rows):
        return lax.dot_general(u_ref[r0:r0 + rows, :], w_bf[...], NT_DIMS, preferred_element_type=F32)

    chunk = min(EPILOGUE_ROWS, tm)

    @pl.when(j < z_tiles)
    def _():
        for r0 in range(0, tm, chunk):
            acc = project(r0, chunk)
            proj_ref[r0:r0 + chunk, :] = (acc * _sigmoid(acc)).astype(BF16)

    @pl.when(j >= z_tiles)
    def _():
        proj_ref[...] = project(0, tm).astype(BF16)


def _inproj(u, w_in_t):
    t = u.shape[0]
    tm = min(INPROJ_TM, t)
    tn = INPROJ_TN
    assert OFF_DT % tn == 0 and DT_SHIFT % SUBLANES == 0 and P_XBC % tn == 0
    first_row = lambda j: pl.multiple_of(j * tn + jnp.where(j >= ALIGNED_TILES, DT_SHIFT, 0), SUBLANES)
    return pl.pallas_call(
        _inproj_body,
        grid=(P_TOTAL // tn, t // tm),
        in_specs=[
            pl.BlockSpec((tm, D_MODEL), lambda j, i: (i, 0)),
            pl.BlockSpec((pl.Element(tn), pl.Element(D_MODEL)), lambda j, i: (first_row(j), 0)),
        ],
        out_specs=pl.BlockSpec((tm, tn), lambda j, i: (i, j)),
        out_shape=jax.ShapeDtypeStruct((t, P_TOTAL), BF16),
        scratch_shapes=[pltpu.VMEM((tn, D_MODEL), BF16)],
        compiler_params=_params(("arbitrary", "arbitrary")),
        name="inproj",
    )(u, w_in_t)


SSD_L = 256
SSD_GROUPS = 2
HEAD_PAIR = 2 * SSM_HEAD_DIM


def _ssd_body(z_ref, x_ref, b_ref, c_ref, dt_ref, widen_ref, wx_ref, wb_ref, wc_ref, bx_ref, bb_ref, bc_ref,
              dtb_ref, alog_ref, d_ref, ng_ref, o_ref, s_ref, xbuf, bbuf, cbuf):
    L = x_ref.shape[0]
    tail = SUBLANES

    @pl.when(pl.program_id(1) == 0)
    def _():
        s_ref[...] = jnp.zeros_like(s_ref)
        xbuf[0:tail, :] = jnp.zeros((tail, xbuf.shape[1]), F32)
        bbuf[0:tail, :] = jnp.zeros((tail, bbuf.shape[1]), F32)
        cbuf[0:tail, :] = jnp.zeros((tail, cbuf.shape[1]), F32)

    def conv_silu(buf, in_ref, w_ref, bias_ref):
        buf[tail:tail + L, :] = in_ref[...].astype(F32)
        acc = bias_ref[...] + w_ref[SSM_CONV - 1:SSM_CONV, :] * buf[tail:tail + L, :]
        for k in range(SSM_CONV - 1):
            off = tail - (SSM_CONV - 1) + k
            acc = acc + w_ref[k:k + 1, :] * buf[off:off + L, :]
        buf[0:tail, :] = buf[L:L + tail, :]
        return acc * _sigmoid(acc)

    xs_all = conv_silu(xbuf, x_ref, wx_ref, bx_ref)
    bm_all = conv_silu(bbuf, b_ref, wb_ref, bb_ref).astype(BF16)
    cm_all = conv_silu(cbuf, c_ref, wc_ref, bc_ref).astype(BF16)

    dt_raw = dt_ref[...] + dtb_ref[...]
    dt_all = jnp.maximum(dt_raw, 0.0) + jnp.log(1.0 + jnp.exp(-jnp.abs(dt_raw)))
    da_all = dt_all * (-jnp.exp(alog_ref[...]))
    hg = SSM_HEADS_PER_GROUP
    gc = SSM_GROUP_CH
    n = SSM_D_STATE
    row = lax.broadcasted_iota(jnp.int32, (L, L), 0)
    col = lax.broadcasted_iota(jnp.int32, (L, L), 1)
    causal = row >= col
    incl = (row <= col).astype(BF16)
    lane = lax.broadcasted_iota(jnp.int32, (L, HEAD_PAIR), 1)
    first = lane < SSM_HEAD_DIM

    def pieces(a):
        hi = a.astype(BF16).astype(F32)
        mid = (a - hi).astype(BF16).astype(F32)
        lo = (a - hi - mid).astype(BF16).astype(F32)
        return hi, mid, lo

    for gi in range(x_ref.shape[1] // gc):
        ch = slice(gi * gc, (gi + 1) * gc)
        st = slice(gi * n, (gi + 1) * n)
        hd = slice(gi * hg, (gi + 1) * hg)
        xs, bm, cm, dt = xs_all[:, ch], bm_all[:, st], cm_all[:, st], dt_all[hd, :]
        da_p = pieces(da_all[hd, :])
        parts = jnp.dot(jnp.concatenate(da_p, axis=0).astype(BF16), incl, preferred_element_type=F32)
        cs = parts[0:hg] + parts[hg:2 * hg] + parts[2 * hg:3 * hg]
        cs_end = cs[:, L - 1:L]
        split = [pieces(a) for a in (dt, jnp.exp(cs_end - cs), jnp.exp(cs))]
        stacked = jnp.concatenate([split[q][s] for s in range(3) for q in range(3)] + [cs], axis=0)
        flipped = stacked.T
        cs_t = flipped[:, 9 * hg:10 * hg]
        wide = jnp.dot(flipped.astype(BF16), widen_ref[...], preferred_element_type=F32)
        dt_x, to_end_x, ecs_x = wide[:, 0:gc], wide[:, gc:2 * gc], wide[:, 2 * gc:3 * gc]

        cb = lax.dot_general(cm, bm, NT_DIMS, preferred_element_type=F32)
        cb = jnp.where(causal, cb, 0.0)
        y_off = jnp.dot(cm, s_ref[gi].astype(BF16), preferred_element_type=F32)

        xdt = xs * dt_x
        xdt_b = xdt.astype(BF16)
        ys = []
        for p in range(hg // 2):
            sl = slice(p * HEAD_PAIR, (p + 1) * HEAD_PAIR)
            yd = []
            for h in (2 * p, 2 * p + 1):
                seg = cs_t[:, h:h + 1] - cs[h:h + 1, :]
                m = cb * jnp.exp(jnp.minimum(seg, 0.0))
                yd.append(jnp.dot(m.astype(BF16), xdt_b[:, sl], preferred_element_type=F32))
            ys.append(jnp.where(first, yd[0], yd[1]))
        y = jnp.concatenate(ys, axis=1) + y_off * ecs_x + xs * d_ref[:, ch]

        xw = (xdt * to_end_x).astype(BF16)
        upd = lax.dot_general(bm, xw, TN_DIMS, preferred_element_type=F32)
        s_ref[gi] = s_ref[gi] * ecs_x[L - 1:L, :] + upd

        v = y * z_ref[:, ch].astype(F32)
        v = v * lax.rsqrt(jnp.mean(v * v, axis=-1, keepdims=True) + EPS)
        o_ref[:, ch] = (v * ng_ref[:, ch]).astype(BF16)


def _ssd(proj, dt_rows, conv_w, conv_b, dt_bias, a_log, d_skip, norm_g):
    t = proj.shape[0]
    L = min(SSD_L, t)
    gc = SSM_GROUP_CH
    n = SSM_D_STATE
    zc, xc = P_Z // gc, P_XBC // gc
    bc, cc = (P_XBC + SSM_D_INNER) // n, (P_XBC + SSM_D_INNER + SSM_GN) // n
    wbc, wcc = SSM_D_INNER // n, (SSM_D_INNER + SSM_GN) // n
    hg = SSM_HEADS_PER_GROUP
    n_factors, n_pieces = 3, 3
    r = jnp.arange((n_factors * n_pieces + 1) * hg)[:, None]
    c = jnp.arange(n_factors * gc)[None, :]
    widen = ((r < n_factors * n_pieces * hg)
             & (r % (n_factors * hg) == (c // gc) * hg + (c % gc) // SSM_HEAD_DIM)).astype(BF16)
    d_wide = jnp.repeat(d_skip.reshape(-1), SSM_HEAD_DIM).reshape(1, -1)
    k = SSD_GROUPS
    gc, n, hg = k * gc, k * n, k * hg
    zc, xc, bc, cc, wbc, wcc = zc // k, xc // k, bc // k, cc // k, wbc // k, wcc // k
    return pl.pallas_call(
        _ssd_body,
        grid=(SSM_N_GROUPS // k, t // L),
        in_specs=[
            pl.BlockSpec((L, gc), lambda g, i: (i, zc + g)),
            pl.BlockSpec((L, gc), lambda g, i: (i, xc + g)),
            pl.BlockSpec((L, n), lambda g, i: (i, bc + g)),
            pl.BlockSpec((L, n), lambda g, i: (i, cc + g)),
            pl.BlockSpec((hg, L), lambda g, i: (g, i)),
            pl.BlockSpec(widen.shape, lambda g, i: (0, 0)),
            pl.BlockSpec((SSM_CONV, gc), lambda g, i: (0, g)),
            pl.BlockSpec((SSM_CONV, n), lambda g, i: (0, wbc + g)),
            pl.BlockSpec((SSM_CONV, n), lambda g, i: (0, wcc + g)),
            pl.BlockSpec((1, gc), lambda g, i: (0, g)),
            pl.BlockSpec((1, n), lambda g, i: (0, wbc + g)),
            pl.BlockSpec((1, n), lambda g, i: (0, wcc + g)),
            pl.BlockSpec((hg, 1), lambda g, i: (g, 0)),
            pl.BlockSpec((hg, 1), lambda g, i: (g, 0)),
            pl.BlockSpec((1, gc), lambda g, i: (0, g)),
            pl.BlockSpec((1, gc), lambda g, i: (0, g)),
        ],
        out_specs=pl.BlockSpec((L, gc), lambda g, i: (i, g)),
        out_shape=jax.ShapeDtypeStruct((t, SSM_D_INNER), BF16),
        scratch_shapes=[
            pltpu.VMEM((k, SSM_D_STATE, SSM_GROUP_CH), F32),
            pltpu.VMEM((L + SUBLANES, gc), F32),
            pltpu.VMEM((L + SUBLANES, n), F32),
            pltpu.VMEM((L + SUBLANES, n), F32),
        ],
        compiler_params=_params(("parallel", "arbitrary")),
        name="ssd",
    )(proj, proj, proj, proj, dt_rows, widen, conv_w, conv_w, conv_w, conv_b, conv_b, conv_b,
      dt_bias, a_log, d_wide, norm_g)


MIX_TM = 256


def _mix_body(yn_ref, b_ref, c_ref, v_ref, cp_ref, vp_ref, g1_ref, g2_ref, bg1_ref, bg2_ref,
              wc_ref, wssm_ref, wsc_ref, o_ref, buf):
    tm = yn_ref.shape[0]
    tail = SUBLANES
    prev = cp_ref[...].astype(F32) * vp_ref[...].astype(F32)
    buf[0:tail, :] = jnp.where(pl.program_id(0) == 0, 0.0, prev)
    cv = c_ref[...].astype(F32) * v_ref[...].astype(F32)
    buf[tail:tail + tm, :] = cv
    conv = wc_ref[SC_WIDTH - 1:SC_WIDTH, :] * cv
    for k in range(SC_WIDTH - 1):
        off = tail - (SC_WIDTH - 1) + k
        conv = conv + wc_ref[k:k + 1, :] * buf[off:off + tm, :]
    sc_in = (b_ref[...].astype(F32) * conv).astype(BF16)
    y_sc = jnp.dot(sc_in, wsc_ref[...], preferred_element_type=F32)
    y_ssm = jnp.dot(yn_ref[...], wssm_ref[...], preferred_element_type=F32)
    g1 = _sigmoid(g1_ref[...].astype(F32) + bg1_ref[...])
    g2 = _sigmoid(g2_ref[...].astype(F32) + bg2_ref[...])
    o_ref[...] = (g1 * y_ssm + g2 * y_sc).astype(BF16)


def _resident(shape):
    return pl.BlockSpec(shape, lambda *_: (0,) * len(shape), pipeline_mode=pl.Buffered(1))


def _mix(y_norm, proj, b_gate, sc_conv_w, w_ssm_out, w_sc_out):
    t = y_norm.shape[0]
    tm = min(MIX_TM, t)
    d = D_MODEL
    sb, gb = P_SC // d, P_GATE // d
    prev_rows = lambda i: jnp.maximum(i * (tm // SUBLANES) - 1, 0)
    return pl.pallas_call(
        _mix_body,
        grid=(t // tm,),
        in_specs=[
            pl.BlockSpec((tm, SSM_D_INNER), lambda i: (i, 0)),
            pl.BlockSpec((tm, d), lambda i: (i, sb)),
            pl.BlockSpec((tm, d), lambda i: (i, sb + 1)),
            pl.BlockSpec((tm, d), lambda i: (i, sb + 2)),
            pl.BlockSpec((SUBLANES, d), lambda i: (prev_rows(i), sb + 1)),
            pl.BlockSpec((SUBLANES, d), lambda i: (prev_rows(i), sb + 2)),
            pl.BlockSpec((tm, d), lambda i: (i, gb)),
            pl.BlockSpec((tm, d), lambda i: (i, gb + 1)),
            pl.BlockSpec((1, d), lambda i: (0, 0)),
            pl.BlockSpec((1, d), lambda i: (0, 1)),
            pl.BlockSpec((SC_WIDTH, d), lambda i: (0, 0)),
            _resident((SSM_D_INNER, d)),
            _resident((d, d)),
        ],
        out_specs=pl.BlockSpec((tm, d), lambda i: (i, 0)),
        out_shape=jax.ShapeDtypeStruct((t, d), BF16),
        scratch_shapes=[pltpu.VMEM((tm + SUBLANES, d), F32)],
        compiler_params=_params(("parallel",)),
        name="mix",
    )(y_norm, proj, proj, proj, proj, proj, proj, proj, b_gate, b_gate, sc_conv_w, w_ssm_out, w_sc_out)


ROUTE_TM = 512
PACK_W = D_MODEL // 2


def _pack_rows(v):
    lo = lax.bitcast_convert_type(v[:, :PACK_W].astype(F32), jnp.uint32)
    hi = lax.bitcast_convert_type(v[:, PACK_W:].astype(F32), jnp.uint32)
    return hi | (lo >> 16)


def _unpack_rows(w):
    lo = lax.bitcast_convert_type(w << 16, F32).astype(BF16)
    hi = lax.bitcast_convert_type(w & jnp.uint32(0xFFFF0000), F32).astype(BF16)
    return jnp.concatenate([lo, hi], axis=1)


def _route_body(m_ref, x_ref, wo_ref, g_ref, wr_ref, br_ref,
                h_ref, xp_ref, e_ref, w_ref, r_ref, cnt_ref, carry):
    tm = m_ref.shape[0]

    @pl.when(pl.program_id(0) == 0)
    def _():
        carry[...] = jnp.zeros_like(carry)

    h = x_ref[...] + jnp.dot(m_ref[...], wo_ref[...], preferred_element_type=F32)
    h_ref[...] = h
    xn = h * lax.rsqrt(jnp.mean(h * h, axis=-1, keepdims=True) + EPS) * g_ref[...]
    xn_hi = xn.astype(BF16)
    xn_lo = (xn - xn_hi.astype(F32)).astype(BF16)
    xp_ref[...] = _pack_rows(xn_hi)

    ne = N_EXPERTS
    both = lax.dot_general(wr_ref[...], xn_hi, NT_DIMS, preferred_element_type=F32)
    cross = lax.dot_general(wr_ref[0:ne, :], xn_lo, NT_DIMS, preferred_element_type=F32)
    logits = both[0:ne] + both[ne:2 * ne] + cross + br_ref[...]
    eidx = lax.broadcasted_iota(jnp.int32, (N_EXPERTS, tm), 0)
    vals, hots = [], []
    for k in range(TOP_K):
        best = jnp.max(logits, axis=0, keepdims=True)
        arg = jnp.min(jnp.where(logits == best, eidx, N_EXPERTS), axis=0, keepdims=True)
        hot = eidx == arg
        e_ref[k:k + 1, :] = arg
        vals.append(best)
        hots.append(hot)
        logits = jnp.where(hot, -jnp.inf, logits)
    exps = [jnp.exp(v - vals[0]) for v in vals]
    denom = exps[0] + exps[1] + exps[2] + exps[3]
    for k in range(TOP_K):
        w_ref[k:k + 1, :] = exps[k] / denom

    cnt = (hots[0] | hots[1] | hots[2] | hots[3]).astype(F32)
    r_i = lax.broadcasted_iota(jnp.int32, (tm, tm), 0)
    c_i = lax.broadcasted_iota(jnp.int32, (tm, tm), 1)
    before = (r_i < c_i).astype(BF16)
    prior = carry[:, 0:1] + jnp.dot(cnt.astype(BF16), before, preferred_element_type=F32)
    for k in range(TOP_K):
        r_ref[k:k + 1, :] = jnp.sum(jnp.where(hots[k], prior, 0.0), axis=0, keepdims=True).astype(jnp.int32)
    total = carry[...] + jnp.sum(cnt, axis=1, keepdims=True)
    carry[...] = total
    cnt_ref[...] = total.astype(jnp.int32)


def _route(mixed, x, w_o, g_ffn, w_router_t, b_router):
    t = x.shape[0]
    tm = min(ROUTE_TM, t)
    d = D_MODEL
    return pl.pallas_call(
        _route_body,
        grid=(t // tm,),
        in_specs=[
            pl.BlockSpec((tm, d), lambda i: (i, 0)),
            pl.BlockSpec((tm, d), lambda i: (i, 0)),
            _resident((d, d)),
            pl.BlockSpec((1, d), lambda i: (0, 0)),
            pl.BlockSpec((2 * N_EXPERTS, d), lambda i: (0, 0)),
            pl.BlockSpec((N_EXPERTS, 1), lambda i: (0, 0)),
        ],
        out_specs=[
            pl.BlockSpec((tm, d), lambda i: (i, 0)),
            pl.BlockSpec((tm, PACK_W), lambda i: (i, 0)),
            pl.BlockSpec((TOP_K, tm), lambda i: (0, i)),
            pl.BlockSpec((TOP_K, tm), lambda i: (0, i)),
            pl.BlockSpec((TOP_K, tm), lambda i: (0, i)),
            pl.BlockSpec((N_EXPERTS, 128), lambda i: (0, 0)),
        ],
        out_shape=[
            jax.ShapeDtypeStruct((t, d), F32),
            jax.ShapeDtypeStruct((t, PACK_W), jnp.uint32),
            jax.ShapeDtypeStruct((TOP_K, t), jnp.int32),
            jax.ShapeDtypeStruct((TOP_K, t), F32),
            jax.ShapeDtypeStruct((TOP_K, t), jnp.int32),
            jax.ShapeDtypeStruct((N_EXPERTS, 128), jnp.int32),
        ],
        scratch_shapes=[pltpu.VMEM((N_EXPERTS, 128), F32)],
        compiler_params=_params(("arbitrary",)),
        name="route",
    )(mixed, x, w_o, g_ffn, w_router_t, b_router)


MOE_BM = 256
DISPATCH_TM = 256


def _dispatch_body(dest_ref, padrow_ref, npad_ref, nused_ref, xp_ref, xs_hbm, zblk, sem, zsem):
    tm = xp_ref.shape[0]
    bm = zblk.shape[0]
    nb = xs_hbm.shape[0] // bm

    @pl.when(pl.program_id(0) == 0)
    def _():
        zblk[...] = jnp.zeros_like(zblk)
        npad = npad_ref[0]
        nused = nused_ref[0]

        def zero_row(j):
            return pltpu.make_async_copy(zblk.at[pl.ds(0, 1), :], xs_hbm.at[pl.ds(padrow_ref[j], 1), :], zsem)

        def zero_block(b):
            return pltpu.make_async_copy(zblk, xs_hbm.at[pl.ds(pl.multiple_of(b * bm, bm), bm), :], zsem)

        def start_row(j, c):
            zero_row(j).start()
            return c

        def wait_row(j, c):
            zero_row(j).wait()
            return c

        def start_block(b, c):
            zero_block(b).start()
            return c

        def wait_block(b, c):
            zero_block(b).wait()
            return c

        lax.fori_loop(0, npad, start_row, 0)
        lax.fori_loop(nused, nb, start_block, 0)
        lax.fori_loop(0, npad, wait_row, 0)
        lax.fori_loop(nused, nb, wait_block, 0)

    def row_copy(r, k):
        return pltpu.make_async_copy(xp_ref.at[pl.ds(r, 1), :],
                                     xs_hbm.at[pl.ds(dest_ref[0, 0, k * tm + r], 1), :], sem)

    def start(r, c):
        for k in range(TOP_K):
            row_copy(r, k).start(priority=k % 2)
        return c

    lax.fori_loop(0, tm, start, 0)
    for k in range(TOP_K):
        pltpu.make_async_copy(xp_ref, xs_hbm.at[pl.ds(0, tm), :], sem).wait()


def _dispatch(xp, dest_tiles, pad_rows, n_pad, n_used, n_rows):
    t = xp.shape[0]
    tm = dest_tiles.shape[2] // TOP_K
    return pl.pallas_call(
        _dispatch_body,
        grid=(t // tm,),
        in_specs=[
            pl.BlockSpec((1, 1, TOP_K * tm), lambda i: (i, 0, 0), memory_space=pltpu.SMEM),
            pl.BlockSpec(memory_space=pltpu.SMEM),
            pl.BlockSpec(memory_space=pltpu.SMEM),
            pl.BlockSpec(memory_space=pltpu.SMEM),
            pl.BlockSpec((tm, PACK_W), lambda i: (i, 0)),
        ],
        out_specs=pl.BlockSpec(memory_space=pltpu.HBM),
        out_shape=jax.ShapeDtypeStruct((n_rows, PACK_W), jnp.uint32),
        scratch_shapes=[
            pltpu.VMEM((MOE_BM, PACK_W), jnp.uint32),
            pltpu.SemaphoreType.DMA(()),
            pltpu.SemaphoreType.DMA(()),
        ],
        compiler_params=_params(("arbitrary",)),
        name="dispatch",
    )(dest_tiles, pad_rows, n_pad, n_used, xp)


FFN_TF = 1024
FFN_TN = 2048
FFN_CHUNK = 2048
N_SLOTS = 2
IN_SLOTS = 3
STREAM_PRIORITY = 1


def _stream_expert_blocks(first_ref, nblk_ref, nused_ref, src_hbm, dst_hbm, ibuf, obuf, isem, osem,
                          prepare, compute):
    j, e = pl.program_id(0), pl.program_id(1)
    n_in = ibuf.shape[0]
    ahead = n_in - 1
    bm_in = ibuf.shape[1]
    n_chunks, bm_out, cw = obuf.shape[1:]
    width = n_chunks * cw
    half = bm_in // 2
    col = pl.multiple_of(j * width, width)
    nblk = nblk_ref[e]
    first = first_ref[e]

    def fetch(b, slot):
        src = src_hbm.at[pl.ds(pl.multiple_of(b * bm_in, bm_in), bm_in), :]
        return pltpu.make_async_copy(src, ibuf.at[slot], isem.at[slot])

    def fetch_start(b, slot):
        for p in range(2):
            src = src_hbm.at[pl.ds(pl.multiple_of(b * bm_in + p * half, half), half), :]
            pltpu.make_async_copy(src, ibuf.at[slot, pl.ds(p * half, half), :], isem.at[slot]).start(priority=p)

    def flush(b, slot):
        out_rows = pl.ds(pl.multiple_of(b * bm_out, bm_out), bm_out)
        return [pltpu.make_async_copy(obuf.at[slot, n], dst_hbm.at[out_rows, pl.ds(col + n * cw, cw)], osem.at[slot])
                for n in range(n_chunks)]

    def flush_start(b, slot):
        for cp in flush(b, slot):
            cp.start(priority=STREAM_PRIORITY)

    def flush_wait(b, slot):
        for cp in flush(b, slot):
            cp.wait()

    for d in range(ahead):
        @pl.when(nblk > d)
        def _():
            fetch_start(first + d, d)

    @pl.when(nblk > 0)
    def _():
        prepare()

    def step(i, carry):
        slot = lax.rem(i, n_in)
        oslot = lax.rem(i, N_SLOTS)
        fetch(first + i, slot).wait()

        @pl.when(i + ahead < nblk)
        def _():
            fetch_start(first + i + ahead, lax.rem(i + ahead, n_in))

        @pl.when(i >= N_SLOTS)
        def _():
            flush_wait(first + i - N_SLOTS, oslot)

        compute(ibuf[slot], obuf.at[oslot])
        flush_start(first + i, oslot)
        return carry

    lax.fori_loop(0, nblk, step, 0)

    @pl.when(nblk >= 2)
    def _():
        flush_wait(first + nblk - 2, lax.rem(nblk, N_SLOTS))

    @pl.when(nblk >= 1)
    def _():
        flush_wait(first + nblk - 1, lax.rem(nblk - 1, N_SLOTS))

    @pl.when(e == pl.num_programs(1) - 1)
    def _():
        obuf[0] = jnp.zeros(obuf.shape[1:], obuf.dtype)

        def zero(b, carry):
            flush_start(b, 0)
            flush_wait(b, 0)
            return carry

        lax.fori_loop(nused_ref[0], dst_hbm.shape[0] // bm_out, zero, 0)


def _ffn_up_body(first_ref, nblk_ref, nused_ref, xs_hbm, wg_ref, wu_ref, bg_ref, bu_ref, h_hbm,
                 wg_bf, wu_bf, ibuf, obuf, isem, osem):
    def prepare():
        wg_bf[...] = wg_ref[...].astype(BF16)
        wu_bf[...] = wu_ref[...].astype(BF16)

    def compute(packed, out_ref):
        x = _unpack_rows(packed)
        gate = jnp.dot(x, wg_bf[...], preferred_element_type=F32) + bg_ref[...]
        up = jnp.dot(x, wu_bf[...], preferred_element_type=F32) + bu_ref[...]
        gate = jnp.minimum(gate, SWIGLU_LIMIT)
        up = jnp.clip(up, -SWIGLU_LIMIT, SWIGLU_LIMIT)
        act = ((up + 1.0) * gate * _sigmoid(SWIGLU_ALPHA * gate)).astype(BF16)
        out_ref[0] = pltpu.bitcast(act, jnp.uint32)

    _stream_expert_blocks(first_ref, nblk_ref, nused_ref, xs_hbm, h_hbm, ibuf, obuf, isem, osem,
                          prepare, compute)


def _ffn_up(first_blk, n_blk, n_used, xs, w_gate_up, b_gate_up):
    n_rows = xs.shape[0]
    tf = FFN_TF
    nf = D_FF // tf
    return pl.pallas_call(
        _ffn_up_body,
        grid_spec=pltpu.PrefetchScalarGridSpec(
            num_scalar_prefetch=3,
            grid=(nf, N_EXPERTS),
            in_specs=[
                pl.BlockSpec(memory_space=pltpu.HBM),
                pl.BlockSpec((None, D_MODEL, tf), lambda j, e, *_: (e, 0, j)),
                pl.BlockSpec((None, D_MODEL, tf), lambda j, e, *_: (e, 0, nf + j)),
                pl.BlockSpec((None, 1, tf), lambda j, e, *_: (e, 0, j)),
                pl.BlockSpec((None, 1, tf), lambda j, e, *_: (e, 0, nf + j)),
            ],
            out_specs=pl.BlockSpec(memory_space=pltpu.HBM),
            scratch_shapes=[
                pltpu.VMEM((D_MODEL, tf), BF16),
                pltpu.VMEM((D_MODEL, tf), BF16),
                pltpu.VMEM((IN_SLOTS, MOE_BM, PACK_W), jnp.uint32),
                pltpu.VMEM((N_SLOTS, 1, MOE_BM // 2, tf), jnp.uint32),
                pltpu.SemaphoreType.DMA((IN_SLOTS,)),
                pltpu.SemaphoreType.DMA((N_SLOTS,)),
            ],
        ),
        out_shape=jax.ShapeDtypeStruct((n_rows // 2, D_FF), jnp.uint32),
        compiler_params=_params(("arbitrary", "arbitrary")),
        name="ffn_up",
    )(first_blk, n_blk, n_used, xs, w_gate_up, w_gate_up, b_gate_up, b_gate_up)


def _ffn_down_body(first_ref, nblk_ref, nused_ref, h_hbm, wd_ref, bd_ref, y_hbm, wd_bf, ibuf, obuf, isem, osem):
    n_chunks, _, cw = wd_bf.shape

    def prepare():
        for n in range(n_chunks):
            wd_bf[n] = wd_ref[:, n * cw:(n + 1) * cw].astype(BF16)

    def compute(paired, out_ref):
        hid = pltpu.bitcast(paired, BF16)

        def chunk(n, carry):
            out_ref[n] = jnp.dot(hid, wd_bf[n], preferred_element_type=F32) + bd_ref[n]
            return carry

        lax.fori_loop(0, n_chunks, chunk, 0)

    _stream_expert_blocks(first_ref, nblk_ref, nused_ref, h_hbm, y_hbm, ibuf, obuf, isem, osem,
                          prepare, compute)


def _ffn_down(first_blk, n_blk, n_used, h, w_down, b_down):
    n_rows = 2 * h.shape[0]
    tn = FFN_TN
    cw = FFN_CHUNK
    nc = tn // cw
    return pl.pallas_call(
        _ffn_down_body,
        grid_spec=pltpu.PrefetchScalarGridSpec(
            num_scalar_prefetch=3,
            grid=(D_MODEL // tn, N_EXPERTS),
            in_specs=[
                pl.BlockSpec(memory_space=pltpu.HBM),
                pl.BlockSpec((None, D_FF, tn), lambda j, e, *_: (e, 0, j)),
                pl.BlockSpec((None, nc, 1, cw), lambda j, e, *_: (e, j, 0, 0)),
            ],
            out_specs=pl.BlockSpec(memory_space=pltpu.HBM),
            scratch_shapes=[
                pltpu.VMEM((nc, D_FF, cw), BF16),
                pltpu.VMEM((IN_SLOTS, MOE_BM // 2, D_FF), jnp.uint32),
                pltpu.VMEM((N_SLOTS, nc, MOE_BM, cw), F32),
                pltpu.SemaphoreType.DMA((IN_SLOTS,)),
                pltpu.SemaphoreType.DMA((N_SLOTS,)),
            ],
        ),
        out_shape=jax.ShapeDtypeStruct((n_rows, D_MODEL), F32),
        compiler_params=_params(("arbitrary", "arbitrary")),
        name="ffn_down",
    )(first_blk, n_blk, n_used, h, w_down, b_down)


COMBINE_TM = 128


def _combine_body(dest_ref, next_ref, h_ref, w_ref, g_ref, y_hbm, o_ref, gbuf, sem):
    i = pl.program_id(0)
    tm = h_ref.shape[0]

    def gather_tile(table_ref, slot):
        def start(r, c):
            for k in range(TOP_K):
                pltpu.make_async_copy(y_hbm.at[pl.ds(table_ref[0, 0, k * tm + r], 1), :],
                                      gbuf.at[slot, k, pl.ds(r, 1), :], sem.at[slot]).start(priority=k % 2)
            return c

        lax.fori_loop(0, tm, start, 0)

    slot = lax.rem(i, N_SLOTS)

    @pl.when(i == 0)
    def _():
        gather_tile(dest_ref, 0)

    @pl.when(i + 1 < pl.num_programs(0))
    def _():
        gather_tile(next_ref, 1 - slot)

    for k in range(TOP_K):
        pltpu.make_async_copy(y_hbm.at[pl.ds(0, tm), :], gbuf.at[slot, k], sem.at[slot]).wait()
    h = h_ref[...]
    for k in range(TOP_K):
        h = h + w_ref[:, k:k + 1] * gbuf[slot, k]
    o_ref[...] = h * lax.rsqrt(jnp.mean(h * h, axis=-1, keepdims=True) + EPS) * g_ref[...]


def _combine(dest_tiles, h1, w_cols, g_final, y):
    t = h1.shape[0]
    tm = dest_tiles.shape[2] // TOP_K
    d = D_MODEL
    last = t // tm - 1
    return pl.pallas_call(
        _combine_body,
        grid=(t // tm,),
        in_specs=[
            pl.BlockSpec((1, 1, TOP_K * tm), lambda i: (i, 0, 0), memory_space=pltpu.SMEM),
            pl.BlockSpec((1, 1, TOP_K * tm), lambda i: (jnp.minimum(i + 1, last), 0, 0), memory_space=pltpu.SMEM),
            pl.BlockSpec((tm, d), lambda i: (i, 0)),
            pl.BlockSpec((tm, TOP_K), lambda i: (i, 0)),
            pl.BlockSpec((1, d), lambda i: (0, 0)),
            pl.BlockSpec(memory_space=pltpu.HBM),
        ],
        out_specs=pl.BlockSpec((tm, d), lambda i: (i, 0)),
        out_shape=jax.ShapeDtypeStruct((t, d), F32),
        scratch_shapes=[pltpu.VMEM((N_SLOTS, TOP_K, tm, d), F32), pltpu.SemaphoreType.DMA((N_SLOTS,))],
        compiler_params=_params(("arbitrary",)),
        name="combine",
    )(dest_tiles, dest_tiles, h1, w_cols, g_final, y)


def _tile_major(a, tm):
    k, t = a.shape
    return a.reshape(k, t // tm, tm).transpose(1, 0, 2).reshape(t // tm, 1, k * tm)


def _routing_tables(top_e, rank, counts, t):
    bm = MOE_BM
    nb = (t * TOP_K) // bm + N_EXPERTS
    padded = (counts + bm - 1) // bm * bm
    pad_end = jnp.cumsum(padded)
    pad_start = pad_end - padded
    onehot = top_e[:, :, None] == jnp.arange(N_EXPERTS, dtype=jnp.int32)
    dest = rank + jnp.sum(jnp.where(onehot, pad_start, 0), axis=-1)
    first_blk = (pad_start // bm).astype(jnp.int32)
    n_blk = (padded // bm).astype(jnp.int32)
    n_used = (pad_end[-1] // bm).astype(jnp.int32).reshape(1)
    gap = padded - counts
    gap_end = jnp.cumsum(gap)
    j = jnp.arange(N_EXPERTS * bm, dtype=jnp.int32)
    ej = jnp.minimum(jnp.sum(j[:, None] >= gap_end[None, :], axis=1), N_EXPERTS - 1)
    pad_rows = (pad_start + counts)[ej] + j - (gap_end - gap)[ej]
    pad_rows = jnp.clip(pad_rows, 0, nb * bm - 1).astype(jnp.int32)
    n_pad = gap_end[-1].astype(jnp.int32).reshape(1)
    return dest.astype(jnp.int32), first_blk, n_blk, n_used, pad_rows, n_pad, nb * bm


def kernel(x, g_mix, w_in, ssm_conv_w, ssm_conv_b, ssm_dt_bias, ssm_a_log, ssm_d, ssm_norm_g, w_ssm_out,
           sc_conv_w, w_sc_out, b_gate, w_o, g_ffn, w_router, b_router, w_gate_up, b_gate_up, w_down,
           b_down, g_final):
    bsz, seq, d = x.shape
    t = bsz * seq
    assert bsz == 1 and d == D_MODEL and w_in.shape[0] == 1
    xt = x.reshape(t, d)
    w_dt = w_in[0, :, OFF_DT:OFF_SC]
    w_dt_hi = w_dt.astype(BF16)
    w_dt = jnp.concatenate([w_dt_hi, (w_dt - w_dt_hi.astype(F32)).astype(BF16)], axis=1)
    col = lambda a: a.reshape(-1, 1)
    row = lambda a: a.reshape(1, -1)

    u, dt_raw = _prenorm(xt, row(g_mix[0]), w_dt)
    proj = _inproj(u, w_in[0].T)
    y_norm = _ssd(proj, dt_raw.T, ssm_conv_w[0], row(ssm_conv_b[0]), col(ssm_dt_bias[0]), col(ssm_a_log[0]),
                  col(ssm_d[0]), row(ssm_norm_g[0]))
    mixed = _mix(y_norm, proj, row(b_gate[0]), sc_conv_w[0], w_ssm_out[0].astype(BF16), w_sc_out[0].astype(BF16))
    wr_t = w_router[0].T
    wr_hi = wr_t.astype(BF16)
    wr_lo = (wr_t - wr_hi.astype(F32)).astype(BF16)
    h1, xp, top_e, top_w, rank, counts = _route(mixed, xt, w_o[0].astype(BF16), row(g_ffn[0]),
                                                jnp.concatenate([wr_hi, wr_lo], axis=0), col(b_router[0]))
    dest, first_blk, n_blk, n_used, pad_rows, n_pad, n_rows = _routing_tables(top_e, rank, counts[:, 0], t)
    xs = _dispatch(xp, _tile_major(dest, min(DISPATCH_TM, t)), pad_rows, n_pad, n_used, n_rows)
    hid = _ffn_up(first_blk, n_blk, n_used, xs, w_gate_up[0], b_gate_up[0].reshape(N_EXPERTS, 1, 2 * D_FF))
    y = _ffn_down(first_blk, n_blk, n_used, hid, w_down[0],
                  b_down[0].reshape(N_EXPERTS, D_MODEL // FFN_CHUNK, 1, FFN_CHUNK))
    out = _combine(_tile_major(dest, min(COMBINE_TM, t)), h1, top_w.T, row(g_final), y)
    return out.reshape(bsz, seq, d)
```

```python
import functools

import jax
import jax.numpy as jnp
from jax import lax
from jax.experimental import pallas as pl
from jax.experimental.pallas import tpu as pltpu

D_MODEL = 2048
SSM_D_INNER = 2 * D_MODEL
SSM_HEAD_DIM = 64
SSM_N_HEADS = SSM_D_INNER // SSM_HEAD_DIM
SSM_N_GROUPS = 8
SSM_HEADS_PER_GROUP = SSM_N_HEADS // SSM_N_GROUPS
SSM_D_STATE = 128
SSM_CONV = 4
SSM_GN = SSM_N_GROUPS * SSM_D_STATE
SSM_CONV_DIM = SSM_D_INNER + 2 * SSM_GN
SSM_GROUP_CH = SSM_D_INNER // SSM_N_GROUPS
SC_DIM = D_MODEL
SC_WIDTH = 3
N_EXPERTS = 32
TOP_K = 4
D_FF = D_MODEL
SWIGLU_LIMIT = 7.0
SWIGLU_ALPHA = 1.702
EPS = 1e-5

OFF_Z = 0
OFF_XBC = OFF_Z + SSM_D_INNER
OFF_DT = OFF_XBC + SSM_CONV_DIM
OFF_SC = OFF_DT + SSM_N_HEADS
OFF_GATE = OFF_SC + 3 * SC_DIM
D_IN_PROJ = OFF_GATE + 2 * D_MODEL

P_Z = 0
P_XBC = P_Z + SSM_D_INNER
P_SC = P_XBC + SSM_CONV_DIM
P_GATE = P_SC + 3 * SC_DIM
P_TOTAL = P_GATE + 2 * D_MODEL

SUBLANES = 8
VMEM_LIMIT = 56 * 1024 * 1024

F32 = jnp.float32
BF16 = jnp.bfloat16
HIGHEST = lax.Precision.HIGHEST
NT_DIMS = (((1,), (1,)), ((), ()))
TN_DIMS = (((0,), (0,)), ((), ()))


def _sigmoid(v):
    return 0.5 * jnp.tanh(0.5 * v) + 0.5


def _params(semantics):
    return pltpu.CompilerParams(dimension_semantics=semantics, vmem_limit_bytes=VMEM_LIMIT)


PRENORM_TM = 512
INPROJ_TM = 1024
INPROJ_TN = 1024
LANES = 128
DT_SHIFT = OFF_SC - OFF_DT
ALIGNED_TILES = OFF_DT // INPROJ_TN
CAST_ROWS = 256
EPILOGUE_ROWS = 256


def _prenorm_body(x_ref, g_ref, wdt_ref, u_ref, dt_ref):
    x = x_ref[...]
    u = x * lax.rsqrt(jnp.mean(x * x, axis=-1, keepdims=True) + EPS) * g_ref[...]
    u_hi = u.astype(BF16)
    u_ref[...] = u_hi
    u_lo = (u - u_hi.astype(F32)).astype(BF16)
    both = jnp.dot(u_hi, wdt_ref[...], preferred_element_type=F32)
    cross = jnp.dot(u_lo, wdt_ref[...], preferred_element_type=F32)
    nh = SSM_N_HEADS
    dt_ref[...] = both[:, :nh] + both[:, nh:] + cross[:, :nh]


def _prenorm(x, g, w_dt):
    t = x.shape[0]
    tm = min(PRENORM_TM, t)
    return pl.pallas_call(
        _prenorm_body,
        grid=(t // tm,),
        in_specs=[
            pl.BlockSpec((tm, D_MODEL), lambda i: (i, 0)),
            pl.BlockSpec((1, D_MODEL), lambda i: (0, 0)),
            pl.BlockSpec((D_MODEL, 2 * SSM_N_HEADS), lambda i: (0, 0)),
        ],
        out_specs=[
            pl.BlockSpec((tm, D_MODEL), lambda i: (i, 0)),
            pl.BlockSpec((tm, SSM_N_HEADS), lambda i: (i, 0)),
        ],
        out_shape=[
            jax.ShapeDtypeStruct((t, D_MODEL), BF16),
            jax.ShapeDtypeStruct((t, SSM_N_HEADS), F32),
        ],
        compiler_params=_params(("parallel",)),
        name="prenorm",
    )(x, g, w_dt)


def _inproj_body(u_ref, wt_ref, proj_ref, w_bf):
    j, i = pl.program_id(0), pl.program_id(1)
    tm = u_ref.shape[0]
    z_tiles = P_XBC // w_bf.shape[0]

    @pl.when(i == 0)
    def _():
        for r in range(0, w_bf.shape[0], CAST_ROWS):
            w_bf[r:r + CAST_ROWS, :] = wt_ref[r:r + CAST_ROWS, :].astype(BF16)

    def project(r0, rows):
        return lax.dot_general(u_ref[r0:r0 + rows, :], w_bf[...], NT_DIMS, preferred_element_type=F32)

    chunk = min(EPILOGUE_ROWS, tm)

    @pl.when(j < z_tiles)
    def _():
        for r0 in range(0, tm, chunk):
            acc = project(r0, chunk)
            proj_ref[r0:r0 + chunk, :] = (acc * _sigmoid(acc)).astype(BF16)

    @pl.when(j >= z_tiles)
    def _():
        proj_ref[...] = project(0, tm).astype(BF16)


def _inproj(u, w_in_t):
    t = u.shape[0]
    tm = min(INPROJ_TM, t)
    tn = INPROJ_TN
    assert OFF_DT % tn == 0 and DT_SHIFT % SUBLANES == 0 and P_XBC % tn == 0
    first_row = lambda j: pl.multiple_of(j * tn + jnp.where(j >= ALIGNED_TILES, DT_SHIFT, 0), SUBLANES)
    return pl.pallas_call(
        _inproj_body,
        grid=(P_TOTAL // tn, t // tm),
        in_specs=[
            pl.BlockSpec((tm, D_MODEL), lambda j, i: (i, 0)),
            pl.BlockSpec((pl.Element(tn), pl.Element(D_MODEL)), lambda j, i: (first_row(j), 0)),
        ],
        out_specs=pl.BlockSpec((tm, tn), lambda j, i: (i, j)),
        out_shape=jax.ShapeDtypeStruct((t, P_TOTAL), BF16),
        scratch_shapes=[pltpu.VMEM((tn, D_MODEL), BF16)],
        compiler_params=_params(("arbitrary", "arbitrary")),
        name="inproj",
    )(u, w_in_t)


SSD_L = 256
SSD_GROUPS = 4
HEAD_PAIR = 2 * SSM_HEAD_DIM


def _ssd_body(z_ref, x_ref, b_ref, c_ref, dt_ref, widen_ref, wx_ref, wb_ref, wc_ref, bx_ref, bb_ref, bc_ref,
              dtb_ref, alog_ref, d_ref, ng_ref, o_ref, s_ref, xbuf, bbuf, cbuf):
    L = x_ref.shape[0]
    tail = SUBLANES

    @pl.when(pl.program_id(1) == 0)
    def _():
        s_ref[...] = jnp.zeros_like(s_ref)
        xbuf[0:tail, :] = jnp.zeros((tail, xbuf.shape[1]), F32)
        bbuf[0:tail, :] = jnp.zeros((tail, bbuf.shape[1]), F32)
        cbuf[0:tail, :] = jnp.zeros((tail, cbuf.shape[1]), F32)

    def conv_silu(buf, in_ref, w_ref, bias_ref):
        buf[tail:tail + L, :] = in_ref[...].astype(F32)
        acc = bias_ref[...] + w_ref[SSM_CONV - 1:SSM_CONV, :] * buf[tail:tail + L, :]
        for k in range(SSM_CONV - 1):
            off = tail - (SSM_CONV - 1) + k
            acc = acc + w_ref[k:k + 1, :] * buf[off:off + L, :]
        buf[0:tail, :] = buf[L:L + tail, :]
        return acc * _sigmoid(acc)

    xs_all = conv_silu(xbuf, x_ref, wx_ref, bx_ref)
    bm_all = conv_silu(bbuf, b_ref, wb_ref, bb_ref).astype(BF16)
    cm_all = conv_silu(cbuf, c_ref, wc_ref, bc_ref).astype(BF16)

    dt_raw = dt_ref[...] + dtb_ref[...]
    dt_all = jnp.maximum(dt_raw, 0.0) + jnp.log(1.0 + jnp.exp(-jnp.abs(dt_raw)))
    da_all = dt_all * (-jnp.exp(alog_ref[...]))
    hg = SSM_HEADS_PER_GROUP
    gc = SSM_GROUP_CH
    n = SSM_D_STATE
    row = lax.broadcasted_iota(jnp.int32, (L, L), 0)
    col = lax.broadcasted_iota(jnp.int32, (L, L), 1)
    causal = row >= col
    incl = (row <= col).astype(BF16)
    lane = lax.broadcasted_iota(jnp.int32, (L, HEAD_PAIR), 1)
    first = lane < SSM_HEAD_DIM

    def pieces(a):
        hi = a.astype(BF16).astype(F32)
        mid = (a - hi).astype(BF16).astype(F32)
        lo = (a - hi - mid).astype(BF16).astype(F32)
        return hi, mid, lo

    for gi in range(x_ref.shape[1] // gc):
        ch = slice(gi * gc, (gi + 1) * gc)
        st = slice(gi * n, (gi + 1) * n)
        hd = slice(gi * hg, (gi + 1) * hg)
        xs, bm, cm, dt = xs_all[:, ch], bm_all[:, st], cm_all[:, st], dt_all[hd, :]
        da_p = pieces(da_all[hd, :])
        parts = jnp.dot(jnp.concatenate(da_p, axis=0).astype(BF16), incl, preferred_element_type=F32)
        cs = parts[0:hg] + parts[hg:2 * hg] + parts[2 * hg:3 * hg]
        cs_end = cs[:, L - 1:L]
        split = [pieces(a) for a in (dt, jnp.exp(cs_end - cs), jnp.exp(cs))]
        stacked = jnp.concatenate([split[q][s] for s in range(3) for q in range(3)] + [cs], axis=0)
        flipped = stacked.T
        cs_t = flipped[:, 9 * hg:10 * hg]
        wide = jnp.dot(flipped.astype(BF16), widen_ref[...], preferred_element_type=F32)
        dt_x, to_end_x, ecs_x = wide[:, 0:gc], wide[:, gc:2 * gc], wide[:, 2 * gc:3 * gc]

        cb = lax.dot_general(cm, bm, NT_DIMS, preferred_element_type=F32)
        cb = jnp.where(causal, cb, 0.0)
        y_off = jnp.dot(cm, s_ref[gi].astype(BF16), preferred_element_type=F32)

        xdt = xs * dt_x
        xdt_b = xdt.astype(BF16)
        ys = []
        for p in range(hg // 2):
            sl = slice(p * HEAD_PAIR, (p + 1) * HEAD_PAIR)
            yd = []
            for h in (2 * p, 2 * p + 1):
                seg = cs_t[:, h:h + 1] - cs[h:h + 1, :]
                m = cb * jnp.exp(jnp.minimum(seg, 0.0))
                yd.append(jnp.dot(m.astype(BF16), xdt_b[:, sl], preferred_element_type=F32))
            ys.append(jnp.where(first, yd[0], yd[1]))
        y = jnp.concatenate(ys, axis=1) + y_off * ecs_x + xs * d_ref[:, ch]

        xw = (xdt * to_end_x).astype(BF16)
        upd = lax.dot_general(bm, xw, TN_DIMS, preferred_element_type=F32)
        s_ref[gi] = s_ref[gi] * ecs_x[L - 1:L, :] + upd

        v = y * z_ref[:, ch].astype(F32)
        v = v * lax.rsqrt(jnp.mean(v * v, axis=-1, keepdims=True) + EPS)
        o_ref[:, ch] = (v * ng_ref[:, ch]).astype(BF16)


def _ssd(proj, dt_rows, conv_w, conv_b, dt_bias, a_log, d_skip, norm_g):
    t = proj.shape[0]
    L = min(SSD_L, t)
    gc = SSM_GROUP_CH
    n = SSM_D_STATE
    zc, xc = P_Z // gc, P_XBC // gc
    bc, cc = (P_XBC + SSM_D_INNER) // n, (P_XBC + SSM_D_INNER + SSM_GN) // n
    wbc, wcc = SSM_D_INNER // n, (SSM_D_INNER + SSM_GN) // n
    hg = SSM_HEADS_PER_GROUP
    n_factors, n_pieces = 3, 3
    r = jnp.arange((n_factors * n_pieces + 1) * hg)[:, None]
    c = jnp.arange(n_factors * gc)[None, :]
    widen = ((r < n_factors * n_pieces * hg)
             & (r % (n_factors * hg) == (c // gc) * hg + (c % gc) // SSM_HEAD_DIM)).astype(BF16)
    d_wide = jnp.repeat(d_skip.reshape(-1), SSM_HEAD_DIM).reshape(1, -1)
    k = SSD_GROUPS
    gc, n, hg = k * gc, k * n, k * hg
    zc, xc, bc, cc, wbc, wcc = zc // k, xc // k, bc // k, cc // k, wbc // k, wcc // k
    return pl.pallas_call(
        _ssd_body,
        grid=(SSM_N_GROUPS // k, t // L),
        in_specs=[
            pl.BlockSpec((L, gc), lambda g, i: (i, zc + g)),
            pl.BlockSpec((L, gc), lambda g, i: (i, xc + g)),
            pl.BlockSpec((L, n), lambda g, i: (i, bc + g)),
            pl.BlockSpec((L, n), lambda g, i: (i, cc + g)),
            pl.BlockSpec((hg, L), lambda g, i: (g, i)),
            pl.BlockSpec(widen.shape, lambda g, i: (0, 0)),
            pl.BlockSpec((SSM_CONV, gc), lambda g, i: (0, g)),
            pl.BlockSpec((SSM_CONV, n), lambda g, i: (0, wbc + g)),
            pl.BlockSpec((SSM_CONV, n), lambda g, i: (0, wcc + g)),
            pl.BlockSpec((1, gc), lambda g, i: (0, g)),
            pl.BlockSpec((1, n), lambda g, i: (0, wbc + g)),
            pl.BlockSpec((1, n), lambda g, i: (0, wcc + g)),
            pl.BlockSpec((hg, 1), lambda g, i: (g, 0)),
            pl.BlockSpec((hg, 1), lambda g, i: (g, 0)),
            pl.BlockSpec((1, gc), lambda g, i: (0, g)),
            pl.BlockSpec((1, gc), lambda g, i: (0, g)),
        ],
        out_specs=pl.BlockSpec((L, gc), lambda g, i: (i, g)),
        out_shape=jax.ShapeDtypeStruct((t, SSM_D_INNER), BF16),
        scratch_shapes=[
            pltpu.VMEM((k, SSM_D_STATE, SSM_GROUP_CH), F32),
            pltpu.VMEM((L + SUBLANES, gc), F32),
            pltpu.VMEM((L + SUBLANES, n), F32),
            pltpu.VMEM((L + SUBLANES, n), F32),
        ],
        compiler_params=_params(("parallel", "arbitrary")),
        name="ssd",
    )(proj, proj, proj, proj, dt_rows, widen, conv_w, conv_w, conv_w, conv_b, conv_b, conv_b,
      dt_bias, a_log, d_wide, norm_g)


MIX_TM = 256


def _mix_body(yn_ref, b_ref, c_ref, v_ref, cp_ref, vp_ref, g1_ref, g2_ref, bg1_ref, bg2_ref,
              wc_ref, wssm_ref, wsc_ref, o_ref, buf):
    tm = yn_ref.shape[0]
    tail = SUBLANES
    prev = cp_ref[...].astype(F32) * vp_ref[...].astype(F32)
    buf[0:tail, :] = jnp.where(pl.program_id(0) == 0, 0.0, prev)
    cv = c_ref[...].astype(F32) * v_ref[...].astype(F32)
    buf[tail:tail + tm, :] = cv
    conv = wc_ref[SC_WIDTH - 1:SC_WIDTH, :] * cv
    for k in range(SC_WIDTH - 1):
        off = tail - (SC_WIDTH - 1) + k
        conv = conv + wc_ref[k:k + 1, :] * buf[off:off + tm, :]
    sc_in = (b_ref[...].astype(F32) * conv).astype(BF16)
    y_sc = jnp.dot(sc_in, wsc_ref[...], preferred_element_type=F32)
    y_ssm = jnp.dot(yn_ref[...], wssm_ref[...], preferred_element_type=F32)
    g1 = _sigmoid(g1_ref[...].astype(F32) + bg1_ref[...])
    g2 = _sigmoid(g2_ref[...].astype(F32) + bg2_ref[...])
    o_ref[...] = (g1 * y_ssm + g2 * y_sc).astype(BF16)


def _resident(shape):
    return pl.BlockSpec(shape, lambda *_: (0,) * len(shape), pipeline_mode=pl.Buffered(1))


def _mix(y_norm, proj, b_gate, sc_conv_w, w_ssm_out, w_sc_out):
    t = y_norm.shape[0]
    tm = min(MIX_TM, t)
    d = D_MODEL
    sb, gb = P_SC // d, P_GATE // d
    prev_rows = lambda i: jnp.maximum(i * (tm // SUBLANES) - 1, 0)
    return pl.pallas_call(
        _mix_body,
        grid=(t // tm,),
        in_specs=[
            pl.BlockSpec((tm, SSM_D_INNER), lambda i: (i, 0)),
            pl.BlockSpec((tm, d), lambda i: (i, sb)),
            pl.BlockSpec((tm, d), lambda i: (i, sb + 1)),
            pl.BlockSpec((tm, d), lambda i: (i, sb + 2)),
            pl.BlockSpec((SUBLANES, d), lambda i: (prev_rows(i), sb + 1)),
            pl.BlockSpec((SUBLANES, d), lambda i: (prev_rows(i), sb + 2)),
            pl.BlockSpec((tm, d), lambda i: (i, gb)),
            pl.BlockSpec((tm, d), lambda i: (i, gb + 1)),
            pl.BlockSpec((1, d), lambda i: (0, 0)),
            pl.BlockSpec((1, d), lambda i: (0, 1)),
            pl.BlockSpec((SC_WIDTH, d), lambda i: (0, 0)),
            _resident((SSM_D_INNER, d)),
            _resident((d, d)),
        ],
        out_specs=pl.BlockSpec((tm, d), lambda i: (i, 0)),
        out_shape=jax.ShapeDtypeStruct((t, d), BF16),
        scratch_shapes=[pltpu.VMEM((tm + SUBLANES, d), F32)],
        compiler_params=_params(("parallel",)),
        name="mix",
    )(y_norm, proj, proj, proj, proj, proj, proj, proj, b_gate, b_gate, sc_conv_w, w_ssm_out, w_sc_out)


ROUTE_TM = 512
PACK_W = D_MODEL // 2


def _pack_rows(v):
    lo = lax.bitcast_convert_type(v[:, :PACK_W].astype(F32), jnp.uint32)
    hi = lax.bitcast_convert_type(v[:, PACK_W:].astype(F32), jnp.uint32)
    return hi | (lo >> 16)


def _unpack_rows(w):
    lo = lax.bitcast_convert_type(w << 16, F32).astype(BF16)
    hi = lax.bitcast_convert_type(w & jnp.uint32(0xFFFF0000), F32).astype(BF16)
    return jnp.concatenate([lo, hi], axis=1)


def _route_body(m_ref, x_ref, wo_ref, g_ref, wr_ref, br_ref,
                h_ref, xp_ref, e_ref, w_ref, r_ref, cnt_ref, carry):
    tm = m_ref.shape[0]

    @pl.when(pl.program_id(0) == 0)
    def _():
        carry[...] = jnp.zeros_like(carry)

    h = x_ref[...] + jnp.dot(m_ref[...], wo_ref[...], preferred_element_type=F32)
    h_ref[...] = h
    xn = h * lax.rsqrt(jnp.mean(h * h, axis=-1, keepdims=True) + EPS) * g_ref[...]
    xn_hi = xn.astype(BF16)
    xn_lo = (xn - xn_hi.astype(F32)).astype(BF16)
    xp_ref[...] = _pack_rows(xn_hi)

    ne = N_EXPERTS
    both = lax.dot_general(wr_ref[...], xn_hi, NT_DIMS, preferred_element_type=F32)
    cross = lax.dot_general(wr_ref[0:ne, :], xn_lo, NT_DIMS, preferred_element_type=F32)
    logits = both[0:ne] + both[ne:2 * ne] + cross + br_ref[...]
    eidx = lax.broadcasted_iota(jnp.int32, (N_EXPERTS, tm), 0)
    vals, hots = [], []
    for k in range(TOP_K):
        best = jnp.max(logits, axis=0, keepdims=True)
        arg = jnp.min(jnp.where(logits == best, eidx, N_EXPERTS), axis=0, keepdims=True)
        hot = eidx == arg
        e_ref[k:k + 1, :] = arg
        vals.append(best)
        hots.append(hot)
        logits = jnp.where(hot, -jnp.inf, logits)
    exps = [jnp.exp(v - vals[0]) for v in vals]
    denom = exps[0] + exps[1] + exps[2] + exps[3]
    for k in range(TOP_K):
        w_ref[k:k + 1, :] = exps[k] / denom

    cnt = (hots[0] | hots[1] | hots[2] | hots[3]).astype(F32)
    r_i = lax.broadcasted_iota(jnp.int32, (tm, tm), 0)
    c_i = lax.broadcasted_iota(jnp.int32, (tm, tm), 1)
    before = (r_i < c_i).astype(BF16)
    prior = carry[:, 0:1] + jnp.dot(cnt.astype(BF16), before, preferred_element_type=F32)
    for k in range(TOP_K):
        r_ref[k:k + 1, :] = jnp.sum(jnp.where(hots[k], prior, 0.0), axis=0, keepdims=True).astype(jnp.int32)
    total = carry[...] + jnp.sum(cnt, axis=1, keepdims=True)
    carry[...] = total
    cnt_ref[...] = total.astype(jnp.int32)


def _route(mixed, x, w_o, g_ffn, w_router_t, b_router):
    t = x.shape[0]
    tm = min(ROUTE_TM, t)
    d = D_MODEL
    return pl.pallas_call(
        _route_body,
        grid=(t // tm,),
        in_specs=[
            pl.BlockSpec((tm, d), lambda i: (i, 0)),
            pl.BlockSpec((tm, d), lambda i: (i, 0)),
            _resident((d, d)),
            pl.BlockSpec((1, d), lambda i: (0, 0)),
            pl.BlockSpec((2 * N_EXPERTS, d), lambda i: (0, 0)),
            pl.BlockSpec((N_EXPERTS, 1), lambda i: (0, 0)),
        ],
        out_specs=[
            pl.BlockSpec((tm, d), lambda i: (i, 0)),
            pl.BlockSpec((tm, PACK_W), lambda i: (i, 0)),
            pl.BlockSpec((TOP_K, tm), lambda i: (0, i)),
            pl.BlockSpec((TOP_K, tm), lambda i: (0, i)),
            pl.BlockSpec((TOP_K, tm), lambda i: (0, i)),
            pl.BlockSpec((N_EXPERTS, 128), lambda i: (0, 0)),
        ],
        out_shape=[
            jax.ShapeDtypeStruct((t, d), F32),
            jax.ShapeDtypeStruct((t, PACK_W), jnp.uint32),
            jax.ShapeDtypeStruct((TOP_K, t), jnp.int32),
            jax.ShapeDtypeStruct((TOP_K, t), F32),
            jax.ShapeDtypeStruct((TOP_K, t), jnp.int32),
            jax.ShapeDtypeStruct((N_EXPERTS, 128), jnp.int32),
        ],
        scratch_shapes=[pltpu.VMEM((N_EXPERTS, 128), F32)],
        compiler_params=_params(("arbitrary",)),
        name="route",
    )(mixed, x, w_o, g_ffn, w_router_t, b_router)


MOE_BM = 256
DISPATCH_TM = 256


def _dispatch_body(dest_ref, padrow_ref, npad_ref, nused_ref, xp_ref, xs_hbm, zblk, sem, zsem):
    tm = xp_ref.shape[0]
    bm = zblk.shape[0]
    nb = xs_hbm.shape[0] // bm

    @pl.when(pl.program_id(0) == 0)
    def _():
        zblk[...] = jnp.zeros_like(zblk)
        npad = npad_ref[0]
        nused = nused_ref[0]

        def zero_row(j):
            return pltpu.make_async_copy(zblk.at[pl.ds(0, 1), :], xs_hbm.at[pl.ds(padrow_ref[j], 1), :], zsem)

        def zero_block(b):
            return pltpu.make_async_copy(zblk, xs_hbm.at[pl.ds(pl.multiple_of(b * bm, bm), bm), :], zsem)

        def start_row(j, c):
            zero_row(j).start()
            return c

        def wait_row(j, c):
            zero_row(j).wait()
            return c

        def start_block(b, c):
            zero_block(b).start()
            return c

        def wait_block(b, c):
            zero_block(b).wait()
            return c

        lax.fori_loop(0, npad, start_row, 0)
        lax.fori_loop(nused, nb, start_block, 0)
        lax.fori_loop(0, npad, wait_row, 0)
        lax.fori_loop(nused, nb, wait_block, 0)

    def row_copy(r, k):
        return pltpu.make_async_copy(xp_ref.at[pl.ds(r, 1), :],
                                     xs_hbm.at[pl.ds(dest_ref[0, 0, k * tm + r], 1), :], sem)

    def start(r, c):
        for k in range(TOP_K):
            row_copy(r, k).start(priority=k % 2)
        return c

    lax.fori_loop(0, tm, start, 0)
    for k in range(TOP_K):
        pltpu.make_async_copy(xp_ref, xs_hbm.at[pl.ds(0, tm), :], sem).wait()


def _dispatch(xp, dest_tiles, pad_rows, n_pad, n_used, n_rows):
    t = xp.shape[0]
    tm = dest_tiles.shape[2] // TOP_K
    return pl.pallas_call(
        _dispatch_body,
        grid=(t // tm,),
        in_specs=[
            pl.BlockSpec((1, 1, TOP_K * tm), lambda i: (i, 0, 0), memory_space=pltpu.SMEM),
            pl.BlockSpec(memory_space=pltpu.SMEM),
            pl.BlockSpec(memory_space=pltpu.SMEM),
            pl.BlockSpec(memory_space=pltpu.SMEM),
            pl.BlockSpec((tm, PACK_W), lambda i: (i, 0)),
        ],
        out_specs=pl.BlockSpec(memory_space=pltpu.HBM),
        out_shape=jax.ShapeDtypeStruct((n_rows, PACK_W), jnp.uint32),
        scratch_shapes=[
            pltpu.VMEM((MOE_BM, PACK_W), jnp.uint32),
            pltpu.SemaphoreType.DMA(()),
            pltpu.SemaphoreType.DMA(()),
        ],
        compiler_params=_params(("arbitrary",)),
        name="dispatch",
    )(dest_tiles, pad_rows, n_pad, n_used, xp)


FFN_TF = 1024
FFN_TN = 2048
FFN_CHUNK = 2048
N_SLOTS = 2
IN_SLOTS = 3
STREAM_PRIORITY = 1


def _stream_expert_blocks(first_ref, nblk_ref, nused_ref, src_hbm, dst_hbm, ibuf, obuf, isem, osem,
                          prepare, compute):
    j, e = pl.program_id(0), pl.program_id(1)
    n_in = ibuf.shape[0]
    ahead = n_in - 1
    bm_in = ibuf.shape[1]
    n_chunks, bm_out, cw = obuf.shape[1:]
    width = n_chunks * cw
    half = bm_in // 2
    col = pl.multiple_of(j * width, width)
    nblk = nblk_ref[e]
    first = first_ref[e]

    def fetch(b, slot):
        src = src_hbm.at[pl.ds(pl.multiple_of(b * bm_in, bm_in), bm_in), :]
        return pltpu.make_async_copy(src, ibuf.at[slot], isem.at[slot])

    def fetch_start(b, slot):
        for p in range(2):
            src = src_hbm.at[pl.ds(pl.multiple_of(b * bm_in + p * half, half), half), :]
            pltpu.make_async_copy(src, ibuf.at[slot, pl.ds(p * half, half), :], isem.at[slot]).start(priority=p)

    def flush(b, slot):
        out_rows = pl.ds(pl.multiple_of(b * bm_out, bm_out), bm_out)
        return [pltpu.make_async_copy(obuf.at[slot, n], dst_hbm.at[out_rows, pl.ds(col + n * cw, cw)], osem.at[slot])
                for n in range(n_chunks)]

    def flush_start(b, slot):
        for cp in flush(b, slot):
            cp.start(priority=STREAM_PRIORITY)

    def flush_wait(b, slot):
        for cp in flush(b, slot):
            cp.wait()

    for d in range(ahead):
        @pl.when(nblk > d)
        def _():
            fetch_start(first + d, d)

    @pl.when(nblk > 0)
    def _():
        prepare()

    def step(i, carry):
        slot = lax.rem(i, n_in)
        oslot = lax.rem(i, N_SLOTS)
        fetch(first + i, slot).wait()

        @pl.when(i + ahead < nblk)
        def _():
            fetch_start(first + i + ahead, lax.rem(i + ahead, n_in))

        @pl.when(i >= N_SLOTS)
        def _():
            flush_wait(first + i - N_SLOTS, oslot)

        compute(ibuf[slot], obuf.at[oslot])
        flush_start(first + i, oslot)
        return carry

    lax.fori_loop(0, nblk, step, 0)

    @pl.when(nblk >= 2)
    def _():
        flush_wait(first + nblk - 2, lax.rem(nblk, N_SLOTS))

    @pl.when(nblk >= 1)
    def _():
        flush_wait(first + nblk - 1, lax.rem(nblk - 1, N_SLOTS))

    @pl.when(e == pl.num_programs(1) - 1)
    def _():
        obuf[0] = jnp.zeros(obuf.shape[1:], obuf.dtype)

        def zero(b, carry):
            flush_start(b, 0)
            flush_wait(b, 0)
            return carry

        lax.fori_loop(nused_ref[0], dst_hbm.shape[0] // bm_out, zero, 0)


def _ffn_up_body(first_ref, nblk_ref, nused_ref, xs_hbm, wg_ref, wu_ref, bg_ref, bu_ref, h_hbm,
                 wg_bf, wu_bf, ibuf, obuf, isem, osem):
    def prepare():
        wg_bf[...] = wg_ref[...].astype(BF16)
        wu_bf[...] = wu_ref[...].astype(BF16)

    def compute(packed, out_ref):
        x = _unpack_rows(packed)
        gate = jnp.dot(x, wg_bf[...], preferred_element_type=F32) + bg_ref[...]
        up = jnp.dot(x, wu_bf[...], preferred_element_type=F32) + bu_ref[...]
        gate = jnp.minimum(gate, SWIGLU_LIMIT)
        up = jnp.clip(up, -SWIGLU_LIMIT, SWIGLU_LIMIT)
        act = ((up + 1.0) * gate * _sigmoid(SWIGLU_ALPHA * gate)).astype(BF16)
        out_ref[0] = pltpu.bitcast(act, jnp.uint32)

    _stream_expert_blocks(first_ref, nblk_ref, nused_ref, xs_hbm, h_hbm, ibuf, obuf, isem, osem,
                          prepare, compute)


def _ffn_up(first_blk, n_blk, n_used, xs, w_gate_up, b_gate_up):
    n_rows = xs.shape[0]
    tf = FFN_TF
    nf = D_FF // tf
    return pl.pallas_call(
        _ffn_up_body,
        grid_spec=pltpu.PrefetchScalarGridSpec(
            num_scalar_prefetch=3,
            grid=(nf, N_EXPERTS),
            in_specs=[
                pl.BlockSpec(memory_space=pltpu.HBM),
                pl.BlockSpec((None, D_MODEL, tf), lambda j, e, *_: (e, 0, j)),
                pl.BlockSpec((None, D_MODEL, tf), lambda j, e, *_: (e, 0, nf + j)),
                pl.BlockSpec((None, 1, tf), lambda j, e, *_: (e, 0, j)),
                pl.BlockSpec((None, 1, tf), lambda j, e, *_: (e, 0, nf + j)),
            ],
            out_specs=pl.BlockSpec(memory_space=pltpu.HBM),
            scratch_shapes=[
                pltpu.VMEM((D_MODEL, tf), BF16),
                pltpu.VMEM((D_MODEL, tf), BF16),
                pltpu.VMEM((IN_SLOTS, MOE_BM, PACK_W), jnp.uint32),
                pltpu.VMEM((N_SLOTS, 1, MOE_BM // 2, tf), jnp.uint32),
                pltpu.SemaphoreType.DMA((IN_SLOTS,)),
                pltpu.SemaphoreType.DMA((N_SLOTS,)),
            ],
        ),
        out_shape=jax.ShapeDtypeStruct((n_rows // 2, D_FF), jnp.uint32),
        compiler_params=_params(("arbitrary", "arbitrary")),
        name="ffn_up",
    )(first_blk, n_blk, n_used, xs, w_gate_up, w_gate_up, b_gate_up, b_gate_up)


def _ffn_down_body(first_ref, nblk_ref, nused_ref, h_hbm, wd_ref, bd_ref, y_hbm, wd_bf, ibuf, obuf, isem, osem):
    n_chunks, _, cw = wd_bf.shape

    def prepare():
        for n in range(n_chunks):
            wd_bf[n] = wd_ref[:, n * cw:(n + 1) * cw].astype(BF16)

    def compute(paired, out_ref):
        hid = pltpu.bitcast(paired, BF16)

        def chunk(n, carry):
            out_ref[n] = jnp.dot(hid, wd_bf[n], preferred_element_type=F32) + bd_ref[n]
            return carry

        lax.fori_loop(0, n_chunks, chunk, 0)

    _stream_expert_blocks(first_ref, nblk_ref, nused_ref, h_hbm, y_hbm, ibuf, obuf, isem, osem,
                          prepare, compute)


def _ffn_down(first_blk, n_blk, n_used, h, w_down, b_down):
    n_rows = 2 * h.shape[0]
    tn = FFN_TN
    cw = FFN_CHUNK
    nc = tn // cw
    return pl.pallas_call(
        _ffn_down_body,
        grid_spec=pltpu.PrefetchScalarGridSpec(
            num_scalar_prefetch=3,
            grid=(D_MODEL // tn, N_EXPERTS),
            in_specs=[
                pl.BlockSpec(memory_space=pltpu.HBM),
                pl.BlockSpec((None, D_FF, tn), lambda j, e, *_: (e, 0, j)),
                pl.BlockSpec((None, nc, 1, cw), lambda j, e, *_: (e, j, 0, 0)),
            ],
            out_specs=pl.BlockSpec(memory_space=pltpu.HBM),
            scratch_shapes=[
                pltpu.VMEM((nc, D_FF, cw), BF16),
                pltpu.VMEM((IN_SLOTS, MOE_BM // 2, D_FF), jnp.uint32),
                pltpu.VMEM((N_SLOTS, nc, MOE_BM, cw), F32),
                pltpu.SemaphoreType.DMA((IN_SLOTS,)),
                pltpu.SemaphoreType.DMA((N_SLOTS,)),
            ],
        ),
        out_shape=jax.ShapeDtypeStruct((n_rows, D_MODEL), F32),
        compiler_params=_params(("arbitrary", "arbitrary")),
        name="ffn_down",
    )(first_blk, n_blk, n_used, h, w_down, b_down)


COMBINE_TM = 128


def _combine_body(dest_ref, next_ref, h_ref, w_ref, g_ref, y_hbm, o_ref, gbuf, sem):
    i = pl.program_id(0)
    tm = h_ref.shape[0]

    def gather_tile(table_ref, slot):
        def start(r, c):
            for k in range(TOP_K):
                pltpu.make_async_copy(y_hbm.at[pl.ds(table_ref[0, 0, k * tm + r], 1), :],
                                      gbuf.at[slot, k, pl.ds(r, 1), :], sem.at[slot]).start(priority=k % 2)
            return c

        lax.fori_loop(0, tm, start, 0)

    slot = lax.rem(i, N_SLOTS)

    @pl.when(i == 0)
    def _():
        gather_tile(dest_ref, 0)

    @pl.when(i + 1 < pl.num_programs(0))
    def _():
        gather_tile(next_ref, 1 - slot)

    for k in range(TOP_K):
        pltpu.make_async_copy(y_hbm.at[pl.ds(0, tm), :], gbuf.at[slot, k], sem.at[slot]).wait()
    h = h_ref[...]
    for k in range(TOP_K):
        h = h + w_ref[:, k:k + 1] * gbuf[slot, k]
    o_ref[...] = h * lax.rsqrt(jnp.mean(h * h, axis=-1, keepdims=True) + EPS) * g_ref[...]


def _combine(dest_tiles, h1, w_cols, g_final, y):
    t = h1.shape[0]
    tm = dest_tiles.shape[2] // TOP_K
    d = D_MODEL
    last = t // tm - 1
    return pl.pallas_call(
        _combine_body,
        grid=(t // tm,),
        in_specs=[
            pl.BlockSpec((1, 1, TOP_K * tm), lambda i: (i, 0, 0), memory_space=pltpu.SMEM),
            pl.BlockSpec((1, 1, TOP_K * tm), lambda i: (jnp.minimum(i + 1, last), 0, 0), memory_space=pltpu.SMEM),
            pl.BlockSpec((tm, d), lambda i: (i, 0)),
            pl.BlockSpec((tm, TOP_K), lambda i: (i, 0)),
            pl.BlockSpec((1, d), lambda i: (0, 0)),
            pl.BlockSpec(memory_space=pltpu.HBM),
        ],
        out_specs=pl.BlockSpec((tm, d), lambda i: (i, 0)),
        out_shape=jax.ShapeDtypeStruct((t, d), F32),
        scratch_shapes=[pltpu.VMEM((N_SLOTS, TOP_K, tm, d), F32), pltpu.SemaphoreType.DMA((N_SLOTS,))],
        compiler_params=_params(("arbitrary",)),
        name="combine",
    )(dest_tiles, dest_tiles, h1, w_cols, g_final, y)


def _tile_major(a, tm):
    k, t = a.shape
    return a.reshape(k, t // tm, tm).transpose(1, 0, 2).reshape(t // tm, 1, k * tm)


def _routing_tables(top_e, rank, counts, t):
    bm = MOE_BM
    nb = (t * TOP_K) // bm + N_EXPERTS
    padded = (counts + bm - 1) // bm * bm
    pad_end = jnp.cumsum(padded)
    pad_start = pad_end - padded
    onehot = top_e[:, :, None] == jnp.arange(N_EXPERTS, dtype=jnp.int32)
    dest = rank + jnp.sum(jnp.where(onehot, pad_start, 0), axis=-1)
    first_blk = (pad_start // bm).astype(jnp.int32)
    n_blk = (padded // bm).astype(jnp.int32)
    n_used = (pad_end[-1] // bm).astype(jnp.int32).reshape(1)
    gap = padded - counts
    gap_end = jnp.cumsum(gap)
    j = jnp.arange(N_EXPERTS * bm, dtype=jnp.int32)
    ej = jnp.minimum(jnp.sum(j[:, None] >= gap_end[None, :], axis=1), N_EXPERTS - 1)
    pad_rows = (pad_start + counts)[ej] + j - (gap_end - gap)[ej]
    pad_rows = jnp.clip(pad_rows, 0, nb * bm - 1).astype(jnp.int32)
    n_pad = gap_end[-1].astype(jnp.int32).reshape(1)
    return dest.astype(jnp.int32), first_blk, n_blk, n_used, pad_rows, n_pad, nb * bm


def kernel(x, g_mix, w_in, ssm_conv_w, ssm_conv_b, ssm_dt_bias, ssm_a_log, ssm_d, ssm_norm_g, w_ssm_out,
           sc_conv_w, w_sc_out, b_gate, w_o, g_ffn, w_router, b_router, w_gate_up, b_gate_up, w_down,
           b_down, g_final):
    bsz, seq, d = x.shape
    t = bsz * seq
    assert bsz == 1 and d == D_MODEL and w_in.shape[0] == 1
    xt = x.reshape(t, d)
    w_dt = w_in[0, :, OFF_DT:OFF_SC]
    w_dt_hi = w_dt.astype(BF16)
    w_dt = jnp.concatenate([w_dt_hi, (w_dt - w_dt_hi.astype(F32)).astype(BF16)], axis=1)
    col = lambda a: a.reshape(-1, 1)
    row = lambda a: a.reshape(1, -1)

    u, dt_raw = _prenorm(xt, row(g_mix[0]), w_dt)
    proj = _inproj(u, w_in[0].T)
    y_norm = _ssd(proj, dt_raw.T, ssm_conv_w[0], row(ssm_conv_b[0]), col(ssm_dt_bias[0]), col(ssm_a_log[0]),
                  col(ssm_d[0]), row(ssm_norm_g[0]))
    mixed = _mix(y_norm, proj, row(b_gate[0]), sc_conv_w[0], w_ssm_out[0].astype(BF16), w_sc_out[0].astype(BF16))
    wr_t = w_router[0].T
    wr_hi = wr_t.astype(BF16)
    wr_lo = (wr_t - wr_hi.astype(F32)).astype(BF16)
    h1, xp, top_e, top_w, rank, counts = _route(mixed, xt, w_o[0].astype(BF16), row(g_ffn[0]),
                                                jnp.concatenate([wr_hi, wr_lo], axis=0), col(b_router[0]))
    dest, first_blk, n_blk, n_used, pad_rows, n_pad, n_rows = _routing_tables(top_e, rank, counts[:, 0], t)
    xs = _dispatch(xp, _tile_major(dest, min(DISPATCH_TM, t)), pad_rows, n_pad, n_used, n_rows)
    hid = _ffn_up(first_blk, n_blk, n_used, xs, w_gate_up[0], b_gate_up[0].reshape(N_EXPERTS, 1, 2 * D_FF))
    y = _ffn_down(first_blk, n_blk, n_used, hid, w_down[0],
                  b_down[0].reshape(N_EXPERTS, D_MODEL // FFN_CHUNK, 1, FFN_CHUNK))
    out = _combine(_tile_major(dest, min(COMBINE_TM, t)), h1, top_w.T, row(g_final), y)
    return out.reshape(bsz, seq, d)
```

```python
import functools

import jax
import jax.numpy as jnp
from jax import lax
from jax.experimental import pallas as pl
from jax.experimental.pallas import tpu as pltpu

D_MODEL = 2048
SSM_D_INNER = 2 * D_MODEL
SSM_HEAD_DIM = 64
SSM_N_HEADS = SSM_D_INNER // SSM_HEAD_DIM
SSM_N_GROUPS = 8
SSM_HEADS_PER_GROUP = SSM_N_HEADS // SSM_N_GROUPS
SSM_D_STATE = 128
SSM_CONV = 4
SSM_GN = SSM_N_GROUPS * SSM_D_STATE
SSM_CONV_DIM = SSM_D_INNER + 2 * SSM_GN
SSM_GROUP_CH = SSM_D_INNER // SSM_N_GROUPS
SC_DIM = D_MODEL
SC_WIDTH = 3
N_EXPERTS = 32
TOP_K = 4
D_FF = D_MODEL
SWIGLU_LIMIT = 7.0
SWIGLU_ALPHA = 1.702
EPS = 1e-5

OFF_Z = 0
OFF_XBC = OFF_Z + SSM_D_INNER
OFF_DT = OFF_XBC + SSM_CONV_DIM
OFF_SC = OFF_DT + SSM_N_HEADS
OFF_GATE = OFF_SC + 3 * SC_DIM
D_IN_PROJ = OFF_GATE + 2 * D_MODEL

P_Z = 0
P_XBC = P_Z + SSM_D_INNER
P_SC = P_XBC + SSM_CONV_DIM
P_GATE = P_SC + 3 * SC_DIM
P_TOTAL = P_GATE + 2 * D_MODEL

SUBLANES = 8
VMEM_LIMIT = 56 * 1024 * 1024

F32 = jnp.float32
BF16 = jnp.bfloat16
HIGHEST = lax.Precision.HIGHEST
NT_DIMS = (((1,), (1,)), ((), ()))
TN_DIMS = (((0,), (0,)), ((), ()))


def _sigmoid(v):
    return 0.5 * jnp.tanh(0.5 * v) + 0.5


def _params(semantics):
    return pltpu.CompilerParams(dimension_semantics=semantics, vmem_limit_bytes=VMEM_LIMIT)


PRENORM_TM = 512
INPROJ_TM = 2048
INPROJ_TN = 1024
LANES = 128
DT_SHIFT = OFF_SC - OFF_DT
ALIGNED_TILES = OFF_DT // INPROJ_TN
CAST_ROWS = 256
EPILOGUE_ROWS = 256


def _prenorm_body(x_ref, g_ref, wdt_ref, u_ref, dt_ref):
    x = x_ref[...]
    u = x * lax.rsqrt(jnp.mean(x * x, axis=-1, keepdims=True) + EPS) * g_ref[...]
    u_hi = u.astype(BF16)
    u_ref[...] = u_hi
    u_lo = (u - u_hi.astype(F32)).astype(BF16)
    both = jnp.dot(u_hi, wdt_ref[...], preferred_element_type=F32)
    cross = jnp.dot(u_lo, wdt_ref[...], preferred_element_type=F32)
    nh = SSM_N_HEADS
    dt_ref[...] = both[:, :nh] + both[:, nh:] + cross[:, :nh]


def _prenorm(x, g, w_dt):
    t = x.shape[0]
    tm = min(PRENORM_TM, t)
    return pl.pallas_call(
        _prenorm_body,
        grid=(t // tm,),
        in_specs=[
            pl.BlockSpec((tm, D_MODEL), lambda i: (i, 0)),
            pl.BlockSpec((1, D_MODEL), lambda i: (0, 0)),
            pl.BlockSpec((D_MODEL, 2 * SSM_N_HEADS), lambda i: (0, 0)),
        ],
        out_specs=[
            pl.BlockSpec((tm, D_MODEL), lambda i: (i, 0)),
            pl.BlockSpec((tm, SSM_N_HEADS), lambda i: (i, 0)),
        ],
        out_shape=[
            jax.ShapeDtypeStruct((t, D_MODEL), BF16),
            jax.ShapeDtypeStruct((t, SSM_N_HEADS), F32),
        ],
        compiler_params=_params(("parallel",)),
        name="prenorm",
    )(x, g, w_dt)


def _inproj_body(u_ref, wt_ref, proj_ref, w_bf):
    j, i = pl.program_id(0), pl.program_id(1)
    tm = u_ref.shape[0]
    z_tiles = P_XBC // w_bf.shape[0]

    @pl.when(i == 0)
    def _():
        for r in range(0, w_bf.shape[0], CAST_ROWS):
            w_bf[r:r + CAST_ROWS, :] = wt_ref[r:r + CAST_ROWS, :].astype(BF16)

    def project(r0, rows):
        return lax.dot_general(u_ref[r0:r0 + rows, :], w_bf[...], NT_DIMS, preferred_element_type=F32)

    chunk = min(EPILOGUE_ROWS, tm)

    @pl.when(j < z_tiles)
    def _():
        for r0 in range(0, tm, chunk):
            acc = project(r0, chunk)
            proj_ref[r0:r0 + chunk, :] = (acc * _sigmoid(acc)).astype(BF16)

    @pl.when(j >= z_tiles)
    def _():
        proj_ref[...] = project(0, tm).astype(BF16)


def _inproj(u, w_in_t):
    t = u.shape[0]
    tm = min(INPROJ_TM, t)
    tn = INPROJ_TN
    assert OFF_DT % tn == 0 and DT_SHIFT % SUBLANES == 0 and P_XBC % tn == 0
    first_row = lambda j: pl.multiple_of(j * tn + jnp.where(j >= ALIGNED_TILES, DT_SHIFT, 0), SUBLANES)
    return pl.pallas_call(
        _inproj_body,
        grid=(P_TOTAL // tn, t // tm),
        in_specs=[
            pl.BlockSpec((tm, D_MODEL), lambda j, i: (i, 0)),
            pl.BlockSpec((pl.Element(tn), pl.Element(D_MODEL)), lambda j, i: (first_row(j), 0)),
        ],
        out_specs=pl.BlockSpec((tm, tn), lambda j, i: (i, j)),
        out_shape=jax.ShapeDtypeStruct((t, P_TOTAL), BF16),
        scratch_shapes=[pltpu.VMEM((tn, D_MODEL), BF16)],
        compiler_params=_params(("arbitrary", "arbitrary")),
        name="inproj",
    )(u, w_in_t)


SSD_L = 256
SSD_GROUPS = 4
HEAD_PAIR = 2 * SSM_HEAD_DIM


def _ssd_body(z_ref, x_ref, b_ref, c_ref, dt_ref, widen_ref, wx_ref, wb_ref, wc_ref, bx_ref, bb_ref, bc_ref,
              dtb_ref, alog_ref, d_ref, ng_ref, o_ref, s_ref, xbuf, bbuf, cbuf):
    L = x_ref.shape[0]
    tail = SUBLANES

    @pl.when(pl.program_id(1) == 0)
    def _():
        s_ref[...] = jnp.zeros_like(s_ref)
        xbuf[0:tail, :] = jnp.zeros((tail, xbuf.shape[1]), F32)
        bbuf[0:tail, :] = jnp.zeros((tail, bbuf.shape[1]), F32)
        cbuf[0:tail, :] = jnp.zeros((tail, cbuf.shape[1]), F32)

    def conv_silu(buf, in_ref, w_ref, bias_ref):
        buf[tail:tail + L, :] = in_ref[...].astype(F32)
        acc = bias_ref[...] + w_ref[SSM_CONV - 1:SSM_CONV, :] * buf[tail:tail + L, :]
        for k in range(SSM_CONV - 1):
            off = tail - (SSM_CONV - 1) + k
            acc = acc + w_ref[k:k + 1, :] * buf[off:off + L, :]
        buf[0:tail, :] = buf[L:L + tail, :]
        return acc * _sigmoid(acc)

    xs_all = conv_silu(xbuf, x_ref, wx_ref, bx_ref)
    bm_all = conv_silu(bbuf, b_ref, wb_ref, bb_ref).astype(BF16)
    cm_all = conv_silu(cbuf, c_ref, wc_ref, bc_ref).astype(BF16)

    dt_raw = dt_ref[...] + dtb_ref[...]
    dt_all = jnp.maximum(dt_raw, 0.0) + jnp.log(1.0 + jnp.exp(-jnp.abs(dt_raw)))
    da_all = dt_all * (-jnp.exp(alog_ref[...]))
    hg = SSM_HEADS_PER_GROUP
    gc = SSM_GROUP_CH
    n = SSM_D_STATE
    row = lax.broadcasted_iota(jnp.int32, (L, L), 0)
    col = lax.broadcasted_iota(jnp.int32, (L, L), 1)
    causal = row >= col
    incl = (row <= col).astype(BF16)
    lane = lax.broadcasted_iota(jnp.int32, (L, HEAD_PAIR), 1)
    first = lane < SSM_HEAD_DIM

    def pieces(a):
        hi = a.astype(BF16).astype(F32)
        mid = (a - hi).astype(BF16).astype(F32)
        lo = (a - hi - mid).astype(BF16).astype(F32)
        return hi, mid, lo

    for gi in range(x_ref.shape[1] // gc):
        ch = slice(gi * gc, (gi + 1) * gc)
        st = slice(gi * n, (gi + 1) * n)
        hd = slice(gi * hg, (gi + 1) * hg)
        xs, bm, cm, dt = xs_all[:, ch], bm_all[:, st], cm_all[:, st], dt_all[hd, :]
        da_p = pieces(da_all[hd, :])
        parts = jnp.dot(jnp.concatenate(da_p, axis=0).astype(BF16), incl, preferred_element_type=F32)
        cs = parts[0:hg] + parts[hg:2 * hg] + parts[2 * hg:3 * hg]
        cs_end = cs[:, L - 1:L]
        split = [pieces(a) for a in (dt, jnp.exp(cs_end - cs), jnp.exp(cs))]
        stacked = jnp.concatenate([split[q][s] for s in range(3) for q in range(3)] + [cs], axis=0)
        flipped = stacked.T
        cs_t = flipped[:, 9 * hg:10 * hg]
        wide = jnp.dot(flipped.astype(BF16), widen_ref[...], preferred_element_type=F32)
        dt_x, to_end_x, ecs_x = wide[:, 0:gc], wide[:, gc:2 * gc], wide[:, 2 * gc:3 * gc]

        cb = lax.dot_general(cm, bm, NT_DIMS, preferred_element_type=F32)
        cb = jnp.where(causal, cb, 0.0)
        y_off = jnp.dot(cm, s_ref[gi].astype(BF16), preferred_element_type=F32)

        xdt = xs * dt_x
        xdt_b = xdt.astype(BF16)
        ys = []
        for p in range(hg // 2):
            sl = slice(p * HEAD_PAIR, (p + 1) * HEAD_PAIR)
            yd = []
            for h in (2 * p, 2 * p + 1):
                seg = cs_t[:, h:h + 1] - cs[h:h + 1, :]
                m = cb * jnp.exp(jnp.minimum(seg, 0.0))
                yd.append(jnp.dot(m.astype(BF16), xdt_b[:, sl], preferred_element_type=F32))
            ys.append(jnp.where(first, yd[0], yd[1]))
        y = jnp.concatenate(ys, axis=1) + y_off * ecs_x + xs * d_ref[:, ch]

        xw = (xdt * to_end_x).astype(BF16)
        upd = lax.dot_general(bm, xw, TN_DIMS, preferred_element_type=F32)
        s_ref[gi] = s_ref[gi] * ecs_x[L - 1:L, :] + upd

        v = y * z_ref[:, ch].astype(F32)
        v = v * lax.rsqrt(jnp.mean(v * v, axis=-1, keepdims=True) + EPS)
        o_ref[:, ch] = (v * ng_ref[:, ch]).astype(BF16)


def _ssd(proj, dt_rows, conv_w, conv_b, dt_bias, a_log, d_skip, norm_g):
    t = proj.shape[0]
    L = min(SSD_L, t)
    gc = SSM_GROUP_CH
    n = SSM_D_STATE
    zc, xc = P_Z // gc, P_XBC // gc
    bc, cc = (P_XBC + SSM_D_INNER) // n, (P_XBC + SSM_D_INNER + SSM_GN) // n
    wbc, wcc = SSM_D_INNER // n, (SSM_D_INNER + SSM_GN) // n
    hg = SSM_HEADS_PER_GROUP
    n_factors, n_pieces = 3, 3
    r = jnp.arange((n_factors * n_pieces + 1) * hg)[:, None]
    c = jnp.arange(n_factors * gc)[None, :]
    widen = ((r < n_factors * n_pieces * hg)
             & (r % (n_factors * hg) == (c // gc) * hg + (c % gc) // SSM_HEAD_DIM)).astype(BF16)
    d_wide = jnp.repeat(d_skip.reshape(-1), SSM_HEAD_DIM).reshape(1, -1)
    k = SSD_GROUPS
    gc, n, hg = k * gc, k * n, k * hg
    zc, xc, bc, cc, wbc, wcc = zc // k, xc // k, bc // k, cc // k, wbc // k, wcc // k
    return pl.pallas_call(
        _ssd_body,
        grid=(SSM_N_GROUPS // k, t // L),
        in_specs=[
            pl.BlockSpec((L, gc), lambda g, i: (i, zc + g)),
            pl.BlockSpec((L, gc), lambda g, i: (i, xc + g)),
            pl.BlockSpec((L, n), lambda g, i: (i, bc + g)),
            pl.BlockSpec((L, n), lambda g, i: (i, cc + g)),
            pl.BlockSpec((hg, L), lambda g, i: (g, i)),
            pl.BlockSpec(widen.shape, lambda g, i: (0, 0)),
            pl.BlockSpec((SSM_CONV, gc), lambda g, i: (0, g)),
            pl.BlockSpec((SSM_CONV, n), lambda g, i: (0, wbc + g)),
            pl.BlockSpec((SSM_CONV, n), lambda g, i: (0, wcc + g)),
            pl.BlockSpec((1, gc), lambda g, i: (0, g)),
            pl.BlockSpec((1, n), lambda g, i: (0, wbc + g)),
            pl.BlockSpec((1, n), lambda g, i: (0, wcc + g)),
            pl.BlockSpec((hg, 1), lambda g, i: (g, 0)),
            pl.BlockSpec((hg, 1), lambda g, i: (g, 0)),
            pl.BlockSpec((1, gc), lambda g, i: (0, g)),
            pl.BlockSpec((1, gc), lambda g, i: (0, g)),
        ],
        out_specs=pl.BlockSpec((L, gc), lambda g, i: (i, g)),
        out_shape=jax.ShapeDtypeStruct((t, SSM_D_INNER), BF16),
        scratch_shapes=[
            pltpu.VMEM((k, SSM_D_STATE, SSM_GROUP_CH), F32),
            pltpu.VMEM((L + SUBLANES, gc), F32),
            pltpu.VMEM((L + SUBLANES, n), F32),
            pltpu.VMEM((L + SUBLANES, n), F32),
        ],
        compiler_params=_params(("parallel", "arbitrary")),
        name="ssd",
    )(proj, proj, proj, proj, dt_rows, widen, conv_w, conv_w, conv_w, conv_b, conv_b, conv_b,
      dt_bias, a_log, d_wide, norm_g)


MIX_TM = 256


def _mix_body(yn_ref, b_ref, c_ref, v_ref, cp_ref, vp_ref, g1_ref, g2_ref, bg1_ref, bg2_ref,
              wc_ref, wssm_ref, wsc_ref, o_ref, buf):
    tm = yn_ref.shape[0]
    tail = SUBLANES
    prev = cp_ref[...].astype(F32) * vp_ref[...].astype(F32)
    buf[0:tail, :] = jnp.where(pl.program_id(0) == 0, 0.0, prev)
    cv = c_ref[...].astype(F32) * v_ref[...].astype(F32)
    buf[tail:tail + tm, :] = cv
    conv = wc_ref[SC_WIDTH - 1:SC_WIDTH, :] * cv
    for k in range(SC_WIDTH - 1):
        off = tail - (SC_WIDTH - 1) + k
        conv = conv + wc_ref[k:k + 1, :] * buf[off:off + tm, :]
    sc_in = (b_ref[...].astype(F32) * conv).astype(BF16)
    y_sc = jnp.dot(sc_in, wsc_ref[...], preferred_element_type=F32)
    y_ssm = jnp.dot(yn_ref[...], wssm_ref[...], preferred_element_type=F32)
    g1 = _sigmoid(g1_ref[...].astype(F32) + bg1_ref[...])
    g2 = _sigmoid(g2_ref[...].astype(F32) + bg2_ref[...])
    o_ref[...] = (g1 * y_ssm + g2 * y_sc).astype(BF16)


def _resident(shape):
    return pl.BlockSpec(shape, lambda *_: (0,) * len(shape), pipeline_mode=pl.Buffered(1))


def _mix(y_norm, proj, b_gate, sc_conv_w, w_ssm_out, w_sc_out):
    t = y_norm.shape[0]
    tm = min(MIX_TM, t)
    d = D_MODEL
    sb, gb = P_SC // d, P_GATE // d
    prev_rows = lambda i: jnp.maximum(i * (tm // SUBLANES) - 1, 0)
    return pl.pallas_call(
        _mix_body,
        grid=(t // tm,),
        in_specs=[
            pl.BlockSpec((tm, SSM_D_INNER), lambda i: (i, 0)),
            pl.BlockSpec((tm, d), lambda i: (i, sb)),
            pl.BlockSpec((tm, d), lambda i: (i, sb + 1)),
            pl.BlockSpec((tm, d), lambda i: (i, sb + 2)),
            pl.BlockSpec((SUBLANES, d), lambda i: (prev_rows(i), sb + 1)),
            pl.BlockSpec((SUBLANES, d), lambda i: (prev_rows(i), sb + 2)),
            pl.BlockSpec((tm, d), lambda i: (i, gb)),
            pl.BlockSpec((tm, d), lambda i: (i, gb + 1)),
            pl.BlockSpec((1, d), lambda i: (0, 0)),
            pl.BlockSpec((1, d), lambda i: (0, 1)),
            pl.BlockSpec((SC_WIDTH, d), lambda i: (0, 0)),
            _resident((SSM_D_INNER, d)),
            _resident((d, d)),
        ],
        out_specs=pl.BlockSpec((tm, d), lambda i: (i, 0)),
        out_shape=jax.ShapeDtypeStruct((t, d), BF16),
        scratch_shapes=[pltpu.VMEM((tm + SUBLANES, d), F32)],
        compiler_params=_params(("parallel",)),
        name="mix",
    )(y_norm, proj, proj, proj, proj, proj, proj, proj, b_gate, b_gate, sc_conv_w, w_ssm_out, w_sc_out)


ROUTE_TM = 512
PACK_W = D_MODEL // 2


def _pack_rows(v):
    lo = lax.bitcast_convert_type(v[:, :PACK_W].astype(F32), jnp.uint32)
    hi = lax.bitcast_convert_type(v[:, PACK_W:].astype(F32), jnp.uint32)
    return hi | (lo >> 16)


def _unpack_rows(w):
    lo = lax.bitcast_convert_type(w << 16, F32).astype(BF16)
    hi = lax.bitcast_convert_type(w & jnp.uint32(0xFFFF0000), F32).astype(BF16)
    return jnp.concatenate([lo, hi], axis=1)


def _route_body(m_ref, x_ref, wo_ref, g_ref, wr_ref, br_ref,
                h_ref, xp_ref, e_ref, w_ref, r_ref, cnt_ref, carry):
    tm = m_ref.shape[0]

    @pl.when(pl.program_id(0) == 0)
    def _():
        carry[...] = jnp.zeros_like(carry)

    h = x_ref[...] + jnp.dot(m_ref[...], wo_ref[...], preferred_element_type=F32)
    h_ref[...] = h
    xn = h * lax.rsqrt(jnp.mean(h * h, axis=-1, keepdims=True) + EPS) * g_ref[...]
    xn_hi = xn.astype(BF16)
    xn_lo = (xn - xn_hi.astype(F32)).astype(BF16)
    xp_ref[...] = _pack_rows(xn_hi)

    ne = N_EXPERTS
    both = lax.dot_general(wr_ref[...], xn_hi, NT_DIMS, preferred_element_type=F32)
    cross = lax.dot_general(wr_ref[0:ne, :], xn_lo, NT_DIMS, preferred_element_type=F32)
    logits = both[0:ne] + both[ne:2 * ne] + cross + br_ref[...]
    eidx = lax.broadcasted_iota(jnp.int32, (N_EXPERTS, tm), 0)
    vals, hots = [], []
    for k in range(TOP_K):
        best = jnp.max(logits, axis=0, keepdims=True)
        arg = jnp.min(jnp.where(logits == best, eidx, N_EXPERTS), axis=0, keepdims=True)
        hot = eidx == arg
        e_ref[k:k + 1, :] = arg
        vals.append(best)
        hots.append(hot)
        logits = jnp.where(hot, -jnp.inf, logits)
    exps = [jnp.exp(v - vals[0]) for v in vals]
    denom = exps[0] + exps[1] + exps[2] + exps[3]
    for k in range(TOP_K):
        w_ref[k:k + 1, :] = exps[k] / denom

    cnt = (hots[0] | hots[1] | hots[2] | hots[3]).astype(F32)
    r_i = lax.broadcasted_iota(jnp.int32, (tm, tm), 0)
    c_i = lax.broadcasted_iota(jnp.int32, (tm, tm), 1)
    before = (r_i < c_i).astype(BF16)
    prior = carry[:, 0:1] + jnp.dot(cnt.astype(BF16), before, preferred_element_type=F32)
    for k in range(TOP_K):
        r_ref[k:k + 1, :] = jnp.sum(jnp.where(hots[k], prior, 0.0), axis=0, keepdims=True).astype(jnp.int32)
    total = carry[...] + jnp.sum(cnt, axis=1, keepdims=True)
    carry[...] = total
    cnt_ref[...] = total.astype(jnp.int32)


def _route(mixed, x, w_o, g_ffn, w_router_t, b_router):
    t = x.shape[0]
    tm = min(ROUTE_TM, t)
    d = D_MODEL
    return pl.pallas_call(
        _route_body,
        grid=(t // tm,),
        in_specs=[
            pl.BlockSpec((tm, d), lambda i: (i, 0)),
            pl.BlockSpec((tm, d), lambda i: (i, 0)),
            _resident((d, d)),
            pl.BlockSpec((1, d), lambda i: (0, 0)),
            pl.BlockSpec((2 * N_EXPERTS, d), lambda i: (0, 0)),
            pl.BlockSpec((N_EXPERTS, 1), lambda i: (0, 0)),
        ],
        out_specs=[
            pl.BlockSpec((tm, d), lambda i: (i, 0)),
            pl.BlockSpec((tm, PACK_W), lambda i: (i, 0)),
            pl.BlockSpec((TOP_K, tm), lambda i: (0, i)),
            pl.BlockSpec((TOP_K, tm), lambda i: (0, i)),
            pl.BlockSpec((TOP_K, tm), lambda i: (0, i)),
            pl.BlockSpec((N_EXPERTS, 128), lambda i: (0, 0)),
        ],
        out_shape=[
            jax.ShapeDtypeStruct((t, d), F32),
            jax.ShapeDtypeStruct((t, PACK_W), jnp.uint32),
            jax.ShapeDtypeStruct((TOP_K, t), jnp.int32),
            jax.ShapeDtypeStruct((TOP_K, t), F32),
            jax.ShapeDtypeStruct((TOP_K, t), jnp.int32),
            jax.ShapeDtypeStruct((N_EXPERTS, 128), jnp.int32),
        ],
        scratch_shapes=[pltpu.VMEM((N_EXPERTS, 128), F32)],
        compiler_params=_params(("arbitrary",)),
        name="route",
    )(mixed, x, w_o, g_ffn, w_router_t, b_router)


MOE_BM = 256
DISPATCH_TM = 256


def _dispatch_body(dest_ref, padrow_ref, npad_ref, nused_ref, xp_ref, xs_hbm, zblk, sem, zsem):
    tm = xp_ref.shape[0]
    bm = zblk.shape[0]
    nb = xs_hbm.shape[0] // bm

    @pl.when(pl.program_id(0) == 0)
    def _():
        zblk[...] = jnp.zeros_like(zblk)
        npad = npad_ref[0]
        nused = nused_ref[0]

        def zero_row(j):
            return pltpu.make_async_copy(zblk.at[pl.ds(0, 1), :], xs_hbm.at[pl.ds(padrow_ref[j], 1), :], zsem)

        def zero_block(b):
            return pltpu.make_async_copy(zblk, xs_hbm.at[pl.ds(pl.multiple_of(b * bm, bm), bm), :], zsem)

        def start_row(j, c):
            zero_row(j).start()
            return c

        def wait_row(j, c):
            zero_row(j).wait()
            return c

        def start_block(b, c):
            zero_block(b).start()
            return c

        def wait_block(b, c):
            zero_block(b).wait()
            return c

        lax.fori_loop(0, npad, start_row, 0)
        lax.fori_loop(nused, nb, start_block, 0)
        lax.fori_loop(0, npad, wait_row, 0)
        lax.fori_loop(nused, nb, wait_block, 0)

    def row_copy(r, k):
        return pltpu.make_async_copy(xp_ref.at[pl.ds(r, 1), :],
                                     xs_hbm.at[pl.ds(dest_ref[0, 0, k * tm + r], 1), :], sem)

    def start(r, c):
        for k in range(TOP_K):
            row_copy(r, k).start(priority=k % 2)
        return c

    lax.fori_loop(0, tm, start, 0)
    for k in range(TOP_K):
        pltpu.make_async_copy(xp_ref, xs_hbm.at[pl.ds(0, tm), :], sem).wait()


def _dispatch(xp, dest_tiles, pad_rows, n_pad, n_used, n_rows):
    t = xp.shape[0]
    tm = dest_tiles.shape[2] // TOP_K
    return pl.pallas_call(
        _dispatch_body,
        grid=(t // tm,),
        in_specs=[
            pl.BlockSpec((1, 1, TOP_K * tm), lambda i: (i, 0, 0), memory_space=pltpu.SMEM),
            pl.BlockSpec(memory_space=pltpu.SMEM),
            pl.BlockSpec(memory_space=pltpu.SMEM),
            pl.BlockSpec(memory_space=pltpu.SMEM),
            pl.BlockSpec((tm, PACK_W), lambda i: (i, 0)),
        ],
        out_specs=pl.BlockSpec(memory_space=pltpu.HBM),
        out_shape=jax.ShapeDtypeStruct((n_rows, PACK_W), jnp.uint32),
        scratch_shapes=[
            pltpu.VMEM((MOE_BM, PACK_W), jnp.uint32),
            pltpu.SemaphoreType.DMA(()),
            pltpu.SemaphoreType.DMA(()),
        ],
        compiler_params=_params(("arbitrary",)),
        name="dispatch",
    )(dest_tiles, pad_rows, n_pad, n_used, xp)


FFN_TF = 1024
FFN_TN = 2048
FFN_CHUNK = 2048
N_SLOTS = 2
IN_SLOTS = 3
STREAM_PRIORITY = 1


def _stream_expert_blocks(first_ref, nblk_ref, nused_ref, src_hbm, dst_hbm, ibuf, obuf, isem, osem,
                          prepare, compute):
    j, e = pl.program_id(0), pl.program_id(1)
    n_in = ibuf.shape[0]
    ahead = n_in - 1
    bm_in = ibuf.shape[1]
    n_chunks, bm_out, cw = obuf.shape[1:]
    width = n_chunks * cw
    half = bm_in // 2
    col = pl.multiple_of(j * width, width)
    nblk = nblk_ref[e]
    first = first_ref[e]

    def fetch(b, slot):
        src = src_hbm.at[pl.ds(pl.multiple_of(b * bm_in, bm_in), bm_in), :]
        return pltpu.make_async_copy(src, ibuf.at[slot], isem.at[slot])

    def fetch_start(b, slot):
        for p in range(2):
            src = src_hbm.at[pl.ds(pl.multiple_of(b * bm_in + p * half, half), half), :]
            pltpu.make_async_copy(src, ibuf.at[slot, pl.ds(p * half, half), :], isem.at[slot]).start(priority=p)

    def flush(b, slot):
        out_rows = pl.ds(pl.multiple_of(b * bm_out, bm_out), bm_out)
        return [pltpu.make_async_copy(obuf.at[slot, n], dst_hbm.at[out_rows, pl.ds(col + n * cw, cw)], osem.at[slot])
                for n in range(n_chunks)]

    def flush_start(b, slot):
        for cp in flush(b, slot):
            cp.start(priority=STREAM_PRIORITY)

    def flush_wait(b, slot):
        for cp in flush(b, slot):
            cp.wait()

    for d in range(ahead):
        @pl.when(nblk > d)
        def _():
            fetch_start(first + d, d)

    @pl.when(nblk > 0)
    def _():
        prepare()

    def step(i, carry):
        slot = lax.rem(i, n_in)
        oslot = lax.rem(i, N_SLOTS)
        fetch(first + i, slot).wait()

        @pl.when(i + ahead < nblk)
        def _():
            fetch_start(first + i + ahead, lax.rem(i + ahead, n_in))

        @pl.when(i >= N_SLOTS)
        def _():
            flush_wait(first + i - N_SLOTS, oslot)

        compute(ibuf[slot], obuf.at[oslot])
        flush_start(first + i, oslot)
        return carry

    lax.fori_loop(0, nblk, step, 0)

    @pl.when(nblk >= 2)
    def _():
        flush_wait(first + nblk - 2, lax.rem(nblk, N_SLOTS))

    @pl.when(nblk >= 1)
    def _():
        flush_wait(first + nblk - 1, lax.rem(nblk - 1, N_SLOTS))

    @pl.when(e == pl.num_programs(1) - 1)
    def _():
        obuf[0] = jnp.zeros(obuf.shape[1:], obuf.dtype)

        def zero(b, carry):
            flush_start(b, 0)
            flush_wait(b, 0)
            return carry

        lax.fori_loop(nused_ref[0], dst_hbm.shape[0] // bm_out, zero, 0)


def _ffn_up_body(first_ref, nblk_ref, nused_ref, xs_hbm, wg_ref, wu_ref, bg_ref, bu_ref, h_hbm,
                 wg_bf, wu_bf, ibuf, obuf, isem, osem):
    def prepare():
        wg_bf[...] = wg_ref[...].astype(BF16)
        wu_bf[...] = wu_ref[...].astype(BF16)

    def compute(packed, out_ref):
        x = _unpack_rows(packed)
        gate = jnp.dot(x, wg_bf[...], preferred_element_type=F32) + bg_ref[...]
        up = jnp.dot(x, wu_bf[...], preferred_element_type=F32) + bu_ref[...]
        gate = jnp.minimum(gate, SWIGLU_LIMIT)
        up = jnp.clip(up, -SWIGLU_LIMIT, SWIGLU_LIMIT)
        act = ((up + 1.0) * gate * _sigmoid(SWIGLU_ALPHA * gate)).astype(BF16)
        out_ref[0] = pltpu.bitcast(act, jnp.uint32)

    _stream_expert_blocks(first_ref, nblk_ref, nused_ref, xs_hbm, h_hbm, ibuf, obuf, isem, osem,
                          prepare, compute)


def _ffn_up(first_blk, n_blk, n_used, xs, w_gate_up, b_gate_up):
    n_rows = xs.shape[0]
    tf = FFN_TF
    nf = D_FF // tf
    return pl.pallas_call(
        _ffn_up_body,
        grid_spec=pltpu.PrefetchScalarGridSpec(
            num_scalar_prefetch=3,
            grid=(nf, N_EXPERTS),
            in_specs=[
                pl.BlockSpec(memory_space=pltpu.HBM),
                pl.BlockSpec((None, D_MODEL, tf), lambda j, e, *_: (e, 0, j)),
                pl.BlockSpec((None, D_MODEL, tf), lambda j, e, *_: (e, 0, nf + j)),
                pl.BlockSpec((None, 1, tf), lambda j, e, *_: (e, 0, j)),
                pl.BlockSpec((None, 1, tf), lambda j, e, *_: (e, 0, nf + j)),
            ],
            out_specs=pl.BlockSpec(memory_space=pltpu.HBM),
            scratch_shapes=[
                pltpu.VMEM((D_MODEL, tf), BF16),
                pltpu.VMEM((D_MODEL, tf), BF16),
                pltpu.VMEM((IN_SLOTS, MOE_BM, PACK_W), jnp.uint32),
                pltpu.VMEM((N_SLOTS, 1, MOE_BM // 2, tf), jnp.uint32),
                pltpu.SemaphoreType.DMA((IN_SLOTS,)),
                pltpu.SemaphoreType.DMA((N_SLOTS,)),
            ],
        ),
        out_shape=jax.ShapeDtypeStruct((n_rows // 2, D_FF), jnp.uint32),
        compiler_params=_params(("arbitrary", "arbitrary")),
        name="ffn_up",
    )(first_blk, n_blk, n_used, xs, w_gate_up, w_gate_up, b_gate_up, b_gate_up)


def _ffn_down_body(first_ref, nblk_ref, nused_ref, h_hbm, wd_ref, bd_ref, y_hbm, wd_bf, ibuf, obuf, isem, osem):
    n_chunks, _, cw = wd_bf.shape

    def prepare():
        for n in range(n_chunks):
            wd_bf[n] = wd_ref[:, n * cw:(n + 1) * cw].astype(BF16)

    def compute(paired, out_ref):
        hid = pltpu.bitcast(paired, BF16)

        def chunk(n, carry):
            out_ref[n] = jnp.dot(hid, wd_bf[n], preferred_element_type=F32) + bd_ref[n]
            return carry

        lax.fori_loop(0, n_chunks, chunk, 0)

    _stream_expert_blocks(first_ref, nblk_ref, nused_ref, h_hbm, y_hbm, ibuf, obuf, isem, osem,
                          prepare, compute)


def _ffn_down(first_blk, n_blk, n_used, h, w_down, b_down):
    n_rows = 2 * h.shape[0]
    tn = FFN_TN
    cw = FFN_CHUNK
    nc = tn // cw
    return pl.pallas_call(
        _ffn_down_body,
        grid_spec=pltpu.PrefetchScalarGridSpec(
            num_scalar_prefetch=3,
            grid=(D_MODEL // tn, N_EXPERTS),
            in_specs=[
                pl.BlockSpec(memory_space=pltpu.HBM),
                pl.BlockSpec((None, D_FF, tn), lambda j, e, *_: (e, 0, j)),
                pl.BlockSpec((None, nc, 1, cw), lambda j, e, *_: (e, j, 0, 0)),
            ],
            out_specs=pl.BlockSpec(memory_space=pltpu.HBM),
            scratch_shapes=[
                pltpu.VMEM((nc, D_FF, cw), BF16),
                pltpu.VMEM((IN_SLOTS, MOE_BM // 2, D_FF), jnp.uint32),
                pltpu.VMEM((N_SLOTS, nc, MOE_BM, cw), F32),
                pltpu.SemaphoreType.DMA((IN_SLOTS,)),
                pltpu.SemaphoreType.DMA((N_SLOTS,)),
            ],
        ),
        out_shape=jax.ShapeDtypeStruct((n_rows, D_MODEL), F32),
        compiler_params=_params(("arbitrary", "arbitrary")),
        name="ffn_down",
    )(first_blk, n_blk, n_used, h, w_down, b_down)


COMBINE_TM = 256


def _combine_body(dest_ref, next_ref, h_ref, w_ref, g_ref, y_hbm, o_ref, gbuf, sem):
    i = pl.program_id(0)
    tm = h_ref.shape[0]

    def gather_tile(table_ref, slot):
        def start(r, c):
            for k in range(TOP_K):
                pltpu.make_async_copy(y_hbm.at[pl.ds(table_ref[0, 0, k * tm + r], 1), :],
                                      gbuf.at[slot, k, pl.ds(r, 1), :], sem.at[slot]).start(priority=k % 2)
            return c

        lax.fori_loop(0, tm, start, 0)

    slot = lax.rem(i, N_SLOTS)

    @pl.when(i == 0)
    def _():
        gather_tile(dest_ref, 0)

    @pl.when(i + 1 < pl.num_programs(0))
    def _():
        gather_tile(next_ref, 1 - slot)

    for k in range(TOP_K):
        pltpu.make_async_copy(y_hbm.at[pl.ds(0, tm), :], gbuf.at[slot, k], sem.at[slot]).wait()
    h = h_ref[...]
    for k in range(TOP_K):
        h = h + w_ref[:, k:k + 1] * gbuf[slot, k]
    o_ref[...] = h * lax.rsqrt(jnp.mean(h * h, axis=-1, keepdims=True) + EPS) * g_ref[...]


def _combine(dest_tiles, h1, w_cols, g_final, y):
    t = h1.shape[0]
    tm = dest_tiles.shape[2] // TOP_K
    d = D_MODEL
    last = t // tm - 1
    return pl.pallas_call(
        _combine_body,
        grid=(t // tm,),
        in_specs=[
            pl.BlockSpec((1, 1, TOP_K * tm), lambda i: (i, 0, 0), memory_space=pltpu.SMEM),
            pl.BlockSpec((1, 1, TOP_K * tm), lambda i: (jnp.minimum(i + 1, last), 0, 0), memory_space=pltpu.SMEM),
            pl.BlockSpec((tm, d), lambda i: (i, 0)),
            pl.BlockSpec((tm, TOP_K), lambda i: (i, 0)),
            pl.BlockSpec((1, d), lambda i: (0, 0)),
            pl.BlockSpec(memory_space=pltpu.HBM),
        ],
        out_specs=pl.BlockSpec((tm, d), lambda i: (i, 0)),
        out_shape=jax.ShapeDtypeStruct((t, d), F32),
        scratch_shapes=[pltpu.VMEM((N_SLOTS, TOP_K, tm, d), F32), pltpu.SemaphoreType.DMA((N_SLOTS,))],
        compiler_params=_params(("arbitrary",)),
        name="combine",
    )(dest_tiles, dest_tiles, h1, w_cols, g_final, y)


def _tile_major(a, tm):
    k, t = a.shape
    return a.reshape(k, t // tm, tm).transpose(1, 0, 2).reshape(t // tm, 1, k * tm)


def _routing_tables(top_e, rank, counts, t):
    bm = MOE_BM
    nb = (t * TOP_K) // bm + N_EXPERTS
    padded = (counts + bm - 1) // bm * bm
    pad_end = jnp.cumsum(padded)
    pad_start = pad_end - padded
    onehot = top_e[:, :, None] == jnp.arange(N_EXPERTS, dtype=jnp.int32)
    dest = rank + jnp.sum(jnp.where(onehot, pad_start, 0), axis=-1)
    first_blk = (pad_start // bm).astype(jnp.int32)
    n_blk = (padded // bm).astype(jnp.int32)
    n_used = (pad_end[-1] // bm).astype(jnp.int32).reshape(1)
    gap = padded - counts
    gap_end = jnp.cumsum(gap)
    j = jnp.arange(N_EXPERTS * bm, dtype=jnp.int32)
    ej = jnp.minimum(jnp.sum(j[:, None] >= gap_end[None, :], axis=1), N_EXPERTS - 1)
    pad_rows = (pad_start + counts)[ej] + j - (gap_end - gap)[ej]
    pad_rows = jnp.clip(pad_rows, 0, nb * bm - 1).astype(jnp.int32)
    n_pad = gap_end[-1].astype(jnp.int32).reshape(1)
    return dest.astype(jnp.int32), first_blk, n_blk, n_used, pad_rows, n_pad, nb * bm


def kernel(x, g_mix, w_in, ssm_conv_w, ssm_conv_b, ssm_dt_bias, ssm_a_log, ssm_d, ssm_norm_g, w_ssm_out,
           sc_conv_w, w_sc_out, b_gate, w_o, g_ffn, w_router, b_router, w_gate_up, b_gate_up, w_down,
           b_down, g_final):
    bsz, seq, d = x.shape
    t = bsz * seq
    assert bsz == 1 and d == D_MODEL and w_in.shape[0] == 1
    xt = x.reshape(t, d)
    w_dt = w_in[0, :, OFF_DT:OFF_SC]
    w_dt_hi = w_dt.astype(BF16)
    w_dt = jnp.concatenate([w_dt_hi, (w_dt - w_dt_hi.astype(F32)).astype(BF16)], axis=1)
    col = lambda a: a.reshape(-1, 1)
    row = lambda a: a.reshape(1, -1)

    u, dt_raw = _prenorm(xt, row(g_mix[0]), w_dt)
    proj = _inproj(u, w_in[0].T)
    y_norm = _ssd(proj, dt_raw.T, ssm_conv_w[0], row(ssm_conv_b[0]), col(ssm_dt_bias[0]), col(ssm_a_log[0]),
                  col(ssm_d[0]), row(ssm_norm_g[0]))
    mixed = _mix(y_norm, proj, row(b_gate[0]), sc_conv_w[0], w_ssm_out[0].astype(BF16), w_sc_out[0].astype(BF16))
    wr_t = w_router[0].T
    wr_hi = wr_t.astype(BF16)
    wr_lo = (wr_t - wr_hi.astype(F32)).astype(BF16)
    h1, xp, top_e, top_w, rank, counts = _route(mixed, xt, w_o[0].astype(BF16), row(g_ffn[0]),
                                                jnp.concatenate([wr_hi, wr_lo], axis=0), col(b_router[0]))
    dest, first_blk, n_blk, n_used, pad_rows, n_pad, n_rows = _routing_tables(top_e, rank, counts[:, 0], t)
    xs = _dispatch(xp, _tile_major(dest, min(DISPATCH_TM, t)), pad_rows, n_pad, n_used, n_rows)
    hid = _ffn_up(first_blk, n_blk, n_used, xs, w_gate_up[0], b_gate_up[0].reshape(N_EXPERTS, 1, 2 * D_FF))
    y = _ffn_down(first_blk, n_blk, n_used, hid, w_down[0],
                  b_down[0].reshape(N_EXPERTS, D_MODEL // FFN_CHUNK, 1, FFN_CHUNK))
    out = _combine(_tile_major(dest, min(COMBINE_TM, t)), h1, top_w.T, row(g_final), y)
    return out.reshape(bsz, seq, d)
```

```python
import functools

import jax
import jax.numpy as jnp
from jax import lax
from jax.experimental import pallas as pl
from jax.experimental.pallas import tpu as pltpu

D_MODEL = 2048
SSM_D_INNER = 2 * D_MODEL
SSM_HEAD_DIM = 64
SSM_N_HEADS = SSM_D_INNER // SSM_HEAD_DIM
SSM_N_GROUPS = 8
SSM_HEADS_PER_GROUP = SSM_N_HEADS // SSM_N_GROUPS
SSM_D_STATE = 128
SSM_CONV = 4
SSM_GN = SSM_N_GROUPS * SSM_D_STATE
SSM_CONV_DIM = SSM_D_INNER + 2 * SSM_GN
SSM_GROUP_CH = SSM_D_INNER // SSM_N_GROUPS
SC_DIM = D_MODEL
SC_WIDTH = 3
N_EXPERTS = 32
TOP_K = 4
D_FF = D_MODEL
SWIGLU_LIMIT = 7.0
SWIGLU_ALPHA = 1.702
EPS = 1e-5

OFF_Z = 0
OFF_XBC = OFF_Z + SSM_D_INNER
OFF_DT = OFF_XBC + SSM_CONV_DIM
OFF_SC = OFF_DT + SSM_N_HEADS
OFF_GATE = OFF_SC + 3 * SC_DIM
D_IN_PROJ = OFF_GATE + 2 * D_MODEL

P_Z = 0
P_XBC = P_Z + SSM_D_INNER
P_SC = P_XBC + SSM_CONV_DIM
P_GATE = P_SC + 3 * SC_DIM
P_TOTAL = P_GATE + 2 * D_MODEL

SUBLANES = 8
VMEM_LIMIT = 56 * 1024 * 1024

F32 = jnp.float32
BF16 = jnp.bfloat16
HIGHEST = lax.Precision.HIGHEST
NT_DIMS = (((1,), (1,)), ((), ()))
TN_DIMS = (((0,), (0,)), ((), ()))


def _sigmoid(v):
    return 0.5 * jnp.tanh(0.5 * v) + 0.5


def _params(semantics):
    return pltpu.CompilerParams(dimension_semantics=semantics, vmem_limit_bytes=VMEM_LIMIT)


PRENORM_TM = 1024
INPROJ_TM = 2048
INPROJ_TN = 1024
LANES = 128
DT_SHIFT = OFF_SC - OFF_DT
ALIGNED_TILES = OFF_DT // INPROJ_TN
CAST_ROWS = 256
EPILOGUE_ROWS = 256


def _prenorm_body(x_ref, g_ref, wdt_ref, u_ref, dt_ref):
    x = x_ref[...]
    u = x * lax.rsqrt(jnp.mean(x * x, axis=-1, keepdims=True) + EPS) * g_ref[...]
    u_hi = u.astype(BF16)
    u_ref[...] = u_hi
    u_lo = (u - u_hi.astype(F32)).astype(BF16)
    both = jnp.dot(u_hi, wdt_ref[...], preferred_element_type=F32)
    cross = jnp.dot(u_lo, wdt_ref[...], preferred_element_type=F32)
    nh = SSM_N_HEADS
    dt_ref[...] = both[:, :nh] + both[:, nh:] + cross[:, :nh]


def _prenorm(x, g, w_dt):
    t = x.shape[0]
    tm = min(PRENORM_TM, t)
    return pl.pallas_call(
        _prenorm_body,
        grid=(t // tm,),
        in_specs=[
            pl.BlockSpec((tm, D_MODEL), lambda i: (i, 0)),
            pl.BlockSpec((1, D_MODEL), lambda i: (0, 0)),
            pl.BlockSpec((D_MODEL, 2 * SSM_N_HEADS), lambda i: (0, 0)),
        ],
        out_specs=[
            pl.BlockSpec((tm, D_MODEL), lambda i: (i, 0)),
            pl.BlockSpec((tm, SSM_N_HEADS), lambda i: (i, 0)),
        ],
        out_shape=[
            jax.ShapeDtypeStruct((t, D_MODEL), BF16),
            jax.ShapeDtypeStruct((t, SSM_N_HEADS), F32),
        ],
        compiler_params=_params(("parallel",)),
        name="prenorm",
    )(x, g, w_dt)


def _inproj_body(u_ref, wt_ref, proj_ref, w_bf):
    j, i = pl.program_id(0), pl.program_id(1)
    tm = u_ref.shape[0]
    z_tiles = P_XBC // w_bf.shape[0]

    @pl.when(i == 0)
    def _():
        for r in range(0, w_bf.shape[0], CAST_ROWS):
            w_bf[r:r + CAST_ROWS, :] = wt_ref[r:r + CAST_ROWS, :].astype(BF16)

    def project(r0, rows):
        return lax.dot_general(u_ref[r0:r0 + rows, :], w_bf[...], NT_DIMS, preferred_element_type=F32)

    chunk = min(EPILOGUE_ROWS, tm)

    @pl.when(j < z_tiles)
    def _():
        for r0 in range(0, tm, chunk):
            acc = project(r0, chunk)
            proj_ref[r0:r0 + chunk, :] = (acc * _sigmoid(acc)).astype(BF16)

    @pl.when(j >= z_tiles)
    def _():
        proj_ref[...] = project(0, tm).astype(BF16)


def _inproj(u, w_in_t):
    t = u.shape[0]
    tm = min(INPROJ_TM, t)
    tn = INPROJ_TN
    assert OFF_DT % tn == 0 and DT_SHIFT % SUBLANES == 0 and P_XBC % tn == 0
    first_row = lambda j: pl.multiple_of(j * tn + jnp.where(j >= ALIGNED_TILES, DT_SHIFT, 0), SUBLANES)
    return pl.pallas_call(
        _inproj_body,
        grid=(P_TOTAL // tn, t // tm),
        in_specs=[
            pl.BlockSpec((tm, D_MODEL), lambda j, i: (i, 0)),
            pl.BlockSpec((pl.Element(tn), pl.Element(D_MODEL)), lambda j, i: (first_row(j), 0)),
        ],
        out_specs=pl.BlockSpec((tm, tn), lambda j, i: (i, j)),
        out_shape=jax.ShapeDtypeStruct((t, P_TOTAL), BF16),
        scratch_shapes=[pltpu.VMEM((tn, D_MODEL), BF16)],
        compiler_params=_params(("arbitrary", "arbitrary")),
        name="inproj",
    )(u, w_in_t)


SSD_L = 256
SSD_GROUPS = 8
HEAD_PAIR = 2 * SSM_HEAD_DIM


def _ssd_body(z_ref, x_ref, b_ref, c_ref, dt_ref, widen_ref, wx_ref, wb_ref, wc_ref, bx_ref, bb_ref, bc_ref,
              dtb_ref, alog_ref, d_ref, ng_ref, o_ref, s_ref, xbuf, bbuf, cbuf):
    L = x_ref.shape[0]
    tail = SUBLANES

    @pl.when(pl.program_id(1) == 0)
    def _():
        s_ref[...] = jnp.zeros_like(s_ref)
        xbuf[0:tail, :] = jnp.zeros((tail, xbuf.shape[1]), F32)
        bbuf[0:tail, :] = jnp.zeros((tail, bbuf.shape[1]), F32)
        cbuf[0:tail, :] = jnp.zeros((tail, cbuf.shape[1]), F32)

    def conv_silu(buf, in_ref, w_ref, bias_ref):
        buf[tail:tail + L, :] = in_ref[...].astype(F32)
        acc = bias_ref[...] + w_ref[SSM_CONV - 1:SSM_CONV, :] * buf[tail:tail + L, :]
        for k in range(SSM_CONV - 1):
            off = tail - (SSM_CONV - 1) + k
            acc = acc + w_ref[k:k + 1, :] * buf[off:off + L, :]
        buf[0:tail, :] = buf[L:L + tail, :]
        return acc * _sigmoid(acc)

    xs_all = conv_silu(xbuf, x_ref, wx_ref, bx_ref)
    bm_all = conv_silu(bbuf, b_ref, wb_ref, bb_ref).astype(BF16)
    cm_all = conv_silu(cbuf, c_ref, wc_ref, bc_ref).astype(BF16)

    dt_raw = dt_ref[...] + dtb_ref[...]
    dt_all = jnp.maximum(dt_raw, 0.0) + jnp.log(1.0 + jnp.exp(-jnp.abs(dt_raw)))
    da_all = dt_all * (-jnp.exp(alog_ref[...]))
    hg = SSM_HEADS_PER_GROUP
    gc = SSM_GROUP_CH
    n = SSM_D_STATE
    row = lax.broadcasted_iota(jnp.int32, (L, L), 0)
    col = lax.broadcasted_iota(jnp.int32, (L, L), 1)
    causal = row >= col
    incl = (row <= col).astype(BF16)
    lane = lax.broadcasted_iota(jnp.int32, (L, HEAD_PAIR), 1)
    first = lane < SSM_HEAD_DIM

    def pieces(a):
        hi = a.astype(BF16).astype(F32)
        mid = (a - hi).astype(BF16).astype(F32)
        lo = (a - hi - mid).astype(BF16).astype(F32)
        return hi, mid, lo

    for gi in range(x_ref.shape[1] // gc):
        ch = slice(gi * gc, (gi + 1) * gc)
        st = slice(gi * n, (gi + 1) * n)
        hd = slice(gi * hg, (gi + 1) * hg)
        xs, bm, cm, dt = xs_all[:, ch], bm_all[:, st], cm_all[:, st], dt_all[hd, :]
        da_p = pieces(da_all[hd, :])
        parts = jnp.dot(jnp.concatenate(da_p, axis=0).astype(BF16), incl, preferred_element_type=F32)
        cs = parts[0:hg] + parts[hg:2 * hg] + parts[2 * hg:3 * hg]
        cs_end = cs[:, L - 1:L]
        split = [pieces(a) for a in (dt, jnp.exp(cs_end - cs), jnp.exp(cs))]
        stacked = jnp.concatenate([split[q][s] for s in range(3) for q in range(3)] + [cs], axis=0)
        flipped = stacked.T
        cs_t = flipped[:, 9 * hg:10 * hg]
        wide = jnp.dot(flipped.astype(BF16), widen_ref[...], preferred_element_type=F32)
        dt_x, to_end_x, ecs_x = wide[:, 0:gc], wide[:, gc:2 * gc], wide[:, 2 * gc:3 * gc]

        cb = lax.dot_general(cm, bm, NT_DIMS, preferred_element_type=F32)
        cb = jnp.where(causal, cb, 0.0)
        y_off = jnp.dot(cm, s_ref[gi].astype(BF16), preferred_element_type=F32)

        xdt = xs * dt_x
        xdt_b = xdt.astype(BF16)
        ys = []
        for p in range(hg // 2):
            sl = slice(p * HEAD_PAIR, (p + 1) * HEAD_PAIR)
            yd = []
            for h in (2 * p, 2 * p + 1):
                seg = cs_t[:, h:h + 1] - cs[h:h + 1, :]
                m = cb * jnp.exp(jnp.minimum(seg, 0.0))
                yd.append(jnp.dot(m.astype(BF16), xdt_b[:, sl], preferred_element_type=F32))
            ys.append(jnp.where(first, yd[0], yd[1]))
        y = jnp.concatenate(ys, axis=1) + y_off * ecs_x + xs * d_ref[:, ch]

        xw = (xdt * to_end_x).astype(BF16)
        upd = lax.dot_general(bm, xw, TN_DIMS, preferred_element_type=F32)
        s_ref[gi] = s_ref[gi] * ecs_x[L - 1:L, :] + upd

        v = y * z_ref[:, ch].astype(F32)
        v = v * lax.rsqrt(jnp.mean(v * v, axis=-1, keepdims=True) + EPS)
        o_ref[:, ch] = (v * ng_ref[:, ch]).astype(BF16)


def _ssd(proj, dt_rows, conv_w, conv_b, dt_bias, a_log, d_skip, norm_g):
    t = proj.shape[0]
    L = min(SSD_L, t)
    gc = SSM_GROUP_CH
    n = SSM_D_STATE
    zc, xc = P_Z // gc, P_XBC // gc
    bc, cc = (P_XBC + SSM_D_INNER) // n, (P_XBC + SSM_D_INNER + SSM_GN) // n
    wbc, wcc = SSM_D_INNER // n, (SSM_D_INNER + SSM_GN) // n
    hg = SSM_HEADS_PER_GROUP
    n_factors, n_pieces = 3, 3
    r = jnp.arange((n_factors * n_pieces + 1) * hg)[:, None]
    c = jnp.arange(n_factors * gc)[None, :]
    widen = ((r < n_factors * n_pieces * hg)
             & (r % (n_factors * hg) == (c // gc) * hg + (c % gc) // SSM_HEAD_DIM)).astype(BF16)
    d_wide = jnp.repeat(d_skip.reshape(-1), SSM_HEAD_DIM).reshape(1, -1)
    k = SSD_GROUPS
    gc, n, hg = k * gc, k * n, k * hg
    zc, xc, bc, cc, wbc, wcc = zc // k, xc // k, bc // k, cc // k, wbc // k, wcc // k
    return pl.pallas_call(
        _ssd_body,
        grid=(SSM_N_GROUPS // k, t // L),
        in_specs=[
            pl.BlockSpec((L, gc), lambda g, i: (i, zc + g)),
            pl.BlockSpec((L, gc), lambda g, i: (i, xc + g)),
            pl.BlockSpec((L, n), lambda g, i: (i, bc + g)),
            pl.BlockSpec((L, n), lambda g, i: (i, cc + g)),
            pl.BlockSpec((hg, L), lambda g, i: (g, i)),
            pl.BlockSpec(widen.shape, lambda g, i: (0, 0)),
            pl.BlockSpec((SSM_CONV, gc), lambda g, i: (0, g)),
            pl.BlockSpec((SSM_CONV, n), lambda g, i: (0, wbc + g)),
            pl.BlockSpec((SSM_CONV, n), lambda g, i: (0, wcc + g)),
            pl.BlockSpec((1, gc), lambda g, i: (0, g)),
            pl.BlockSpec((1, n), lambda g, i: (0, wbc + g)),
            pl.BlockSpec((1, n), lambda g, i: (0, wcc + g)),
            pl.BlockSpec((hg, 1), lambda g, i: (g, 0)),
            pl.BlockSpec((hg, 1), lambda g, i: (g, 0)),
            pl.BlockSpec((1, gc), lambda g, i: (0, g)),
            pl.BlockSpec((1, gc), lambda g, i: (0, g)),
        ],
        out_specs=pl.BlockSpec((L, gc), lambda g, i: (i, g)),
        out_shape=jax.ShapeDtypeStruct((t, SSM_D_INNER), BF16),
        scratch_shapes=[
            pltpu.VMEM((k, SSM_D_STATE, SSM_GROUP_CH), F32),
            pltpu.VMEM((L + SUBLANES, gc), F32),
            pltpu.VMEM((L + SUBLANES, n), F32),
            pltpu.VMEM((L + SUBLANES, n), F32),
        ],
        compiler_params=_params(("parallel", "arbitrary")),
        name="ssd",
    )(proj, proj, proj, proj, dt_rows, widen, conv_w, conv_w, conv_w, conv_b, conv_b, conv_b,
      dt_bias, a_log, d_wide, norm_g)


MIX_TM = 256


def _mix_body(yn_ref, b_ref, c_ref, v_ref, cp_ref, vp_ref, g1_ref, g2_ref, bg1_ref, bg2_ref,
              wc_ref, wssm_ref, wsc_ref, o_ref, buf):
    tm = yn_ref.shape[0]
    tail = SUBLANES
    prev = cp_ref[...].astype(F32) * vp_ref[...].astype(F32)
    buf[0:tail, :] = jnp.where(pl.program_id(0) == 0, 0.0, prev)
    cv = c_ref[...].astype(F32) * v_ref[...].astype(F32)
    buf[tail:tail + tm, :] = cv
    conv = wc_ref[SC_WIDTH - 1:SC_WIDTH, :] * cv
    for k in range(SC_WIDTH - 1):
        off = tail - (SC_WIDTH - 1) + k
        conv = conv + wc_ref[k:k + 1, :] * buf[off:off + tm, :]
    sc_in = (b_ref[...].astype(F32) * conv).astype(BF16)
    y_sc = jnp.dot(sc_in, wsc_ref[...], preferred_element_type=F32)
    y_ssm = jnp.dot(yn_ref[...], wssm_ref[...], preferred_element_type=F32)
    g1 = _sigmoid(g1_ref[...].astype(F32) + bg1_ref[...])
    g2 = _sigmoid(g2_ref[...].astype(F32) + bg2_ref[...])
    o_ref[...] = (g1 * y_ssm + g2 * y_sc).astype(BF16)


def _resident(shape):
    return pl.BlockSpec(shape, lambda *_: (0,) * len(shape), pipeline_mode=pl.Buffered(1))


def _mix(y_norm, proj, b_gate, sc_conv_w, w_ssm_out, w_sc_out):
    t = y_norm.shape[0]
    tm = min(MIX_TM, t)
    d = D_MODEL
    sb, gb = P_SC // d, P_GATE // d
    prev_rows = lambda i: jnp.maximum(i * (tm // SUBLANES) - 1, 0)
    return pl.pallas_call(
        _mix_body,
        grid=(t // tm,),
        in_specs=[
            pl.BlockSpec((tm, SSM_D_INNER), lambda i: (i, 0)),
            pl.BlockSpec((tm, d), lambda i: (i, sb)),
            pl.BlockSpec((tm, d), lambda i: (i, sb + 1)),
            pl.BlockSpec((tm, d), lambda i: (i, sb + 2)),
            pl.BlockSpec((SUBLANES, d), lambda i: (prev_rows(i), sb + 1)),
            pl.BlockSpec((SUBLANES, d), lambda i: (prev_rows(i), sb + 2)),
            pl.BlockSpec((tm, d), lambda i: (i, gb)),
            pl.BlockSpec((tm, d), lambda i: (i, gb + 1)),
            pl.BlockSpec((1, d), lambda i: (0, 0)),
            pl.BlockSpec((1, d), lambda i: (0, 1)),
            pl.BlockSpec((SC_WIDTH, d), lambda i: (0, 0)),
            _resident((SSM_D_INNER, d)),
            _resident((d, d)),
        ],
        out_specs=pl.BlockSpec((tm, d), lambda i: (i, 0)),
        out_shape=jax.ShapeDtypeStruct((t, d), BF16),
        scratch_shapes=[pltpu.VMEM((tm + SUBLANES, d), F32)],
        compiler_params=_params(("parallel",)),
        name="mix",
    )(y_norm, proj, proj, proj, proj, proj, proj, proj, b_gate, b_gate, sc_conv_w, w_ssm_out, w_sc_out)


ROUTE_TM = 512
PACK_W = D_MODEL // 2


def _pack_rows(v):
    lo = lax.bitcast_convert_type(v[:, :PACK_W].astype(F32), jnp.uint32)
    hi = lax.bitcast_convert_type(v[:, PACK_W:].astype(F32), jnp.uint32)
    return hi | (lo >> 16)


def _unpack_rows(w):
    lo = lax.bitcast_convert_type(w << 16, F32).astype(BF16)
    hi = lax.bitcast_convert_type(w & jnp.uint32(0xFFFF0000), F32).astype(BF16)
    return jnp.concatenate([lo, hi], axis=1)


def _route_body(m_ref, x_ref, wo_ref, g_ref, wr_ref, br_ref,
                h_ref, xp_ref, e_ref, w_ref, r_ref, cnt_ref, carry):
    tm = m_ref.shape[0]

    @pl.when(pl.program_id(0) == 0)
    def _():
        carry[...] = jnp.zeros_like(carry)

    h = x_ref[...] + jnp.dot(m_ref[...], wo_ref[...], preferred_element_type=F32)
    h_ref[...] = h
    xn = h * lax.rsqrt(jnp.mean(h * h, axis=-1, keepdims=True) + EPS) * g_ref[...]
    xn_hi = xn.astype(BF16)
    xn_lo = (xn - xn_hi.astype(F32)).astype(BF16)
    xp_ref[...] = _pack_rows(xn_hi)

    ne = N_EXPERTS
    both = lax.dot_general(wr_ref[...], xn_hi, NT_DIMS, preferred_element_type=F32)
    cross = lax.dot_general(wr_ref[0:ne, :], xn_lo, NT_DIMS, preferred_element_type=F32)
    logits = both[0:ne] + both[ne:2 * ne] + cross + br_ref[...]
    eidx = lax.broadcasted_iota(jnp.int32, (N_EXPERTS, tm), 0)
    vals, hots = [], []
    for k in range(TOP_K):
        best = jnp.max(logits, axis=0, keepdims=True)
        arg = jnp.min(jnp.where(logits == best, eidx, N_EXPERTS), axis=0, keepdims=True)
        hot = eidx == arg
        e_ref[k:k + 1, :] = arg
        vals.append(best)
        hots.append(hot)
        logits = jnp.where(hot, -jnp.inf, logits)
    exps = [jnp.exp(v - vals[0]) for v in vals]
    denom = exps[0] + exps[1] + exps[2] + exps[3]
    for k in range(TOP_K):
        w_ref[k:k + 1, :] = exps[k] / denom

    cnt = (hots[0] | hots[1] | hots[2] | hots[3]).astype(F32)
    r_i = lax.broadcasted_iota(jnp.int32, (tm, tm), 0)
    c_i = lax.broadcasted_iota(jnp.int32, (tm, tm), 1)
    before = (r_i < c_i).astype(BF16)
    prior = carry[:, 0:1] + jnp.dot(cnt.astype(BF16), before, preferred_element_type=F32)
    for k in range(TOP_K):
        r_ref[k:k + 1, :] = jnp.sum(jnp.where(hots[k], prior, 0.0), axis=0, keepdims=True).astype(jnp.int32)
    total = carry[...] + jnp.sum(cnt, axis=1, keepdims=True)
    carry[...] = total
    cnt_ref[...] = total.astype(jnp.int32)


def _route(mixed, x, w_o, g_ffn, w_router_t, b_router):
    t = x.shape[0]
    tm = min(ROUTE_TM, t)
    d = D_MODEL
    return pl.pallas_call(
        _route_body,
        grid=(t // tm,),
        in_specs=[
            pl.BlockSpec((tm, d), lambda i: (i, 0)),
            pl.BlockSpec((tm, d), lambda i: (i, 0)),
            _resident((d, d)),
            pl.BlockSpec((1, d), lambda i: (0, 0)),
            pl.BlockSpec((2 * N_EXPERTS, d), lambda i: (0, 0)),
            pl.BlockSpec((N_EXPERTS, 1), lambda i: (0, 0)),
        ],
        out_specs=[
            pl.BlockSpec((tm, d), lambda i: (i, 0)),
            pl.BlockSpec((tm, PACK_W), lambda i: (i, 0)),
            pl.BlockSpec((TOP_K, tm), lambda i: (0, i)),
            pl.BlockSpec((TOP_K, tm), lambda i: (0, i)),
            pl.BlockSpec((TOP_K, tm), lambda i: (0, i)),
            pl.BlockSpec((N_EXPERTS, 128), lambda i: (0, 0)),
        ],
        out_shape=[
            jax.ShapeDtypeStruct((t, d), F32),
            jax.ShapeDtypeStruct((t, PACK_W), jnp.uint32),
            jax.ShapeDtypeStruct((TOP_K, t), jnp.int32),
            jax.ShapeDtypeStruct((TOP_K, t), F32),
            jax.ShapeDtypeStruct((TOP_K, t), jnp.int32),
            jax.ShapeDtypeStruct((N_EXPERTS, 128), jnp.int32),
        ],
        scratch_shapes=[pltpu.VMEM((N_EXPERTS, 128), F32)],
        compiler_params=_params(("arbitrary",)),
        name="route",
    )(mixed, x, w_o, g_ffn, w_router_t, b_router)


MOE_BM = 256
DISPATCH_TM = 256


def _dispatch_body(dest_ref, padrow_ref, npad_ref, nused_ref, xp_ref, xs_hbm, zblk, sem, zsem):
    tm = xp_ref.shape[0]
    bm = zblk.shape[0]
    nb = xs_hbm.shape[0] // bm

    @pl.when(pl.program_id(0) == 0)
    def _():
        zblk[...] = jnp.zeros_like(zblk)
        npad = npad_ref[0]
        nused = nused_ref[0]

        def zero_row(j):
            return pltpu.make_async_copy(zblk.at[pl.ds(0, 1), :], xs_hbm.at[pl.ds(padrow_ref[j], 1), :], zsem)

        def zero_block(b):
            return pltpu.make_async_copy(zblk, xs_hbm.at[pl.ds(pl.multiple_of(b * bm, bm), bm), :], zsem)

        def start_row(j, c):
            zero_row(j).start()
            return c

        def wait_row(j, c):
            zero_row(j).wait()
            return c

        def start_block(b, c):
            zero_block(b).start()
            return c

        def wait_block(b, c):
            zero_block(b).wait()
            return c

        lax.fori_loop(0, npad, start_row, 0)
        lax.fori_loop(nused, nb, start_block, 0)
        lax.fori_loop(0, npad, wait_row, 0)
        lax.fori_loop(nused, nb, wait_block, 0)

    def row_copy(r, k):
        return pltpu.make_async_copy(xp_ref.at[pl.ds(r, 1), :],
                                     xs_hbm.at[pl.ds(dest_ref[0, 0, k * tm + r], 1), :], sem)

    def start(r, c):
        for k in range(TOP_K):
            row_copy(r, k).start(priority=k % 2)
        return c

    lax.fori_loop(0, tm, start, 0)
    for k in range(TOP_K):
        pltpu.make_async_copy(xp_ref, xs_hbm.at[pl.ds(0, tm), :], sem).wait()


def _dispatch(xp, dest_tiles, pad_rows, n_pad, n_used, n_rows):
    t = xp.shape[0]
    tm = dest_tiles.shape[2] // TOP_K
    return pl.pallas_call(
        _dispatch_body,
        grid=(t // tm,),
        in_specs=[
            pl.BlockSpec((1, 1, TOP_K * tm), lambda i: (i, 0, 0), memory_space=pltpu.SMEM),
            pl.BlockSpec(memory_space=pltpu.SMEM),
            pl.BlockSpec(memory_space=pltpu.SMEM),
            pl.BlockSpec(memory_space=pltpu.SMEM),
            pl.BlockSpec((tm, PACK_W), lambda i: (i, 0)),
        ],
        out_specs=pl.BlockSpec(memory_space=pltpu.HBM),
        out_shape=jax.ShapeDtypeStruct((n_rows, PACK_W), jnp.uint32),
        scratch_shapes=[
            pltpu.VMEM((MOE_BM, PACK_W), jnp.uint32),
            pltpu.SemaphoreType.DMA(()),
            pltpu.SemaphoreType.DMA(()),
        ],
        compiler_params=_params(("arbitrary",)),
        name="dispatch",
    )(dest_tiles, pad_rows, n_pad, n_used, xp)


FFN_TF = 1024
FFN_TN = 2048
FFN_CHUNK = 2048
N_SLOTS = 2
IN_SLOTS = 3
STREAM_PRIORITY = 1


def _stream_expert_blocks(first_ref, nblk_ref, nused_ref, src_hbm, dst_hbm, ibuf, obuf, isem, osem,
                          prepare, compute):
    j, e = pl.program_id(0), pl.program_id(1)
    n_in = ibuf.shape[0]
    ahead = n_in - 1
    bm_in = ibuf.shape[1]
    n_chunks, bm_out, cw = obuf.shape[1:]
    width = n_chunks * cw
    half = bm_in // 2
    col = pl.multiple_of(j * width, width)
    nblk = nblk_ref[e]
    first = first_ref[e]

    def fetch(b, slot):
        src = src_hbm.at[pl.ds(pl.multiple_of(b * bm_in, bm_in), bm_in), :]
        return pltpu.make_async_copy(src, ibuf.at[slot], isem.at[slot])

    def fetch_start(b, slot):
        for p in range(2):
            src = src_hbm.at[pl.ds(pl.multiple_of(b * bm_in + p * half, half), half), :]
            pltpu.make_async_copy(src, ibuf.at[slot, pl.ds(p * half, half), :], isem.at[slot]).start(priority=p)

    def flush(b, slot):
        out_rows = pl.ds(pl.multiple_of(b * bm_out, bm_out), bm_out)
        return [pltpu.make_async_copy(obuf.at[slot, n], dst_hbm.at[out_rows, pl.ds(col + n * cw, cw)], osem.at[slot])
                for n in range(n_chunks)]

    def flush_start(b, slot):
        for cp in flush(b, slot):
            cp.start(priority=STREAM_PRIORITY)

    def flush_wait(b, slot):
        for cp in flush(b, slot):
            cp.wait()

    for d in range(ahead):
        @pl.when(nblk > d)
        def _():
            fetch_start(first + d, d)

    @pl.when(nblk > 0)
    def _():
        prepare()

    def step(i, carry):
        slot = lax.rem(i, n_in)
        oslot = lax.rem(i, N_SLOTS)
        fetch(first + i, slot).wait()

        @pl.when(i + ahead < nblk)
        def _():
            fetch_start(first + i + ahead, lax.rem(i + ahead, n_in))

        @pl.when(i >= N_SLOTS)
        def _():
            flush_wait(first + i - N_SLOTS, oslot)

        compute(ibuf[slot], obuf.at[oslot])
        flush_start(first + i, oslot)
        return carry

    lax.fori_loop(0, nblk, step, 0)

    @pl.when(nblk >= 2)
    def _():
        flush_wait(first + nblk - 2, lax.rem(nblk, N_SLOTS))

    @pl.when(nblk >= 1)
    def _():
        flush_wait(first + nblk - 1, lax.rem(nblk - 1, N_SLOTS))

    @pl.when(e == pl.num_programs(1) - 1)
    def _():
        obuf[0] = jnp.zeros(obuf.shape[1:], obuf.dtype)

        def zero(b, carry):
            flush_start(b, 0)
            flush_wait(b, 0)
            return carry

        lax.fori_loop(nused_ref[0], dst_hbm.shape[0] // bm_out, zero, 0)


def _ffn_up_body(first_ref, nblk_ref, nused_ref, xs_hbm, wg_ref, wu_ref, bg_ref, bu_ref, h_hbm,
                 wg_bf, wu_bf, ibuf, obuf, isem, osem):
    def prepare():
        wg_bf[...] = wg_ref[...].astype(BF16)
        wu_bf[...] = wu_ref[...].astype(BF16)

    def compute(packed, out_ref):
        x = _unpack_rows(packed)
        gate = jnp.dot(x, wg_bf[...], preferred_element_type=F32) + bg_ref[...]
        up = jnp.dot(x, wu_bf[...], preferred_element_type=F32) + bu_ref[...]
        gate = jnp.minimum(gate, SWIGLU_LIMIT)
        up = jnp.clip(up, -SWIGLU_LIMIT, SWIGLU_LIMIT)
        act = ((up + 1.0) * gate * _sigmoid(SWIGLU_ALPHA * gate)).astype(BF16)
        out_ref[0] = pltpu.bitcast(act, jnp.uint32)

    _stream_expert_blocks(first_ref, nblk_ref, nused_ref, xs_hbm, h_hbm, ibuf, obuf, isem, osem,
                          prepare, compute)


def _ffn_up(first_blk, n_blk, n_used, xs, w_gate_up, b_gate_up):
    n_rows = xs.shape[0]
    tf = FFN_TF
    nf = D_FF // tf
    return pl.pallas_call(
        _ffn_up_body,
        grid_spec=pltpu.PrefetchScalarGridSpec(
            num_scalar_prefetch=3,
            grid=(nf, N_EXPERTS),
            in_specs=[
                pl.BlockSpec(memory_space=pltpu.HBM),
                pl.BlockSpec((None, D_MODEL, tf), lambda j, e, *_: (e, 0, j)),
                pl.BlockSpec((None, D_MODEL, tf), lambda j, e, *_: (e, 0, nf + j)),
                pl.BlockSpec((None, 1, tf), lambda j, e, *_: (e, 0, j)),
                pl.BlockSpec((None, 1, tf), lambda j, e, *_: (e, 0, nf + j)),
            ],
            out_specs=pl.BlockSpec(memory_space=pltpu.HBM),
            scratch_shapes=[
                pltpu.VMEM((D_MODEL, tf), BF16),
                pltpu.VMEM((D_MODEL, tf), BF16),
                pltpu.VMEM((IN_SLOTS, MOE_BM, PACK_W), jnp.uint32),
                pltpu.VMEM((N_SLOTS, 1, MOE_BM // 2, tf), jnp.uint32),
                pltpu.SemaphoreType.DMA((IN_SLOTS,)),
                pltpu.SemaphoreType.DMA((N_SLOTS,)),
            ],
        ),
        out_shape=jax.ShapeDtypeStruct((n_rows // 2, D_FF), jnp.uint32),
        compiler_params=_params(("arbitrary", "arbitrary")),
        name="ffn_up",
    )(first_blk, n_blk, n_used, xs, w_gate_up, w_gate_up, b_gate_up, b_gate_up)


def _ffn_down_body(first_ref, nblk_ref, nused_ref, h_hbm, wd_ref, bd_ref, y_hbm, wd_bf, ibuf, obuf, isem, osem):
    n_chunks, _, cw = wd_bf.shape

    def prepare():
        for n in range(n_chunks):
            wd_bf[n] = wd_ref[:, n * cw:(n + 1) * cw].astype(BF16)

    def compute(paired, out_ref):
        hid = pltpu.bitcast(paired, BF16)

        def chunk(n, carry):
            out_ref[n] = jnp.dot(hid, wd_bf[n], preferred_element_type=F32) + bd_ref[n]
            return carry

        lax.fori_loop(0, n_chunks, chunk, 0)

    _stream_expert_blocks(first_ref, nblk_ref, nused_ref, h_hbm, y_hbm, ibuf, obuf, isem, osem,
                          prepare, compute)


def _ffn_down(first_blk, n_blk, n_used, h, w_down, b_down):
    n_rows = 2 * h.shape[0]
    tn = FFN_TN
    cw = FFN_CHUNK
    nc = tn // cw
    return pl.pallas_call(
        _ffn_down_body,
        grid_spec=pltpu.PrefetchScalarGridSpec(
            num_scalar_prefetch=3,
            grid=(D_MODEL // tn, N_EXPERTS),
            in_specs=[
                pl.BlockSpec(memory_space=pltpu.HBM),
                pl.BlockSpec((None, D_FF, tn), lambda j, e, *_: (e, 0, j)),
                pl.BlockSpec((None, nc, 1, cw), lambda j, e, *_: (e, j, 0, 0)),
            ],
            out_specs=pl.BlockSpec(memory_space=pltpu.HBM),
            scratch_shapes=[
                pltpu.VMEM((nc, D_FF, cw), BF16),
                pltpu.VMEM((IN_SLOTS, MOE_BM // 2, D_FF), jnp.uint32),
                pltpu.VMEM((N_SLOTS, nc, MOE_BM, cw), F32),
                pltpu.SemaphoreType.DMA((IN_SLOTS,)),
                pltpu.SemaphoreType.DMA((N_SLOTS,)),
            ],
        ),
        out_shape=jax.ShapeDtypeStruct((n_rows, D_MODEL), F32),
        compiler_params=_params(("arbitrary", "arbitrary")),
        name="ffn_down",
    )(first_blk, n_blk, n_used, h, w_down, b_down)


COMBINE_TM = 512


def _combine_body(dest_ref, next_ref, h_ref, w_ref, g_ref, y_hbm, o_ref, gbuf, sem):
    i = pl.program_id(0)
    tm = h_ref.shape[0]

    def gather_tile(table_ref, slot):
        def start(r, c):
            for k in range(TOP_K):
                pltpu.make_async_copy(y_hbm.at[pl.ds(table_ref[0, 0, k * tm + r], 1), :],
                                      gbuf.at[slot, k, pl.ds(r, 1), :], sem.at[slot]).start(priority=k % 2)
            return c

        lax.fori_loop(0, tm, start, 0)

    slot = lax.rem(i, N_SLOTS)

    @pl.when(i == 0)
    def _():
        gather_tile(dest_ref, 0)

    @pl.when(i + 1 < pl.num_programs(0))
    def _():
        gather_tile(next_ref, 1 - slot)

    for k in range(TOP_K):
        pltpu.make_async_copy(y_hbm.at[pl.ds(0, tm), :], gbuf.at[slot, k], sem.at[slot]).wait()
    h = h_ref[...]
    for k in range(TOP_K):
        h = h + w_ref[:, k:k + 1] * gbuf[slot, k]
    o_ref[...] = h * lax.rsqrt(jnp.mean(h * h, axis=-1, keepdims=True) + EPS) * g_ref[...]


def _combine(dest_tiles, h1, w_cols, g_final, y):
    t = h1.shape[0]
    tm = dest_tiles.shape[2] // TOP_K
    d = D_MODEL
    last = t // tm - 1
    return pl.pallas_call(
        _combine_body,
        grid=(t // tm,),
        in_specs=[
            pl.BlockSpec((1, 1, TOP_K * tm), lambda i: (i, 0, 0), memory_space=pltpu.SMEM),
            pl.BlockSpec((1, 1, TOP_K * tm), lambda i: (jnp.minimum(i + 1, last), 0, 0), memory_space=pltpu.SMEM),
            pl.BlockSpec((tm, d), lambda i: (i, 0)),
            pl.BlockSpec((tm, TOP_K), lambda i: (i, 0)),
            pl.BlockSpec((1, d), lambda i: (0, 0)),
            pl.BlockSpec(memory_space=pltpu.HBM),
        ],
        out_specs=pl.BlockSpec((tm, d), lambda i: (i, 0)),
        out_shape=jax.ShapeDtypeStruct((t, d), F32),
        scratch_shapes=[pltpu.VMEM((N_SLOTS, TOP_K, tm, d), F32), pltpu.SemaphoreType.DMA((N_SLOTS,))],
        compiler_params=_params(("arbitrary",)),
        name="combine",
    )(dest_tiles, dest_tiles, h1, w_cols, g_final, y)


def _tile_major(a, tm):
    k, t = a.shape
    return a.reshape(k, t // tm, tm).transpose(1, 0, 2).reshape(t // tm, 1, k * tm)


def _routing_tables(top_e, rank, counts, t):
    bm = MOE_BM
    nb = (t * TOP_K) // bm + N_EXPERTS
    padded = (counts + bm - 1) // bm * bm
    pad_end = jnp.cumsum(padded)
    pad_start = pad_end - padded
    onehot = top_e[:, :, None] == jnp.arange(N_EXPERTS, dtype=jnp.int32)
    dest = rank + jnp.sum(jnp.where(onehot, pad_start, 0), axis=-1)
    first_blk = (pad_start // bm).astype(jnp.int32)
    n_blk = (padded // bm).astype(jnp.int32)
    n_used = (pad_end[-1] // bm).astype(jnp.int32).reshape(1)
    gap = padded - counts
    gap_end = jnp.cumsum(gap)
    j = jnp.arange(N_EXPERTS * bm, dtype=jnp.int32)
    ej = jnp.minimum(jnp.sum(j[:, None] >= gap_end[None, :], axis=1), N_EXPERTS - 1)
    pad_rows = (pad_start + counts)[ej] + j - (gap_end - gap)[ej]
    pad_rows = jnp.clip(pad_rows, 0, nb * bm - 1).astype(jnp.int32)
    n_pad = gap_end[-1].astype(jnp.int32).reshape(1)
    return dest.astype(jnp.int32), first_blk, n_blk, n_used, pad_rows, n_pad, nb * bm


def kernel(x, g_mix, w_in, ssm_conv_w, ssm_conv_b, ssm_dt_bias, ssm_a_log, ssm_d, ssm_norm_g, w_ssm_out,
           sc_conv_w, w_sc_out, b_gate, w_o, g_ffn, w_router, b_router, w_gate_up, b_gate_up, w_down,
           b_down, g_final):
    bsz, seq, d = x.shape
    t = bsz * seq
    assert bsz == 1 and d == D_MODEL and w_in.shape[0] == 1
    xt = x.reshape(t, d)
    w_dt = w_in[0, :, OFF_DT:OFF_SC]
    w_dt_hi = w_dt.astype(BF16)
    w_dt = jnp.concatenate([w_dt_hi, (w_dt - w_dt_hi.astype(F32)).astype(BF16)], axis=1)
    col = lambda a: a.reshape(-1, 1)
    row = lambda a: a.reshape(1, -1)

    u, dt_raw = _prenorm(xt, row(g_mix[0]), w_dt)
    proj = _inproj(u, w_in[0].T)
    y_norm = _ssd(proj, dt_raw.T, ssm_conv_w[0], row(ssm_conv_b[0]), col(ssm_dt_bias[0]), col(ssm_a_log[0]),
                  col(ssm_d[0]), row(ssm_norm_g[0]))
    mixed = _mix(y_norm, proj, row(b_gate[0]), sc_conv_w[0], w_ssm_out[0].astype(BF16), w_sc_out[0].astype(BF16))
    wr_t = w_router[0].T
    wr_hi = wr_t.astype(BF16)
    wr_lo = (wr_t - wr_hi.astype(F32)).astype(BF16)
    h1, xp, top_e, top_w, rank, counts = _route(mixed, xt, w_o[0].astype(BF16), row(g_ffn[0]),
                                                jnp.concatenate([wr_hi, wr_lo], axis=0), col(b_router[0]))
    dest, first_blk, n_blk, n_used, pad_rows, n_pad, n_rows = _routing_tables(top_e, rank, counts[:, 0], t)
    xs = _dispatch(xp, _tile_major(dest, min(DISPATCH_TM, t)), pad_rows, n_pad, n_used, n_rows)
    hid = _ffn_up(first_blk, n_blk, n_used, xs, w_gate_up[0], b_gate_up[0].reshape(N_EXPERTS, 1, 2 * D_FF))
    y = _ffn_down(first_blk, n_blk, n_used, hid, w_down[0],
                  b_down[0].reshape(N_EXPERTS, D_MODEL // FFN_CHUNK, 1, FFN_CHUNK))
    out = _combine(_tile_major(dest, min(COMBINE_TM, t)), h1, top_w.T, row(g_final), y)
    return out.reshape(bsz, seq, d)
```

```python
import functools

import jax
import jax.numpy as jnp
from jax import lax
from jax.experimental import pallas as pl
from jax.experimental.pallas import tpu as pltpu

D_MODEL = 2048
SSM_D_INNER = 2 * D_MODEL
SSM_HEAD_DIM = 64
SSM_N_HEADS = SSM_D_INNER // SSM_HEAD_DIM
SSM_N_GROUPS = 8
SSM_HEADS_PER_GROUP = SSM_N_HEADS // SSM_N_GROUPS
SSM_D_STATE = 128
SSM_CONV = 4
SSM_GN = SSM_N_GROUPS * SSM_D_STATE
SSM_CONV_DIM = SSM_D_INNER + 2 * SSM_GN
SSM_GROUP_CH = SSM_D_INNER // SSM_N_GROUPS
SC_DIM = D_MODEL
SC_WIDTH = 3
N_EXPERTS = 32
TOP_K = 4
D_FF = D_MODEL
SWIGLU_LIMIT = 7.0
SWIGLU_ALPHA = 1.702
EPS = 1e-5

OFF_Z = 0
OFF_XBC = OFF_Z + SSM_D_INNER
OFF_DT = OFF_XBC + SSM_CONV_DIM
OFF_SC = OFF_DT + SSM_N_HEADS
OFF_GATE = OFF_SC + 3 * SC_DIM
D_IN_PROJ = OFF_GATE + 2 * D_MODEL

P_Z = 0
P_XBC = P_Z + SSM_D_INNER
P_SC = P_XBC + SSM_CONV_DIM
P_GATE = P_SC + 3 * SC_DIM
P_TOTAL = P_GATE + 2 * D_MODEL

SUBLANES = 8
VMEM_LIMIT = 56 * 1024 * 1024

F32 = jnp.float32
BF16 = jnp.bfloat16
HIGHEST = lax.Precision.HIGHEST
NT_DIMS = (((1,), (1,)), ((), ()))
TN_DIMS = (((0,), (0,)), ((), ()))


def _sigmoid(v):
    return 0.5 * jnp.tanh(0.5 * v) + 0.5


def _params(semantics):
    return pltpu.CompilerParams(dimension_semantics=semantics, vmem_limit_bytes=VMEM_LIMIT)


PRENORM_TM = 1024
INPROJ_TM = 2048
INPROJ_TN = 1024
LANES = 128
DT_SHIFT = OFF_SC - OFF_DT
ALIGNED_TILES = OFF_DT // INPROJ_TN
CAST_ROWS = 256
EPILOGUE_ROWS = 256


def _prenorm_body(x_ref, g_ref, wdt_ref, u_ref, dt_ref):
    x = x_ref[...]
    u = x * lax.rsqrt(jnp.mean(x * x, axis=-1, keepdims=True) + EPS) * g_ref[...]
    u_hi = u.astype(BF16)
    u_ref[...] = u_hi
    u_lo = (u - u_hi.astype(F32)).astype(BF16)
    both = jnp.dot(u_hi, wdt_ref[...], preferred_element_type=F32)
    cross = jnp.dot(u_lo, wdt_ref[...], preferred_element_type=F32)
    nh = SSM_N_HEADS
    dt_ref[...] = both[:, :nh] + both[:, nh:] + cross[:, :nh]


def _prenorm(x, g, w_dt):
    t = x.shape[0]
    tm = min(PRENORM_TM, t)
    return pl.pallas_call(
        _prenorm_body,
        grid=(t // tm,),
        in_specs=[
            pl.BlockSpec((tm, D_MODEL), lambda i: (i, 0)),
            pl.BlockSpec((1, D_MODEL), lambda i: (0, 0)),
            pl.BlockSpec((D_MODEL, 2 * SSM_N_HEADS), lambda i: (0, 0)),
        ],
        out_specs=[
            pl.BlockSpec((tm, D_MODEL), lambda i: (i, 0)),
            pl.BlockSpec((tm, SSM_N_HEADS), lambda i: (i, 0)),
        ],
        out_shape=[
            jax.ShapeDtypeStruct((t, D_MODEL), BF16),
            jax.ShapeDtypeStruct((t, SSM_N_HEADS), F32),
        ],
        compiler_params=_params(("parallel",)),
        name="prenorm",
    )(x, g, w_dt)


def _inproj_body(u_ref, wt_ref, proj_ref, w_bf):
    j, i = pl.program_id(0), pl.program_id(1)
    tm = u_ref.shape[0]
    z_tiles = P_XBC // w_bf.shape[0]

    @pl.when(i == 0)
    def _():
        for r in range(0, w_bf.shape[0], CAST_ROWS):
            w_bf[r:r + CAST_ROWS, :] = wt_ref[r:r + CAST_ROWS, :].astype(BF16)

    def project(r0, rows):
        return lax.dot_general(u_ref[r0:r0 + rows, :], w_bf[...], NT_DIMS, preferred_element_type=F32)

    chunk = min(EPILOGUE_ROWS, tm)

    @pl.when(j < z_tiles)
    def _():
        for r0 in range(0, tm, chunk):
            acc = project(r0, chunk)
            proj_ref[r0:r0 + chunk, :] = (acc * _sigmoid(acc)).astype(BF16)

    @pl.when(j >= z_tiles)
    def _():
        proj_ref[...] = project(0, tm).astype(BF16)


def _inproj(u, w_in_t):
    t = u.shape[0]
    tm = min(INPROJ_TM, t)
    tn = INPROJ_TN
    assert OFF_DT % tn == 0 and DT_SHIFT % SUBLANES == 0 and P_XBC % tn == 0
    first_row = lambda j: pl.multiple_of(j * tn + jnp.where(j >= ALIGNED_TILES, DT_SHIFT, 0), SUBLANES)
    return pl.pallas_call(
        _inproj_body,
        grid=(P_TOTAL // tn, t // tm),
        in_specs=[
            pl.BlockSpec((tm, D_MODEL), lambda j, i: (i, 0)),
            pl.BlockSpec((pl.Element(tn), pl.Element(D_MODEL)), lambda j, i: (first_row(j), 0)),
        ],
        out_specs=pl.BlockSpec((tm, tn), lambda j, i: (i, j)),
        out_shape=jax.ShapeDtypeStruct((t, P_TOTAL), BF16),
        scratch_shapes=[pltpu.VMEM((tn, D_MODEL), BF16)],
        compiler_params=_params(("arbitrary", "arbitrary")),
        name="inproj",
    )(u, w_in_t)


SSD_L = 256
SSD_GROUPS = 8
HEAD_PAIR = 2 * SSM_HEAD_DIM


def _ssd_body(z_ref, x_ref, b_ref, c_ref, dt_ref, widen_ref, wx_ref, wb_ref, wc_ref, bx_ref, bb_ref, bc_ref,
              dtb_ref, alog_ref, d_ref, ng_ref, o_ref, s_ref, xbuf, bbuf, cbuf):
    L = x_ref.shape[0]
    tail = SUBLANES

    @pl.when(pl.program_id(1) == 0)
    def _():
        s_ref[...] = jnp.zeros_like(s_ref)
        xbuf[0:tail, :] = jnp.zeros((tail, xbuf.shape[1]), F32)
        bbuf[0:tail, :] = jnp.zeros((tail, bbuf.shape[1]), F32)
        cbuf[0:tail, :] = jnp.zeros((tail, cbuf.shape[1]), F32)

    def conv_silu(buf, in_ref, w_ref, bias_ref):
        buf[tail:tail + L, :] = in_ref[...].astype(F32)
        acc = bias_ref[...] + w_ref[SSM_CONV - 1:SSM_CONV, :] * buf[tail:tail + L, :]
        for k in range(SSM_CONV - 1):
            off = tail - (SSM_CONV - 1) + k
            acc = acc + w_ref[k:k + 1, :] * buf[off:off + L, :]
        buf[0:tail, :] = buf[L:L + tail, :]
        return acc * _sigmoid(acc)

    xs_all = conv_silu(xbuf, x_ref, wx_ref, bx_ref)
    bm_all = conv_silu(bbuf, b_ref, wb_ref, bb_ref).astype(BF16)
    cm_all = conv_silu(cbuf, c_ref, wc_ref, bc_ref).astype(BF16)

    dt_raw = dt_ref[...] + dtb_ref[...]
    dt_all = jnp.maximum(dt_raw, 0.0) + jnp.log(1.0 + jnp.exp(-jnp.abs(dt_raw)))
    da_all = dt_all * (-jnp.exp(alog_ref[...]))
    hg = SSM_HEADS_PER_GROUP
    gc = SSM_GROUP_CH
    n = SSM_D_STATE
    row = lax.broadcasted_iota(jnp.int32, (L, L), 0)
    col = lax.broadcasted_iota(jnp.int32, (L, L), 1)
    causal = row >= col
    incl = (row <= col).astype(BF16)
    lane = lax.broadcasted_iota(jnp.int32, (L, HEAD_PAIR), 1)
    first = lane < SSM_HEAD_DIM

    def pieces(a):
        hi = a.astype(BF16).astype(F32)
        mid = (a - hi).astype(BF16).astype(F32)
        lo = (a - hi - mid).astype(BF16).astype(F32)
        return hi, mid, lo

    for gi in range(x_ref.shape[1] // gc):
        ch = slice(gi * gc, (gi + 1) * gc)
        st = slice(gi * n, (gi + 1) * n)
        hd = slice(gi * hg, (gi + 1) * hg)
        xs, bm, cm, dt = xs_all[:, ch], bm_all[:, st], cm_all[:, st], dt_all[hd, :]
        da_p = pieces(da_all[hd, :])
        parts = jnp.dot(jnp.concatenate(da_p, axis=0).astype(BF16), incl, preferred_element_type=F32)
        cs = parts[0:hg] + parts[hg:2 * hg] + parts[2 * hg:3 * hg]
        cs_end = cs[:, L - 1:L]
        split = [pieces(a) for a in (dt, jnp.exp(cs_end - cs), jnp.exp(cs))]
        stacked = jnp.concatenate([split[q][s] for s in range(3) for q in range(3)] + [cs], axis=0)
        flipped = stacked.T
        cs_t = flipped[:, 9 * hg:10 * hg]
        wide = jnp.dot(flipped.astype(BF16), widen_ref[...], preferred_element_type=F32)
        dt_x, to_end_x, ecs_x = wide[:, 0:gc], wide[:, gc:2 * gc], wide[:, 2 * gc:3 * gc]

        cb = lax.dot_general(cm, bm, NT_DIMS, preferred_element_type=F32)
        cb = jnp.where(causal, cb, 0.0)
        y_off = jnp.dot(cm, s_ref[gi].astype(BF16), preferred_element_type=F32)

        xdt = xs * dt_x
        xdt_b = xdt.astype(BF16)
        ys = []
        for p in range(hg // 2):
            sl = slice(p * HEAD_PAIR, (p + 1) * HEAD_PAIR)
            yd = []
            for h in (2 * p, 2 * p + 1):
                seg = cs_t[:, h:h + 1] - cs[h:h + 1, :]
                m = cb * jnp.exp(jnp.minimum(seg, 0.0))
                yd.append(jnp.dot(m.astype(BF16), xdt_b[:, sl], preferred_element_type=F32))
            ys.append(jnp.where(first, yd[0], yd[1]))
        y = jnp.concatenate(ys, axis=1) + y_off * ecs_x + xs * d_ref[:, ch]

        xw = (xdt * to_end_x).astype(BF16)
        upd = lax.dot_general(bm, xw, TN_DIMS, preferred_element_type=F32)
        s_ref[gi] = s_ref[gi] * ecs_x[L - 1:L, :] + upd

        v = y * z_ref[:, ch].astype(F32)
        v = v * lax.rsqrt(jnp.mean(v * v, axis=-1, keepdims=True) + EPS)
        o_ref[:, ch] = (v * ng_ref[:, ch]).astype(BF16)


def _ssd(proj, dt_rows, conv_w, conv_b, dt_bias, a_log, d_skip, norm_g):
    t = proj.shape[0]
    L = min(SSD_L, t)
    gc = SSM_GROUP_CH
    n = SSM_D_STATE
    zc, xc = P_Z // gc, P_XBC // gc
    bc, cc = (P_XBC + SSM_D_INNER) // n, (P_XBC + SSM_D_INNER + SSM_GN) // n
    wbc, wcc = SSM_D_INNER // n, (SSM_D_INNER + SSM_GN) // n
    hg = SSM_HEADS_PER_GROUP
    n_factors, n_pieces = 3, 3
    r = jnp.arange((n_factors * n_pieces + 1) * hg)[:, None]
    c = jnp.arange(n_factors * gc)[None, :]
    widen = ((r < n_factors * n_pieces * hg)
             & (r % (n_factors * hg) == (c // gc) * hg + (c % gc) // SSM_HEAD_DIM)).astype(BF16)
    d_wide = jnp.repeat(d_skip.reshape(-1), SSM_HEAD_DIM).reshape(1, -1)
    k = SSD_GROUPS
    gc, n, hg = k * gc, k * n, k * hg
    zc, xc, bc, cc, wbc, wcc = zc // k, xc // k, bc // k, cc // k, wbc // k, wcc // k
    return pl.pallas_call(
        _ssd_body,
        grid=(SSM_N_GROUPS // k, t // L),
        in_specs=[
            pl.BlockSpec((L, gc), lambda g, i: (i, zc + g)),
            pl.BlockSpec((L, gc), lambda g, i: (i, xc + g)),
            pl.BlockSpec((L, n), lambda g, i: (i, bc + g)),
            pl.BlockSpec((L, n), lambda g, i: (i, cc + g)),
            pl.BlockSpec((hg, L), lambda g, i: (g, i)),
            pl.BlockSpec(widen.shape, lambda g, i: (0, 0)),
            pl.BlockSpec((SSM_CONV, gc), lambda g, i: (0, g)),
            pl.BlockSpec((SSM_CONV, n), lambda g, i: (0, wbc + g)),
            pl.BlockSpec((SSM_CONV, n), lambda g, i: (0, wcc + g)),
            pl.BlockSpec((1, gc), lambda g, i: (0, g)),
            pl.BlockSpec((1, n), lambda g, i: (0, wbc + g)),
            pl.BlockSpec((1, n), lambda g, i: (0, wcc + g)),
            pl.BlockSpec((hg, 1), lambda g, i: (g, 0)),
            pl.BlockSpec((hg, 1), lambda g, i: (g, 0)),
            pl.BlockSpec((1, gc), lambda g, i: (0, g)),
            pl.BlockSpec((1, gc), lambda g, i: (0, g)),
        ],
        out_specs=pl.BlockSpec((L, gc), lambda g, i: (i, g)),
        out_shape=jax.ShapeDtypeStruct((t, SSM_D_INNER), BF16),
        scratch_shapes=[
            pltpu.VMEM((k, SSM_D_STATE, SSM_GROUP_CH), F32),
            pltpu.VMEM((L + SUBLANES, gc), F32),
            pltpu.VMEM((L + SUBLANES, n), F32),
            pltpu.VMEM((L + SUBLANES, n), F32),
        ],
        compiler_params=_params(("parallel", "arbitrary")),
        name="ssd",
    )(proj, proj, proj, proj, dt_rows, widen, conv_w, conv_w, conv_w, conv_b, conv_b, conv_b,
      dt_bias, a_log, d_wide, norm_g)


MIX_TM = 256


def _mix_body(yn_ref, b_ref, c_ref, v_ref, cp_ref, vp_ref, g1_ref, g2_ref, bg1_ref, bg2_ref,
              wc_ref, wssm_ref, wsc_ref, o_ref, buf):
    tm = yn_ref.shape[0]
    tail = SUBLANES
    prev = cp_ref[...].astype(F32) * vp_ref[...].astype(F32)
    buf[0:tail, :] = jnp.where(pl.program_id(0) == 0, 0.0, prev)
    cv = c_ref[...].astype(F32) * v_ref[...].astype(F32)
    buf[tail:tail + tm, :] = cv
    conv = wc_ref[SC_WIDTH - 1:SC_WIDTH, :] * cv
    for k in range(SC_WIDTH - 1):
        off = tail - (SC_WIDTH - 1) + k
        conv = conv + wc_ref[k:k + 1, :] * buf[off:off + tm, :]
    sc_in = (b_ref[...].astype(F32) * conv).astype(BF16)
    y_sc = jnp.dot(sc_in, wsc_ref[...], preferred_element_type=F32)
    y_ssm = jnp.dot(yn_ref[...], wssm_ref[...], preferred_element_type=F32)
    g1 = _sigmoid(g1_ref[...].astype(F32) + bg1_ref[...])
    g2 = _sigmoid(g2_ref[...].astype(F32) + bg2_ref[...])
    o_ref[...] = (g1 * y_ssm + g2 * y_sc).astype(BF16)


def _resident(shape):
    return pl.BlockSpec(shape, lambda *_: (0,) * len(shape), pipeline_mode=pl.Buffered(1))


def _mix(y_norm, proj, b_gate, sc_conv_w, w_ssm_out, w_sc_out):
    t = y_norm.shape[0]
    tm = min(MIX_TM, t)
    d = D_MODEL
    sb, gb = P_SC // d, P_GATE // d
    prev_rows = lambda i: jnp.maximum(i * (tm // SUBLANES) - 1, 0)
    return pl.pallas_call(
        _mix_body,
        grid=(t // tm,),
        in_specs=[
            pl.BlockSpec((tm, SSM_D_INNER), lambda i: (i, 0)),
            pl.BlockSpec((tm, d), lambda i: (i, sb)),
            pl.BlockSpec((tm, d), lambda i: (i, sb + 1)),
            pl.BlockSpec((tm, d), lambda i: (i, sb + 2)),
            pl.BlockSpec((SUBLANES, d), lambda i: (prev_rows(i), sb + 1)),
            pl.BlockSpec((SUBLANES, d), lambda i: (prev_rows(i), sb + 2)),
            pl.BlockSpec((tm, d), lambda i: (i, gb)),
            pl.BlockSpec((tm, d), lambda i: (i, gb + 1)),
            pl.BlockSpec((1, d), lambda i: (0, 0)),
            pl.BlockSpec((1, d), lambda i: (0, 1)),
            pl.BlockSpec((SC_WIDTH, d), lambda i: (0, 0)),
            _resident((SSM_D_INNER, d)),
            _resident((d, d)),
        ],
        out_specs=pl.BlockSpec((tm, d), lambda i: (i, 0)),
        out_shape=jax.ShapeDtypeStruct((t, d), BF16),
        scratch_shapes=[pltpu.VMEM((tm + SUBLANES, d), F32)],
        compiler_params=_params(("parallel",)),
        name="mix",
    )(y_norm, proj, proj, proj, proj, proj, proj, proj, b_gate, b_gate, sc_conv_w, w_ssm_out, w_sc_out)


ROUTE_TM = 512
PACK_W = D_MODEL // 2


def _pack_rows(v):
    lo = lax.bitcast_convert_type(v[:, :PACK_W].astype(F32), jnp.uint32)
    hi = lax.bitcast_convert_type(v[:, PACK_W:].astype(F32), jnp.uint32)
    return hi | (lo >> 16)


def _unpack_rows(w):
    lo = lax.bitcast_convert_type(w << 16, F32).astype(BF16)
    hi = lax.bitcast_convert_type(w & jnp.uint32(0xFFFF0000), F32).astype(BF16)
    return jnp.concatenate([lo, hi], axis=1)


def _route_body(m_ref, x_ref, wo_ref, g_ref, wr_ref, br_ref,
                h_ref, xp_ref, e_ref, w_ref, r_ref, cnt_ref, carry):
    tm = m_ref.shape[0]

    @pl.when(pl.program_id(0) == 0)
    def _():
        carry[...] = jnp.zeros_like(carry)

    h = x_ref[...] + jnp.dot(m_ref[...], wo_ref[...], preferred_element_type=F32)
    h_ref[...] = h
    xn = h * lax.rsqrt(jnp.mean(h * h, axis=-1, keepdims=True) + EPS) * g_ref[...]
    xn_hi = xn.astype(BF16)
    xn_lo = (xn - xn_hi.astype(F32)).astype(BF16)
    xp_ref[...] = _pack_rows(xn_hi)

    ne = N_EXPERTS
    both = lax.dot_general(wr_ref[...], xn_hi, NT_DIMS, preferred_element_type=F32)
    cross = lax.dot_general(wr_ref[0:ne, :], xn_lo, NT_DIMS, preferred_element_type=F32)
    logits = both[0:ne] + both[ne:2 * ne] + cross + br_ref[...]
    eidx = lax.broadcasted_iota(jnp.int32, (N_EXPERTS, tm), 0)
    vals, hots = [], []
    for k in range(TOP_K):
        best = jnp.max(logits, axis=0, keepdims=True)
        arg = jnp.min(jnp.where(logits == best, eidx, N_EXPERTS), axis=0, keepdims=True)
        hot = eidx == arg
        e_ref[k:k + 1, :] = arg
        vals.append(best)
        hots.append(hot)
        logits = jnp.where(hot, -jnp.inf, logits)
    exps = [jnp.exp(v - vals[0]) for v in vals]
    denom = exps[0] + exps[1] + exps[2] + exps[3]
    for k in range(TOP_K):
        w_ref[k:k + 1, :] = exps[k] / denom

    cnt = (hots[0] | hots[1] | hots[2] | hots[3]).astype(F32)
    r_i = lax.broadcasted_iota(jnp.int32, (tm, tm), 0)
    c_i = lax.broadcasted_iota(jnp.int32, (tm, tm), 1)
    before = (r_i < c_i).astype(BF16)
    prior = carry[:, 0:1] + jnp.dot(cnt.astype(BF16), before, preferred_element_type=F32)
    for k in range(TOP_K):
        r_ref[k:k + 1, :] = jnp.sum(jnp.where(hots[k], prior, 0.0), axis=0, keepdims=True).astype(jnp.int32)
    total = carry[...] + jnp.sum(cnt, axis=1, keepdims=True)
    carry[...] = total
    cnt_ref[...] = total.astype(jnp.int32)


def _route(mixed, x, w_o, g_ffn, w_router_t, b_router):
    t = x.shape[0]
    tm = min(ROUTE_TM, t)
    d = D_MODEL
    return pl.pallas_call(
        _route_body,
        grid=(t // tm,),
        in_specs=[
            pl.BlockSpec((tm, d), lambda i: (i, 0)),
            pl.BlockSpec((tm, d), lambda i: (i, 0)),
            _resident((d, d)),
            pl.BlockSpec((1, d), lambda i: (0, 0)),
            pl.BlockSpec((2 * N_EXPERTS, d), lambda i: (0, 0)),
            pl.BlockSpec((N_EXPERTS, 1), lambda i: (0, 0)),
        ],
        out_specs=[
            pl.BlockSpec((tm, d), lambda i: (i, 0)),
            pl.BlockSpec((tm, PACK_W), lambda i: (i, 0)),
            pl.BlockSpec((TOP_K, tm), lambda i: (0, i)),
            pl.BlockSpec((TOP_K, tm), lambda i: (0, i)),
            pl.BlockSpec((TOP_K, tm), lambda i: (0, i)),
            pl.BlockSpec((N_EXPERTS, 128), lambda i: (0, 0)),
        ],
        out_shape=[
            jax.ShapeDtypeStruct((t, d), F32),
            jax.ShapeDtypeStruct((t, PACK_W), jnp.uint32),
            jax.ShapeDtypeStruct((TOP_K, t), jnp.int32),
            jax.ShapeDtypeStruct((TOP_K, t), F32),
            jax.ShapeDtypeStruct((TOP_K, t), jnp.int32),
            jax.ShapeDtypeStruct((N_EXPERTS, 128), jnp.int32),
        ],
        scratch_shapes=[pltpu.VMEM((N_EXPERTS, 128), F32)],
        compiler_params=_params(("arbitrary",)),
        name="route",
    )(mixed, x, w_o, g_ffn, w_router_t, b_router)


MOE_BM = 256
DISPATCH_TM = 256


def _dispatch_body(dest_ref, padrow_ref, npad_ref, nused_ref, xp_ref, xs_hbm, zblk, sem, zsem):
    tm = xp_ref.shape[0]
    bm = zblk.shape[0]
    nb = xs_hbm.shape[0] // bm

    @pl.when(pl.program_id(0) == 0)
    def _():
        zblk[...] = jnp.zeros_like(zblk)
        npad = npad_ref[0]
        nused = nused_ref[0]

        def zero_row(j):
            return pltpu.make_async_copy(zblk.at[pl.ds(0, 1), :], xs_hbm.at[pl.ds(padrow_ref[j], 1), :], zsem)

        def zero_block(b):
            return pltpu.make_async_copy(zblk, xs_hbm.at[pl.ds(pl.multiple_of(b * bm, bm), bm), :], zsem)

        def start_row(j, c):
            zero_row(j).start()
            return c

        def wait_row(j, c):
            zero_row(j).wait()
            return c

        def start_block(b, c):
            zero_block(b).start()
            return c

        def wait_block(b, c):
            zero_block(b).wait()
            return c

        lax.fori_loop(0, npad, start_row, 0)
        lax.fori_loop(nused, nb, start_block, 0)
        lax.fori_loop(0, npad, wait_row, 0)
        lax.fori_loop(nused, nb, wait_block, 0)

    def row_copy(r, k):
        return pltpu.make_async_copy(xp_ref.at[pl.ds(r, 1), :],
                                     xs_hbm.at[pl.ds(dest_ref[0, 0, k * tm + r], 1), :], sem)

    def start(r, c):
        for k in range(TOP_K):
            row_copy(r, k).start(priority=k % 2)
        return c

    lax.fori_loop(0, tm, start, 0)
    for k in range(TOP_K):
        pltpu.make_async_copy(xp_ref, xs_hbm.at[pl.ds(0, tm), :], sem).wait()


def _dispatch(xp, dest_tiles, pad_rows, n_pad, n_used, n_rows):
    t = xp.shape[0]
    tm = dest_tiles.shape[2] // TOP_K
    return pl.pallas_call(
        _dispatch_body,
        grid=(t // tm,),
        in_specs=[
            pl.BlockSpec((1, 1, TOP_K * tm), lambda i: (i, 0, 0), memory_space=pltpu.SMEM),
            pl.BlockSpec(memory_space=pltpu.SMEM),
            pl.BlockSpec(memory_space=pltpu.SMEM),
            pl.BlockSpec(memory_space=pltpu.SMEM),
            pl.BlockSpec((tm, PACK_W), lambda i: (i, 0)),
        ],
        out_specs=pl.BlockSpec(memory_space=pltpu.HBM),
        out_shape=jax.ShapeDtypeStruct((n_rows, PACK_W), jnp.uint32),
        scratch_shapes=[
            pltpu.VMEM((MOE_BM, PACK_W), jnp.uint32),
            pltpu.SemaphoreType.DMA(()),
            pltpu.SemaphoreType.DMA(()),
        ],
        compiler_params=_params(("arbitrary",)),
        name="dispatch",
    )(dest_tiles, pad_rows, n_pad, n_used, xp)


FFN_TF = 1024
FFN_TN = 2048
FFN_CHUNK = 2048
N_SLOTS = 2
IN_SLOTS = 3
STREAM_PRIORITY = 1


def _stream_expert_blocks(first_ref, nblk_ref, nused_ref, src_hbm, dst_hbm, ibuf, obuf, isem, osem,
                          prepare, compute):
    j, e = pl.program_id(0), pl.program_id(1)
    n_in = ibuf.shape[0]
    ahead = n_in - 1
    bm_in = ibuf.shape[1]
    n_chunks, bm_out, cw = obuf.shape[1:]
    width = n_chunks * cw
    half = bm_in // 2
    col = pl.multiple_of(j * width, width)
    nblk = nblk_ref[e]
    first = first_ref[e]

    def fetch(b, slot):
        src = src_hbm.at[pl.ds(pl.multiple_of(b * bm_in, bm_in), bm_in), :]
        return pltpu.make_async_copy(src, ibuf.at[slot], isem.at[slot])

    def fetch_start(b, slot):
        for p in range(2):
            src = src_hbm.at[pl.ds(pl.multiple_of(b * bm_in + p * half, half), half), :]
            pltpu.make_async_copy(src, ibuf.at[slot, pl.ds(p * half, half), :], isem.at[slot]).start(priority=p)

    def flush(b, slot):
        out_rows = pl.ds(pl.multiple_of(b * bm_out, bm_out), bm_out)
        return [pltpu.make_async_copy(obuf.at[slot, n], dst_hbm.at[out_rows, pl.ds(col + n * cw, cw)], osem.at[slot])
                for n in range(n_chunks)]

    def flush_start(b, slot):
        for cp in flush(b, slot):
            cp.start(priority=STREAM_PRIORITY)

    def flush_wait(b, slot):
        for cp in flush(b, slot):
            cp.wait()

    for d in range(ahead):
        @pl.when(nblk > d)
        def _():
            fetch_start(first + d, d)

    @pl.when(nblk > 0)
    def _():
        prepare()

    def step(i, carry):
        slot = lax.rem(i, n_in)
        oslot = lax.rem(i, N_SLOTS)
        fetch(first + i, slot).wait()

        @pl.when(i + ahead < nblk)
        def _():
            fetch_start(first + i + ahead, lax.rem(i + ahead, n_in))

        @pl.when(i >= N_SLOTS)
        def _():
            flush_wait(first + i - N_SLOTS, oslot)

        compute(ibuf[slot], obuf.at[oslot])
        flush_start(first + i, oslot)
        return carry

    lax.fori_loop(0, nblk, step, 0)

    @pl.when(nblk >= 2)
    def _():
        flush_wait(first + nblk - 2, lax.rem(nblk, N_SLOTS))

    @pl.when(nblk >= 1)
    def _():
        flush_wait(first + nblk - 1, lax.rem(nblk - 1, N_SLOTS))

    @pl.when(e == pl.num_programs(1) - 1)
    def _():
        obuf[0] = jnp.zeros(obuf.shape[1:], obuf.dtype)

        def zero(b, carry):
            flush_start(b, 0)
            flush_wait(b, 0)
            return carry

        lax.fori_loop(nused_ref[0], dst_hbm.shape[0] // bm_out, zero, 0)


def _ffn_up_body(first_ref, nblk_ref, nused_ref, xs_hbm, wg_ref, wu_ref, bg_ref, bu_ref, h_hbm,
                 wg_bf, wu_bf, ibuf, obuf, isem, osem):
    def prepare():
        wg_bf[...] = wg_ref[...].astype(BF16)
        wu_bf[...] = wu_ref[...].astype(BF16)

    def compute(packed, out_ref):
        x = _unpack_rows(packed)
        gate = jnp.dot(x, wg_bf[...], preferred_element_type=F32) + bg_ref[...]
        up = jnp.dot(x, wu_bf[...], preferred_element_type=F32) + bu_ref[...]
        gate = jnp.minimum(gate, SWIGLU_LIMIT)
        up = jnp.clip(up, -SWIGLU_LIMIT, SWIGLU_LIMIT)
        act = ((up + 1.0) * gate * _sigmoid(SWIGLU_ALPHA * gate)).astype(BF16)
        out_ref[0] = pltpu.bitcast(act, jnp.uint32)

    _stream_expert_blocks(first_ref, nblk_ref, nused_ref, xs_hbm, h_hbm, ibuf, obuf, isem, osem,
                          prepare, compute)


def _ffn_up(first_blk, n_blk, n_used, xs, w_gate_up, b_gate_up):
    n_rows = xs.shape[0]
    tf = FFN_TF
    nf = D_FF // tf
    return pl.pallas_call(
        _ffn_up_body,
        grid_spec=pltpu.PrefetchScalarGridSpec(
            num_scalar_prefetch=3,
            grid=(nf, N_EXPERTS),
            in_specs=[
                pl.BlockSpec(memory_space=pltpu.HBM),
                pl.BlockSpec((None, D_MODEL, tf), lambda j, e, *_: (e, 0, j)),
                pl.BlockSpec((None, D_MODEL, tf), lambda j, e, *_: (e, 0, nf + j)),
                pl.BlockSpec((None, 1, tf), lambda j, e, *_: (e, 0, j)),
                pl.BlockSpec((None, 1, tf), lambda j, e, *_: (e, 0, nf + j)),
            ],
            out_specs=pl.BlockSpec(memory_space=pltpu.HBM),
            scratch_shapes=[
                pltpu.VMEM((D_MODEL, tf), BF16),
                pltpu.VMEM((D_MODEL, tf), BF16),
                pltpu.VMEM((IN_SLOTS, MOE_BM, PACK_W), jnp.uint32),
                pltpu.VMEM((N_SLOTS, 1, MOE_BM // 2, tf), jnp.uint32),
                pltpu.SemaphoreType.DMA((IN_SLOTS,)),
                pltpu.SemaphoreType.DMA((N_SLOTS,)),
            ],
        ),
        out_shape=jax.ShapeDtypeStruct((n_rows // 2, D_FF), jnp.uint32),
        compiler_params=_params(("arbitrary", "arbitrary")),
        name="ffn_up",
    )(first_blk, n_blk, n_used, xs, w_gate_up, w_gate_up, b_gate_up, b_gate_up)


def _ffn_down_body(first_ref, nblk_ref, nused_ref, h_hbm, wd_ref, bd_ref, y_hbm, wd_bf, ibuf, obuf, isem, osem):
    n_chunks, _, cw = wd_bf.shape

    def prepare():
        for n in range(n_chunks):
            wd_bf[n] = wd_ref[:, n * cw:(n + 1) * cw].astype(BF16)

    def compute(paired, out_ref):
        hid = pltpu.bitcast(paired, BF16)

        def chunk(n, carry):
            out_ref[n] = jnp.dot(hid, wd_bf[n], preferred_element_type=F32) + bd_ref[n]
            return carry

        lax.fori_loop(0, n_chunks, chunk, 0)

    _stream_expert_blocks(first_ref, nblk_ref, nused_ref, h_hbm, y_hbm, ibuf, obuf, isem, osem,
                          prepare, compute)


def _ffn_down(first_blk, n_blk, n_used, h, w_down, b_down):
    n_rows = 2 * h.shape[0]
    tn = FFN_TN
    cw = FFN_CHUNK
    nc = tn // cw
    return pl.pallas_call(
        _ffn_down_body,
        grid_spec=pltpu.PrefetchScalarGridSpec(
            num_scalar_prefetch=3,
            grid=(D_MODEL // tn, N_EXPERTS),
            in_specs=[
                pl.BlockSpec(memory_space=pltpu.HBM),
                pl.BlockSpec((None, D_FF, tn), lambda j, e, *_: (e, 0, j)),
                pl.BlockSpec((None, nc, 1, cw), lambda j, e, *_: (e, j, 0, 0)),
            ],
            out_specs=pl.BlockSpec(memory_space=pltpu.HBM),
            scratch_shapes=[
                pltpu.VMEM((nc, D_FF, cw), BF16),
                pltpu.VMEM((IN_SLOTS, MOE_BM // 2, D_FF), jnp.uint32),
                pltpu.VMEM((N_SLOTS, nc, MOE_BM, cw), F32),
                pltpu.SemaphoreType.DMA((IN_SLOTS,)),
                pltpu.SemaphoreType.DMA((N_SLOTS,)),
            ],
        ),
        out_shape=jax.ShapeDtypeStruct((n_rows, D_MODEL), F32),
        compiler_params=_params(("arbitrary", "arbitrary")),
        name="ffn_down",
    )(first_blk, n_blk, n_used, h, w_down, b_down)


def _expert_changed(be_ref, b):
    return jnp.logical_or(b == 0, be_ref[b] != be_ref[jnp.maximum(b - 1, 0)])


def _ffn_up_flat_body(be_ref, nb_ref, xs_ref, wg_ref, wu_ref, bg_ref, bu_ref, h_ref, wg_bf, wu_bf):
    b = pl.program_id(1)

    @pl.when(b < nb_ref[0])
    def _():
        @pl.when(_expert_changed(be_ref, b))
        def _():
            wg_bf[...] = wg_ref[...].astype(BF16)
            wu_bf[...] = wu_ref[...].astype(BF16)

        x = _unpack_rows(xs_ref[...])
        gate = jnp.dot(x, wg_bf[...], preferred_element_type=F32) + bg_ref[...]
        up = jnp.dot(x, wu_bf[...], preferred_element_type=F32) + bu_ref[...]
        gate = jnp.minimum(gate, SWIGLU_LIMIT)
        up = jnp.clip(up, -SWIGLU_LIMIT, SWIGLU_LIMIT)
        act = ((up + 1.0) * gate * _sigmoid(SWIGLU_ALPHA * gate)).astype(BF16)
        h_ref[...] = pltpu.bitcast(act, jnp.uint32)

    @pl.when(b >= nb_ref[0])
    def _():
        h_ref[...] = jnp.zeros_like(h_ref)


def _ffn_up_flat(block_e, n_used, xs, w_gate_up, b_gate_up):
    n_rows = xs.shape[0]
    nb = n_rows // MOE_BM
    tf = FFN_TF
    nf = D_FF // tf
    live = lambda b, nbr: jnp.minimum(b, nbr[0] - 1)
    return pl.pallas_call(
        _ffn_up_flat_body,
        grid_spec=pltpu.PrefetchScalarGridSpec(
            num_scalar_prefetch=2,
            grid=(nf, nb),
            in_specs=[
                pl.BlockSpec((MOE_BM, PACK_W), lambda j, b, be, nbr: (live(b, nbr), 0)),
                pl.BlockSpec((None, D_MODEL, tf), lambda j, b, be, nbr: (be[live(b, nbr)], 0, j)),
                pl.BlockSpec((None, D_MODEL, tf), lambda j, b, be, nbr: (be[live(b, nbr)], 0, nf + j)),
                pl.BlockSpec((None, 1, tf), lambda j, b, be, nbr: (be[live(b, nbr)], 0, j)),
                pl.BlockSpec((None, 1, tf), lambda j, b, be, nbr: (be[live(b, nbr)], 0, nf + j)),
            ],
            out_specs=pl.BlockSpec((MOE_BM // 2, tf), lambda j, b, be, nbr: (b, j)),
            scratch_shapes=[pltpu.VMEM((D_MODEL, tf), BF16), pltpu.VMEM((D_MODEL, tf), BF16)],
        ),
        out_shape=jax.ShapeDtypeStruct((n_rows // 2, D_FF), jnp.uint32),
        compiler_params=_params(("arbitrary", "arbitrary")),
        name="ffn_up",
    )(block_e, n_used, xs, w_gate_up, w_gate_up, b_gate_up, b_gate_up)


def _ffn_down_flat_body(be_ref, nb_ref, h_ref, wd_ref, bd_ref, y_ref, wd_bf):
    b = pl.program_id(1)

    @pl.when(b < nb_ref[0])
    def _():
        @pl.when(_expert_changed(be_ref, b))
        def _():
            wd_bf[...] = wd_ref[...].astype(BF16)

        hid = pltpu.bitcast(h_ref[...], BF16)
        y_ref[...] = jnp.dot(hid, wd_bf[...], preferred_element_type=F32) + bd_ref[...]

    @pl.when(b >= nb_ref[0])
    def _():
        y_ref[...] = jnp.zeros_like(y_ref)


def _ffn_down_flat(block_e, n_used, h, w_down, b_down):
    n_rows = 2 * h.shape[0]
    nb = n_rows // MOE_BM
    tn = FFN_TN
    live = lambda b, nbr: jnp.minimum(b, nbr[0] - 1)
    return pl.pallas_call(
        _ffn_down_flat_body,
        grid_spec=pltpu.PrefetchScalarGridSpec(
            num_scalar_prefetch=2,
            grid=(D_MODEL // tn, nb),
            in_specs=[
                pl.BlockSpec((MOE_BM // 2, D_FF), lambda j, b, be, nbr: (live(b, nbr), 0)),
                pl.BlockSpec((None, D_FF, tn), lambda j, b, be, nbr: (be[live(b, nbr)], 0, j)),
                pl.BlockSpec((None, 1, tn), lambda j, b, be, nbr: (be[live(b, nbr)], 0, j)),
            ],
            out_specs=pl.BlockSpec((MOE_BM, tn), lambda j, b, be, nbr: (b, j)),
            scratch_shapes=[pltpu.VMEM((D_FF, tn), BF16)],
        ),
        out_shape=jax.ShapeDtypeStruct((n_rows, D_MODEL), F32),
        compiler_params=_params(("arbitrary", "arbitrary")),
        name="ffn_down",
    )(block_e, n_used, h, w_down, b_down)


COMBINE_TM = 512


def _combine_body(dest_ref, next_ref, h_ref, w_ref, g_ref, y_hbm, o_ref, gbuf, sem):
    i = pl.program_id(0)
    tm = h_ref.shape[0]

    def gather_tile(table_ref, slot):
        def start(r, c):
            for k in range(TOP_K):
                pltpu.make_async_copy(y_hbm.at[pl.ds(table_ref[0, 0, k * tm + r], 1), :],
                                      gbuf.at[slot, k, pl.ds(r, 1), :], sem.at[slot]).start(priority=k % 2)
            return c

        lax.fori_loop(0, tm, start, 0)

    slot = lax.rem(i, N_SLOTS)

    @pl.when(i == 0)
    def _():
        gather_tile(dest_ref, 0)

    @pl.when(i + 1 < pl.num_programs(0))
    def _():
        gather_tile(next_ref, 1 - slot)

    for k in range(TOP_K):
        pltpu.make_async_copy(y_hbm.at[pl.ds(0, tm), :], gbuf.at[slot, k], sem.at[slot]).wait()
    h = h_ref[...]
    for k in range(TOP_K):
        h = h + w_ref[:, k:k + 1] * gbuf[slot, k]
    o_ref[...] = h * lax.rsqrt(jnp.mean(h * h, axis=-1, keepdims=True) + EPS) * g_ref[...]


def _combine(dest_tiles, h1, w_cols, g_final, y):
    t = h1.shape[0]
    tm = dest_tiles.shape[2] // TOP_K
    d = D_MODEL
    last = t // tm - 1
    return pl.pallas_call(
        _combine_body,
        grid=(t // tm,),
        in_specs=[
            pl.BlockSpec((1, 1, TOP_K * tm), lambda i: (i, 0, 0), memory_space=pltpu.SMEM),
            pl.BlockSpec((1, 1, TOP_K * tm), lambda i: (jnp.minimum(i + 1, last), 0, 0), memory_space=pltpu.SMEM),
            pl.BlockSpec((tm, d), lambda i: (i, 0)),
            pl.BlockSpec((tm, TOP_K), lambda i: (i, 0)),
            pl.BlockSpec((1, d), lambda i: (0, 0)),
            pl.BlockSpec(memory_space=pltpu.HBM),
        ],
        out_specs=pl.BlockSpec((tm, d), lambda i: (i, 0)),
        out_shape=jax.ShapeDtypeStruct((t, d), F32),
        scratch_shapes=[pltpu.VMEM((N_SLOTS, TOP_K, tm, d), F32), pltpu.SemaphoreType.DMA((N_SLOTS,))],
        compiler_params=_params(("arbitrary",)),
        name="combine",
    )(dest_tiles, dest_tiles, h1, w_cols, g_final, y)


def _tile_major(a, tm):
    k, t = a.shape
    return a.reshape(k, t // tm, tm).transpose(1, 0, 2).reshape(t // tm, 1, k * tm)


def _routing_tables(top_e, rank, counts, t):
    bm = MOE_BM
    nb = (t * TOP_K) // bm + N_EXPERTS
    padded = (counts + bm - 1) // bm * bm
    pad_end = jnp.cumsum(padded)
    pad_start = pad_end - padded
    onehot = top_e[:, :, None] == jnp.arange(N_EXPERTS, dtype=jnp.int32)
    dest = rank + jnp.sum(jnp.where(onehot, pad_start, 0), axis=-1)
    first_blk = (pad_start // bm).astype(jnp.int32)
    n_blk = (padded // bm).astype(jnp.int32)
    n_used = (pad_end[-1] // bm).astype(jnp.int32).reshape(1)
    gap = padded - counts
    gap_end = jnp.cumsum(gap)
    j = jnp.arange(N_EXPERTS * bm, dtype=jnp.int32)
    ej = jnp.minimum(jnp.sum(j[:, None] >= gap_end[None, :], axis=1), N_EXPERTS - 1)
    pad_rows = (pad_start + counts)[ej] + j - (gap_end - gap)[ej]
    pad_rows = jnp.clip(pad_rows, 0, nb * bm - 1).astype(jnp.int32)
    n_pad = gap_end[-1].astype(jnp.int32).reshape(1)
    return dest.astype(jnp.int32), first_blk, n_blk, n_used, pad_rows, n_pad, nb * bm


def kernel(x, g_mix, w_in, ssm_conv_w, ssm_conv_b, ssm_dt_bias, ssm_a_log, ssm_d, ssm_norm_g, w_ssm_out,
           sc_conv_w, w_sc_out, b_gate, w_o, g_ffn, w_router, b_router, w_gate_up, b_gate_up, w_down,
           b_down, g_final):
    bsz, seq, d = x.shape
    t = bsz * seq
    assert bsz == 1 and d == D_MODEL and w_in.shape[0] == 1
    xt = x.reshape(t, d)
    w_dt = w_in[0, :, OFF_DT:OFF_SC]
    w_dt_hi = w_dt.astype(BF16)
    w_dt = jnp.concatenate([w_dt_hi, (w_dt - w_dt_hi.astype(F32)).astype(BF16)], axis=1)
    col = lambda a: a.reshape(-1, 1)
    row = lambda a: a.reshape(1, -1)

    u, dt_raw = _prenorm(xt, row(g_mix[0]), w_dt)
    proj = _inproj(u, w_in[0].T)
    y_norm = _ssd(proj, dt_raw.T, ssm_conv_w[0], row(ssm_conv_b[0]), col(ssm_dt_bias[0]), col(ssm_a_log[0]),
                  col(ssm_d[0]), row(ssm_norm_g[0]))
    mixed = _mix(y_norm, proj, row(b_gate[0]), sc_conv_w[0], w_ssm_out[0].astype(BF16), w_sc_out[0].astype(BF16))
    wr_t = w_router[0].T
    wr_hi = wr_t.astype(BF16)
    wr_lo = (wr_t - wr_hi.astype(F32)).astype(BF16)
    h1, xp, top_e, top_w, rank, counts = _route(mixed, xt, w_o[0].astype(BF16), row(g_ffn[0]),
                                                jnp.concatenate([wr_hi, wr_lo], axis=0), col(b_router[0]))
    dest, first_blk, n_blk, n_used, pad_rows, n_pad, n_rows = _routing_tables(top_e, rank, counts[:, 0], t)
    xs = _dispatch(xp, _tile_major(dest, min(DISPATCH_TM, t)), pad_rows, n_pad, n_used, n_rows)
    blocks = jnp.arange(n_rows // MOE_BM, dtype=jnp.int32)
    block_e = jnp.clip(jnp.sum(blocks[:, None] >= (first_blk + n_blk)[None, :], axis=1), 0, N_EXPERTS - 1).astype(jnp.int32)
    hid = _ffn_up_flat(block_e, n_used, xs, w_gate_up[0], b_gate_up[0].reshape(N_EXPERTS, 1, 2 * D_FF))
    y = _ffn_down_flat(block_e, n_used, hid, w_down[0], b_down[0].reshape(N_EXPERTS, 1, D_MODEL))
    out = _combine(_tile_major(dest, min(COMBINE_TM, t)), h1, top_w.T, row(g_final), y)
    return out.reshape(bsz, seq, d)
```
